```python
import jax, jax.numpy as jnp
from jax import lax
import numpy as np

D_MODEL = 2048
BATCH = 16
SEQ = 2048
DEPTH = 1
DEC_BATCH = 32
DEC_SEQ = 32
PAST_LEN = 2048

CHUNK = 64
LEFT_CHUNKS = 8
LEFT_LEN = LEFT_CHUNKS * CHUNK
BAND = (LEFT_CHUNKS + 1) * CHUNK
N_HEADS = 16
HEAD_DIM = 64
ATTN_DIM = N_HEADS * HEAD_DIM
REL_CLIP = 128
N_REL = 2 * REL_CLIP + 1
CONV_DIM = D_MODEL // 2
CONV_WIDTH = 3
D_FF = 4 * D_MODEL
D_PLE = 256
IN_COLS = 3 * ATTN_DIM + 3 * CONV_DIM + 2 * D_MODEL
EPS = 1e-6

kernel_name = "hybrid_chunk_band_attn_shortconv_stream_step"


def rms_norm(x, g):
    xf = x.astype(jnp.float32)
    y = xf * lax.rsqrt(jnp.mean(xf * xf, axis=-1, keepdims=True) + EPS)
    return (y * g.astype(jnp.float32)).astype(x.dtype)


def split_proj(n, w_in):
    z = n @ w_in
    sizes = [ATTN_DIM, ATTN_DIM, ATTN_DIM, CONV_DIM, CONV_DIM, CONV_DIM, D_MODEL]
    idx = [int(i) for i in np.cumsum(sizes)]
    return jnp.split(z, idx, axis=-1)


def rel_bias(rel_table, rel):
    return rel_table[:, jnp.clip(rel, -REL_CLIP, REL_CLIP) + REL_CLIP]


def attend(q, k, v, bias, mask):
    s = jnp.einsum('bqhd,bkhd->bhqk', q, k).astype(jnp.float32) * (HEAD_DIM ** -0.5)
    s = s + bias.astype(jnp.float32)
    if mask is not None:
        s = jnp.where(mask, s, -1e30)
    p = jax.nn.softmax(s, axis=-1).astype(v.dtype)
    return jnp.einsum('bhqk,bkhd->bqhd', p, v)


def chunk_band_attention_prompt(q, k, v, rel_table):
    B, T = q.shape[0], q.shape[1]
    nc = T // CHUNK
    kp = jnp.pad(k, ((0, 0), (LEFT_LEN, 0), (0, 0), (0, 0)))
    vp = jnp.pad(v, ((0, 0), (LEFT_LEN, 0), (0, 0), (0, 0)))
    qi = jnp.arange(CHUNK)[:, None]
    kj = jnp.arange(BAND)[None, :]
    bias = rel_bias(rel_table, kj - LEFT_LEN - qi)
    qc = q.reshape(B, nc, CHUNK, N_HEADS, HEAD_DIM).transpose(1, 0, 2, 3, 4)

    def one_chunk(args):
        c, qb = args
        start = c * CHUNK
        kb = lax.dynamic_slice_in_dim(kp, start, BAND, axis=1)
        vb = lax.dynamic_slice_in_dim(vp, start, BAND, axis=1)
        mask = (start - LEFT_LEN + kj) >= 0
        return attend(qb, kb, vb, bias, mask)

    out = lax.map(one_chunk, (jnp.arange(nc), qc))
    return out.transpose(1, 0, 2, 3, 4).reshape(B, T, ATTN_DIM)


def chunk_band_attention_sample(q, k_new, v_new, k_cache, v_cache, rel_table):
    B, S = q.shape[0], q.shape[1]
    lc = k_cache.shape[1]
    k = jnp.concatenate([k_cache.astype(k_new.dtype), k_new], axis=1)
    v = jnp.concatenate([v_cache.astype(v_new.dtype), v_new], axis=1)
    qpos = jnp.arange(S)[:, None]
    kpos = jnp.arange(lc + S)[None, :] - lc
    bias = rel_bias(rel_table, kpos - qpos)
    return attend(q, k, v, bias, None).reshape(B, S, ATTN_DIM)


def causal_conv(u_ext, w, b):
    T = u_ext.shape[1] - (CONV_WIDTH - 1)
    y = b
    for j in range(CONV_WIDTH):
        y = y + w[j] * u_ext[:, j:j + T]
    return y


def layer_forward(x, pe, attn_fn, conv_left, w_in, w_attn_out, conv_w, conv_b, w_conv_out, w_o,
                  g_pre_mix, g_post_mix, g_pre_mlp, g_post_mlp, w_up, w_down, w_pe, w_pe_gate, g_pe):
    B, T = x.shape[0], x.shape[1]
    n = rms_norm(x, g_pre_mix)
    q, k, v, cb, cc, cx, ga, gb = split_proj(n, w_in)
    q = q.reshape(B, T, N_HEADS, HEAD_DIM)
    k = k.reshape(B, T, N_HEADS, HEAD_DIM)
    v = v.reshape(B, T, N_HEADS, HEAD_DIM)
    ya = attn_fn(q, k, v) @ w_attn_out
    u = cc * cx
    u_ext = jnp.concatenate([conv_left.astype(u.dtype), u], axis=1)
    yb = (cb * causal_conv(u_ext, conv_w, conv_b)) @ w_conv_out
    mix = (jax.nn.sigmoid(ga) * ya + jax.nn.sigmoid(gb) * yb) @ w_o
    h = x + rms_norm(mix, g_post_mix)
    f = jnp.square(jax.nn.relu(rms_norm(h, g_pre_mlp) @ w_up)) @ w_down
    h = h + rms_norm(f, g_post_mlp)
    gate = jax.nn.sigmoid(h @ w_pe_gate)
    h = h + rms_norm(gate * (pe @ w_pe), g_pe)
    conv_tail = u_ext[:, -(CONV_WIDTH - 1):]
    return h, k, v, conv_tail


def setup_inputs(seed: int = 0) -> dict:
    key = jax.random.key(seed)
    ks = jax.random.split(key, 32)
    f32 = jnp.float32
    lc = min(LEFT_LEN, PAST_LEN)

    def nrm(k, shape, scale):
        return jax.random.normal(k, shape, f32) * scale

    def gain(k):
        return 1.0 + 0.05 * jax.random.normal(k, (DEPTH, D_MODEL), f32)

    return {
        "x_prompt": nrm(ks[0], (BATCH, SEQ, D_MODEL), 1.0),
        "x_sample": nrm(ks[1], (DEC_BATCH, DEC_SEQ, D_MODEL), 1.0),
        "cache_k": nrm(ks[2], (DEPTH, DEC_BATCH, lc, N_HEADS, HEAD_DIM), 1.0),
        "cache_v": nrm(ks[3], (DEPTH, DEC_BATCH, lc, N_HEADS, HEAD_DIM), 1.0),
        "state_conv": nrm(ks[4], (DEPTH, DEC_BATCH, CONV_WIDTH - 1, CONV_DIM), 1.0),
        "p_prompt": nrm(ks[5], (DEPTH, BATCH, SEQ, D_PLE), 1.0),
        "p_sample": nrm(ks[6], (DEPTH, DEC_BATCH, DEC_SEQ, D_PLE), 1.0),
        "w_in": nrm(ks[7], (DEPTH, D_MODEL, IN_COLS), D_MODEL ** -0.5),
        "rel_table": nrm(ks[8], (DEPTH, N_HEADS, N_REL), 0.2),
        "w_attn_out": nrm(ks[9], (DEPTH, ATTN_DIM, D_MODEL), ATTN_DIM ** -0.5),
        "conv_w": nrm(ks[10], (DEPTH, CONV_WIDTH, CONV_DIM), CONV_WIDTH ** -0.5),
        "conv_b": nrm(ks[11], (DEPTH, CONV_DIM), 0.01),
        "w_conv_out": nrm(ks[12], (DEPTH, CONV_DIM, D_MODEL), CONV_DIM ** -0.5),
        "w_o": nrm(ks[13], (DEPTH, D_MODEL, D_MODEL), D_MODEL ** -0.5),
        "g_pre_mix": gain(ks[14]),
        "g_post_mix": gain(ks[15]),
        "g_pre_mlp": gain(ks[16]),
        "g_post_mlp": gain(ks[17]),
        "w_up": nrm(ks[18], (DEPTH, D_MODEL, D_FF), D_MODEL ** -0.5),
        "w_down": nrm(ks[19], (DEPTH, D_FF, D_MODEL), D_FF ** -0.5),
        "w_pe": nrm(ks[20], (DEPTH, D_PLE, D_MODEL), D_PLE ** -0.5),
        "w_pe_gate": nrm(ks[21], (DEPTH, D_MODEL, D_MODEL), D_MODEL ** -0.5),
        "g_pe": gain(ks[22]),
    }


def reference(x_prompt, x_sample, cache_k, cache_v, state_conv, p_prompt, p_sample,
              w_in, rel_table, w_attn_out, conv_w, conv_b, w_conv_out, w_o,
              g_pre_mix, g_post_mix, g_pre_mlp, g_post_mlp, w_up, w_down, w_pe, w_pe_gate, g_pe):
    hp, hs = x_prompt, x_sample
    kp_l, vp_l, cp_l, ks_l, vs_l, cs_l = [], [], [], [], [], []
    lp = min(LEFT_LEN, x_prompt.shape[1])
    for i in range(DEPTH):
        w = (w_in[i], w_attn_out[i], conv_w[i], conv_b[i], w_conv_out[i], w_o[i],
             g_pre_mix[i], g_post_mix[i], g_pre_mlp[i], g_post_mlp[i],
             w_up[i], w_down[i], w_pe[i], w_pe_gate[i], g_pe[i])
        rt = rel_table[i]

        def attn_prompt(q, k, v, rt=rt):
            return chunk_band_attention_prompt(q, k, v, rt)

        def attn_sample(q, k, v, rt=rt, kc=cache_k[i], vc=cache_v[i]):
            return chunk_band_attention_sample(q, k, v, kc, vc, rt)

        zero_left = jnp.zeros((hp.shape[0], CONV_WIDTH - 1, CONV_DIM), hp.dtype)
        hp, kpr, vpr, cpr = layer_forward(hp, p_prompt[i], attn_prompt, zero_left, *w)
        hs, ksm, vsm, csm = layer_forward(hs, p_sample[i], attn_sample, state_conv[i], *w)
        kp_l.append(kpr[:, -lp:]); vp_l.append(vpr[:, -lp:]); cp_l.append(cpr)
        ks_l.append(ksm); vs_l.append(vsm); cs_l.append(csm)
    k_prompt = jnp.stack(kp_l); v_prompt = jnp.stack(vp_l); conv_prompt = jnp.stack(cp_l)
    k_sample = jnp.stack(ks_l); v_sample = jnp.stack(vs_l); conv_sample = jnp.stack(cs_l)
    return (hp, hs, k_prompt, v_prompt, conv_prompt, k_sample, v_sample, conv_sample)
```

```python
import functools

import jax
import jax.numpy as jnp
from jax import lax
from jax.experimental import pallas as pl
from jax.experimental.pallas import tpu as pltpu

F32 = jnp.float32
BF16 = jnp.bfloat16

CHUNK = 64
LEFT_CHUNKS = 8
LEFT_LEN = LEFT_CHUNKS * CHUNK
HEAD_DIM = 64
REL_CLIP = 128
EPS = 1e-6
NEG = -1e30

LANES = 128
ATTN_TQ = 256
ATTN_SLOTS = LEFT_LEN // ATTN_TQ + 1
MIB = 1024 * 1024


def _rms(x, g):
    return x * lax.rsqrt(jnp.mean(x * x, axis=-1, keepdims=True) + EPS) * g


def _params(sem, vmem_mib):
    return pltpu.CompilerParams(dimension_semantics=sem, vmem_limit_bytes=vmem_mib * MIB)


def _in_proj_kernel(x_ref, g_ref, w_ref, qkv_ref, kv_ref, cb_ref, u_ref, gates_ref, n_ref, cc_ref):
    j = pl.program_id(1)

    @pl.when(j == 0)
    def _():
        n_ref[...] = _rms(x_ref[...], g_ref[...]).astype(BF16)

    z = jnp.dot(n_ref[...], w_ref[...], preferred_element_type=F32)

    @pl.when(j == 0)
    def _():
        qkv_ref[...] = (z * (HEAD_DIM ** -0.5)).astype(BF16)

    @pl.when((j == 1) | (j == 2))
    def _():
        qkv_ref[...] = z.astype(BF16)
        kv_ref[...] = z

    @pl.when(j == 3)
    def _():
        cb_ref[...] = z.astype(BF16)

    @pl.when(j == 4)
    def _():
        cc_ref[...] = z

    @pl.when(j == 5)
    def _():
        u_ref[...] = cc_ref[...] * z

    @pl.when(j >= 6)
    def _():
        gates_ref[...] = jax.nn.sigmoid(z).astype(BF16)


def _in_proj(x, g, w_in, tm):
    t, d = x.shape
    tn = 1024
    nj = w_in.shape[1] // tn
    assert nj == 10 and t % tm == 0
    out_shape = (
        jax.ShapeDtypeStruct((t, 3 * tn), BF16),
        jax.ShapeDtypeStruct((t, 2 * tn), F32),
        jax.ShapeDtypeStruct((t, tn), BF16),
        jax.ShapeDtypeStruct((t, tn), F32),
        jax.ShapeDtypeStruct((t, 4 * tn), BF16),
    )
    return pl.pallas_call(
        _in_proj_kernel,
        out_shape=out_shape,
        grid=(t // tm, nj),
        in_specs=[
            pl.BlockSpec((tm, d), lambda i, j: (i, 0)),
            pl.BlockSpec((1, d), lambda i, j: (0, 0)),
            pl.BlockSpec((d, tn), lambda i, j: (0, j)),
        ],
        out_specs=(
            pl.BlockSpec((tm, tn), lambda i, j: (i, jnp.minimum(j, 2))),
            pl.BlockSpec((tm, tn), lambda i, j: (i, jnp.clip(j - 1, 0, 1))),
            pl.BlockSpec((tm, tn), lambda i, j: (i, 0)),
            pl.BlockSpec((tm, tn), lambda i, j: (i, 0)),
            pl.BlockSpec((tm, tn), lambda i, j: (i, jnp.clip(j - 6, 0, 3))),
        ),
        scratch_shapes=[pltpu.VMEM((tm, d), BF16), pltpu.VMEM((tm, tn), F32)],
        compiler_params=_params(("arbitrary", "arbitrary"), 56),
    )(x, g, w_in)


def _softmax_pv(score_blocks, value_blocks):
    m = functools.reduce(jnp.maximum, [jnp.max(s, axis=-1, keepdims=True) for s in score_blocks])
    ps = [jnp.exp(s - m) for s in score_blocks]
    denom = functools.reduce(jnp.add, [jnp.sum(p, axis=-1, keepdims=True) for p in ps])
    acc = functools.reduce(
        jnp.add,
        [jnp.dot(p.astype(BF16), v, preferred_element_type=F32) for p, v in zip(ps, value_blocks)])
    return acc / denom


def _head_pair(q_pair, key_blocks, value_blocks, bias_fn):
    lane = lax.broadcasted_iota(jnp.int32, q_pair.shape, 1)
    outs = []
    for hh in range(2):
        own = (lane >= HEAD_DIM) if hh else (lane < HEAD_DIM)
        qm = jnp.where(own, q_pair, jnp.zeros_like(q_pair))
        scores = [
            lax.dot_general(qm, kb, (((1,), (1,)), ((), ())), preferred_element_type=F32) + bias_fn(hh, n)
            for n, kb in enumerate(key_blocks)]
        outs.append(_softmax_pv(scores, value_blocks))
    return jnp.where(lane < HEAD_DIM, outs[0], outs[1])


def _attn_prompt_kernel(q_ref, *refs):
    k_refs = refs[:ATTN_SLOTS]
    v_refs = refs[ATTN_SLOTS:2 * ATTN_SLOTS]
    bias_ref, o_ref = refs[2 * ATTN_SLOTS:]
    t = pl.program_id(1)
    nk = ATTN_SLOTS * ATTN_TQ
    col = lax.broadcasted_iota(jnp.int32, (1, nk), 1)
    left_pad = jnp.where(col + t * ATTN_TQ >= LEFT_LEN, 0.0, NEG).astype(F32)
    for hp in range(q_ref.shape[1] // LANES):
        sl = slice(hp * LANES, (hp + 1) * LANES)
        keys = jnp.concatenate([r[:, sl] for r in k_refs], axis=0)
        vals = jnp.concatenate([r[:, sl] for r in v_refs], axis=0)
        bias_fn = lambda hh, n, hp=hp: bias_ref[2 * hp + hh] + left_pad
        o_ref[:, sl] = _head_pair(q_ref[:, sl], [keys], [vals], bias_fn).astype(o_ref.dtype)


def _attn_prompt(qkv, bias, batch, seq):
    t, width = qkv.shape[0], qkv.shape[1] // 3
    nt = seq // ATTN_TQ
    blk = (ATTN_TQ, width)

    def slot_spec(s, col):
        return pl.BlockSpec(blk, lambda b, i: (b * nt + jnp.maximum(i - (ATTN_SLOTS - 1) + s, 0), col))

    return pl.pallas_call(
        _attn_prompt_kernel,
        out_shape=jax.ShapeDtypeStruct((t, width), BF16),
        grid=(batch, nt),
        in_specs=[pl.BlockSpec(blk, lambda b, i: (b * nt + i, 0))]
        + [slot_spec(s, 1) for s in range(ATTN_SLOTS)]
        + [slot_spec(s, 2) for s in range(ATTN_SLOTS)]
        + [pl.BlockSpec(bias.shape, lambda b, i: (0, 0, 0))],
        out_specs=pl.BlockSpec(blk, lambda b, i: (b * nt + i, 0)),
        compiler_params=_params(("arbitrary", "arbitrary"), 48),
    )(qkv, *([qkv] * (2 * ATTN_SLOTS)), bias)


def _attn_sample_kernel(q_ref, kn_ref, vn_ref, kc_ref, vc_ref, bias_c_ref, bias_n_ref, o_ref):
    for hp in range(q_ref.shape[1] // LANES):
        sl = slice(hp * LANES, (hp + 1) * LANES)
        keys = [kc_ref[:, sl].astype(BF16), kn_ref[:, sl]]
        vals = [vc_ref[:, sl].astype(BF16), vn_ref[:, sl]]
        bias_fn = lambda hh, n, hp=hp: (bias_c_ref, bias_n_ref)[n][2 * hp + hh]
        o_ref[:, sl] = _head_pair(q_ref[:, sl], keys, vals, bias_fn).astype(o_ref.dtype)


def _attn_sample(qkv, cache_k, cache_v, bias_c, bias_n, batch, seq):
    t, width = qkv.shape[0], qkv.shape[1] // 3
    lc = cache_k.shape[1]
    blk = (seq, width)
    return pl.pallas_call(
        _attn_sample_kernel,
        out_shape=jax.ShapeDtypeStruct((t, width), BF16),
        grid=(batch,),
        in_specs=[
            pl.BlockSpec(blk, lambda b: (b, 0)),
            pl.BlockSpec(blk, lambda b: (b, 1)),
            pl.BlockSpec(blk, lambda b: (b, 2)),
            pl.BlockSpec((None, lc, width), lambda b: (b, 0, 0)),
            pl.BlockSpec((None, lc, width), lambda b: (b, 0, 0)),
            pl.BlockSpec(bias_c.shape, lambda b: (0, 0, 0)),
            pl.BlockSpec(bias_n.shape, lambda b: (0, 0, 0)),
        ],
        out_specs=pl.BlockSpec(blk, lambda b: (b, 0)),
        compiler_params=_params(("arbitrary",), 32),
    )(qkv, qkv, qkv, cache_k, cache_v, bias_c, bias_n)


def _mix_kernel(seq_rows, attn_ref, cb_ref, u_ref, halo_ref, gates_ref, x_ref, wa_ref, wc_ref, wo_ref,
                cw_ref, cbias_ref, g_ref, o_ref):
    tm, d = x_ref.shape
    u = u_ref[...]
    if halo_ref.shape[0] == 2:
        prev2, prev1 = halo_ref[0:1, :], halo_ref[1:2, :]
    else:
        keep = ((pl.program_id(0) * tm) % seq_rows != 0).astype(F32)
        nh = halo_ref.shape[0]
        prev2, prev1 = halo_ref[nh - 2:nh - 1, :] * keep, halo_ref[nh - 1:nh, :] * keep
    row = lax.broadcasted_iota(jnp.int32, u.shape, 0)
    um1 = jnp.where(row == 0, prev1, pltpu.roll(u, 1, axis=0))
    um2 = jnp.where(row == 0, prev2, jnp.where(row == 1, prev1, pltpu.roll(u, 2, axis=0)))
    conv = cbias_ref[...] + cw_ref[0:1, :] * um2 + cw_ref[1:2, :] * um1 + cw_ref[2:3, :] * u
    ya = jnp.dot(attn_ref[...], wa_ref[...], preferred_element_type=F32)
    yb = jnp.dot((cb_ref[...].astype(F32) * conv).astype(BF16), wc_ref[...], preferred_element_type=F32)
    merged = gates_ref[:, :d].astype(F32) * ya + gates_ref[:, d:].astype(F32) * yb
    mix = jnp.dot(merged.astype(BF16), wo_ref[...], preferred_element_type=F32)
    o_ref[...] = x_ref[...] + _rms(mix, g_ref[...])


def _mix(attn, cb, u, state, gates, x, wa, wc, wo, conv_w, conv_b, g, tm, seq_rows):
    t, d = x.shape
    c = u.shape[1]
    halo_rows = 8
    if state is None:
        halo, halo_spec = u, pl.BlockSpec(
            (halo_rows, c), lambda i: (jnp.maximum(i * (tm // halo_rows) - 1, 0), 0))
    else:
        assert tm == seq_rows
        halo, halo_spec = state, pl.BlockSpec((None,) + state.shape[1:], lambda i: (i, 0, 0))
    row_spec = lambda w: pl.BlockSpec((tm, w), lambda i: (i, 0))
    resident = lambda a: pl.BlockSpec(a.shape, lambda i: (0,) * a.ndim, pipeline_mode=pl.Buffered(1))
    return pl.pallas_call(
        functools.partial(_mix_kernel, seq_rows),
        out_shape=jax.ShapeDtypeStruct((t, d), F32),
        grid=(t // tm,),
        in_specs=[row_spec(attn.shape[1]), row_spec(c), row_spec(c), halo_spec, row_spec(2 * d), row_spec(d),
                  resident(wa), resident(wc), resident(wo), resident(conv_w), resident(conv_b), resident(g)],
        out_specs=row_spec(d),
        compiler_params=_params(("arbitrary",), 48),
    )(attn, cb, u, halo, gates, x, wa, wc, wo, conv_w, conv_b, g)


def _mlp_kernel(h_ref, g1_ref, wu_ref, wd_ref, g2_ref, o_ref, n_ref, acc_ref):
    j = pl.program_id(1)

    @pl.when(j == 0)
    def _():
        n_ref[...] = _rms(h_ref[...], g1_ref[...]).astype(BF16)

    a = jnp.dot(n_ref[...], wu_ref[...], preferred_element_type=F32)
    a = jnp.square(jnp.maximum(a, 0.0)).astype(BF16)
    part = jnp.dot(a, wd_ref[...], preferred_element_type=F32)

    @pl.when(j == 0)
    def _():
        acc_ref[...] = part

    @pl.when(j > 0)
    def _():
        acc_ref[...] += part

    @pl.when(j == pl.num_programs(1) - 1)
    def _():
        o_ref[...] = h_ref[...] + _rms(acc_ref[...], g2_ref[...])


def _mlp(h, g1, w_up, w_down, g2, tm, tf):
    t, d = h.shape
    ff = w_up.shape[1]
    return pl.pallas_call(
        _mlp_kernel,
        out_shape=jax.ShapeDtypeStruct((t, d), F32),
        grid=(t // tm, ff // tf),
        in_specs=[
            pl.BlockSpec((tm, d), lambda i, j: (i, 0)),
            pl.BlockSpec((1, d), lambda i, j: (0, 0)),
            pl.BlockSpec((d, tf), lambda i, j: (0, j)),
            pl.BlockSpec((tf, d), lambda i, j: (j, 0)),
            pl.BlockSpec((1, d), lambda i, j: (0, 0)),
        ],
        out_specs=pl.BlockSpec((tm, d), lambda i, j: (i, 0)),
        scratch_shapes=[pltpu.VMEM((tm, d), BF16), pltpu.VMEM((tm, d), F32)],
        compiler_params=_params(("arbitrary", "arbitrary"), 56),
    )(h, g1, w_up, w_down, g2)


def _pe_kernel(h_ref, p_ref, wg_ref, wp_ref, g_ref, o_ref):
    h = h_ref[...]
    gate = jax.nn.sigmoid(jnp.dot(h.astype(BF16), wg_ref[...], preferred_element_type=F32))
    pe = jnp.dot(p_ref[...].astype(BF16), wp_ref[...], preferred_element_type=F32)
    o_ref[...] = h + _rms(gate * pe, g_ref[...])


def _pe(h, p, wg, wp, g, tm):
    t, d = h.shape
    resident = lambda a: pl.BlockSpec(a.shape, lambda i: (0,) * a.ndim, pipeline_mode=pl.Buffered(1))
    return pl.pallas_call(
        _pe_kernel,
        out_shape=jax.ShapeDtypeStruct((t, d), F32),
        grid=(t // tm,),
        in_specs=[pl.BlockSpec((tm, d), lambda i: (i, 0)), pl.BlockSpec((tm, p.shape[1]), lambda i: (i, 0)),
                  resident(wg), resident(wp), resident(g)],
        out_specs=pl.BlockSpec((tm, d), lambda i: (i, 0)),
        compiler_params=_params(("arbitrary",), 48),
    )(h, p, wg, wp, g)


def _rel_lookup(rel_table, rel):
    return rel_table[:, jnp.clip(rel, -REL_CLIP, REL_CLIP) + REL_CLIP]


def _prompt_bias(rel_table):
    qi = jnp.arange(ATTN_TQ)[:, None]
    kj = jnp.arange(ATTN_SLOTS * ATTN_TQ)[None, :]
    first = (qi // CHUNK) * CHUNK
    visible = (kj >= first) & (kj < first + (LEFT_CHUNKS + 1) * CHUNK)
    return jnp.where(visible, _rel_lookup(rel_table, kj - LEFT_LEN - qi), NEG).astype(F32)


def _sample_bias(rel_table, lc, seq):
    qi = jnp.arange(seq)[:, None]
    cache = _rel_lookup(rel_table, jnp.arange(lc)[None, :] - lc - qi)
    new = _rel_lookup(rel_table, jnp.arange(seq)[None, :] - qi)
    return cache.astype(F32), new.astype(F32)


def _layer(x, pe_in, batch, seq, attn_fn, state, w, tm_proj, tm_mix, tm_mlp, tf_mlp, tm_pe):
    t = x.shape[0]
    qkv, kv, cb, u, gates = _in_proj(x, w["g_pre_mix"], w["w_in"], min(tm_proj, t))
    attn = attn_fn(qkv)
    h = _mix(attn, cb, u, state, gates, x, w["w_attn_out"], w["w_conv_out"], w["w_o"], w["conv_w"],
             w["conv_b"], w["g_post_mix"], tm_mix, seq)
    h = _mlp(h, w["g_pre_mlp"], w["w_up"], w["w_down"], w["g_post_mlp"], min(tm_mlp, t), tf_mlp)
    h = _pe(h, pe_in, w["w_pe_gate"], w["w_pe"], w["g_pe"], min(tm_pe, t))
    width = kv.shape[1] // 2
    heads = width // HEAD_DIM
    k = kv[:, :width].reshape(batch, seq, heads, HEAD_DIM)
    v = kv[:, width:].reshape(batch, seq, heads, HEAD_DIM)
    conv_tail = u.reshape(batch, seq, -1)[:, -2:]
    return h, k, v, conv_tail


def kernel(x_prompt, x_sample, cache_k, cache_v, state_conv, p_prompt, p_sample, w_in, rel_table, w_attn_out,
           conv_w, conv_b, w_conv_out, w_o, g_pre_mix, g_post_mix, g_pre_mlp, g_post_mlp, w_up, w_down, w_pe,
           w_pe_gate, g_pe):
    depth = w_in.shape[0]
    bp, sp, d = x_prompt.shape
    bs, ss, _ = x_sample.shape
    lc = cache_k.shape[2]
    lp = min(LEFT_LEN, sp)
    assert sp % ATTN_TQ == 0 and lc == LEFT_LEN and ss <= CHUNK

    hp = x_prompt.reshape(bp * sp, d)
    hs = x_sample.reshape(bs * ss, d)
    outs = [[] for _ in range(6)]
    for i in range(depth):
        w = {
            "w_in": w_in[i].astype(BF16), "w_attn_out": w_attn_out[i].astype(BF16),
            "w_conv_out": w_conv_out[i].astype(BF16), "w_o": w_o[i].astype(BF16),
            "w_up": w_up[i].astype(BF16), "w_down": w_down[i].astype(BF16),
            "w_pe": w_pe[i].astype(BF16), "w_pe_gate": w_pe_gate[i].astype(BF16),
            "conv_w": conv_w[i], "conv_b": conv_b[i][None],
            "g_pre_mix": g_pre_mix[i][None], "g_post_mix": g_post_mix[i][None],
            "g_pre_mlp": g_pre_mlp[i][None], "g_post_mlp": g_post_mlp[i][None], "g_pe": g_pe[i][None],
        }
        bias_p = _prompt_bias(rel_table[i])
        bias_c, bias_n = _sample_bias(rel_table[i], lc, ss)
        kc = cache_k[i].reshape(bs, lc, -1)
        vc = cache_v[i].reshape(bs, lc, -1)

        hp, kpr, vpr, cpr = _layer(
            hp, p_prompt[i].reshape(bp * sp, -1), bp, sp,
            lambda qkv: _attn_prompt(qkv, bias_p, bp, sp), None, w,
            tm_proj=512, tm_mix=256, tm_mlp=512, tf_mlp=1024, tm_pe=512)
        hs, ksm, vsm, csm = _layer(
            hs, p_sample[i].reshape(bs * ss, -1), bs, ss,
            lambda qkv: _attn_sample(qkv, kc, vc, bias_c, bias_n, bs, ss), state_conv[i], w,
            tm_proj=512, tm_mix=ss, tm_mlp=512, tf_mlp=1024, tm_pe=512)
        for lst, val in zip(outs, (kpr[:, -lp:], vpr[:, -lp:], cpr, ksm, vsm, csm)):
            lst.append(val)
    k_prompt, v_prompt, conv_prompt, k_sample, v_sample, conv_sample = [jnp.stack(o) for o in outs]
    return (hp.reshape(bp, sp, d), hs.reshape(bs, ss, d), k_prompt, v_prompt, conv_prompt,
            k_sample, v_sample, conv_sample)
```

```python
import functools

import jax
import jax.numpy as jnp
from jax import lax
from jax.experimental import pallas as pl
from jax.experimental.pallas import tpu as pltpu

F32 = jnp.float32
BF16 = jnp.bfloat16

CHUNK = 64
LEFT_CHUNKS = 8
LEFT_LEN = LEFT_CHUNKS * CHUNK
BAND = (LEFT_CHUNKS + 1) * CHUNK
HEAD_DIM = 64
REL_CLIP = 128
EPS = 1e-6
NEG = -1e30

LANES = 128
ATTN_TQ = 256
ATTN_NK = LEFT_LEN + ATTN_TQ
ATTN_SLOTS = ATTN_NK // ATTN_TQ
REL_ROW = ATTN_NK + ATTN_TQ
MIB = 1024 * 1024


def _rms(x, g):
    return x * lax.rsqrt(jnp.mean(x * x, axis=-1, keepdims=True) + EPS) * g


def _params(sem, vmem_mib):
    return pltpu.CompilerParams(dimension_semantics=sem, vmem_limit_bytes=vmem_mib * MIB)


def _in_proj_kernel(tiles_per_seq, x_ref, g_ref, w_ref, qkv_ref, *refs):
    if tiles_per_seq is None:
        kv_ref, cb_ref, u_ref, gates_ref, n_ref, cc_ref = refs
    else:
        kt_ref, vt_ref, cb_ref, u_ref, gates_ref, n_ref, cc_ref = refs
    i, j = pl.program_id(0), pl.program_id(1)

    @pl.when(j == 0)
    def _():
        n_ref[...] = _rms(x_ref[...], g_ref[...]).astype(BF16)

    z = jnp.dot(n_ref[...], w_ref[...], preferred_element_type=F32)

    @pl.when(j == 0)
    def _():
        qkv_ref[...] = (z * (HEAD_DIM ** -0.5)).astype(BF16)

    @pl.when((j == 1) | (j == 2))
    def _():
        qkv_ref[...] = z.astype(BF16)

    if tiles_per_seq is None:
        @pl.when((j == 1) | (j == 2))
        def _():
            kv_ref[...] = z
    else:
        is_tail = i % tiles_per_seq == tiles_per_seq - 1

        @pl.when(is_tail & (j == 1))
        def _():
            kt_ref[...] = z.T

        @pl.when(is_tail & (j == 2))
        def _():
            vt_ref[...] = z.T

    @pl.when(j == 3)
    def _():
        cb_ref[...] = z.astype(BF16)

    @pl.when(j == 4)
    def _():
        cc_ref[...] = z

    @pl.when(j == 5)
    def _():
        u_ref[...] = cc_ref[...] * z

    @pl.when(j >= 6)
    def _():
        gates_ref[...] = jax.nn.sigmoid(z).astype(BF16)


def _in_proj(x, g, w_in, tm, tiles_per_seq):
    t, d = x.shape
    tn = 1024
    nj = w_in.shape[1] // tn
    assert nj == 10 and t % tm == 0
    if tiles_per_seq is None:
        kv_shapes = (jax.ShapeDtypeStruct((t, 2 * tn), F32),)
        kv_specs = (pl.BlockSpec((tm, tn), lambda i, j: (i, jnp.clip(j - 1, 0, 1))),)
    else:
        nseq = t // (tm * tiles_per_seq)
        kv_shapes = (jax.ShapeDtypeStruct((nseq, tn, tm), F32),) * 2
        kv_specs = (pl.BlockSpec((None, tn, tm), lambda i, j: (i // tiles_per_seq, 0, 0)),) * 2
    out_shape = (
        jax.ShapeDtypeStruct((t, 3 * tn), BF16),
        *kv_shapes,
        jax.ShapeDtypeStruct((t, tn), BF16),
        jax.ShapeDtypeStruct((t, tn), F32),
        jax.ShapeDtypeStruct((t, 4 * tn), BF16),
    )
    return pl.pallas_call(
        functools.partial(_in_proj_kernel, tiles_per_seq),
        out_shape=out_shape,
        grid=(t // tm, nj),
        in_specs=[
            pl.BlockSpec((tm, d), lambda i, j: (i, 0)),
            pl.BlockSpec((1, d), lambda i, j: (0, 0)),
            pl.BlockSpec((d, tn), lambda i, j: (0, j)),
        ],
        out_specs=(
            pl.BlockSpec((tm, tn), lambda i, j: (i, jnp.minimum(j, 2))),
            *kv_specs,
            pl.BlockSpec((tm, tn), lambda i, j: (i, 0)),
            pl.BlockSpec((tm, tn), lambda i, j: (i, 0)),
            pl.BlockSpec((tm, tn), lambda i, j: (i, jnp.clip(j - 6, 0, 3))),
        ),
        scratch_shapes=[pltpu.VMEM((tm, d), BF16), pltpu.VMEM((tm, tn), F32)],
        compiler_params=_params(("arbitrary", "arbitrary"), 56),
    )(x, g, w_in)


def _rel_rows(rel_table):
    h, n_rel = rel_table.shape
    lo = LEFT_LEN - REL_CLIP
    hi = ATTN_NK - lo - n_rel
    first = jnp.broadcast_to(rel_table[:, :1], (h, lo))
    last = jnp.broadcast_to(rel_table[:, -1:], (h, hi))
    negative = jnp.broadcast_to(rel_table[:, :1], (h, REL_ROW - ATTN_NK))
    return jnp.concatenate([first, rel_table, last, negative], axis=1).astype(F32)


def _toeplitz(row, nq):
    return pltpu.roll(jnp.broadcast_to(row, (nq, row.shape[1])), 0, 1, stride=1, stride_axis=0)


def _softmax_pv(score_blocks, value_blocks, value_dims):
    m = functools.reduce(jnp.maximum, [jnp.max(s, axis=-1, keepdims=True) for s in score_blocks])
    ps = [jnp.exp(s - m) for s in score_blocks]
    denom = functools.reduce(jnp.add, [jnp.sum(p, axis=-1, keepdims=True) for p in ps])
    acc = functools.reduce(jnp.add, [
        lax.dot_general(p.astype(BF16), v, (((1,), (vd,)), ((), ())), preferred_element_type=F32)
        for p, v, vd in zip(ps, value_blocks, value_dims)])
    return acc / denom


def _head_pair(q_pair, key_blocks, key_dims, value_blocks, value_dims, bias_fn):
    lane = lax.broadcasted_iota(jnp.int32, q_pair.shape, 1)
    outs = []
    for hh in range(2):
        own = (lane >= HEAD_DIM) if hh else (lane < HEAD_DIM)
        qm = jnp.where(own, q_pair, jnp.zeros_like(q_pair))
        scores = [
            lax.dot_general(qm, kb, (((1,), (kd,)), ((), ())), preferred_element_type=F32) + bias_fn(hh, n)
            for n, (kb, kd) in enumerate(zip(key_blocks, key_dims))]
        outs.append(_softmax_pv(scores, value_blocks, value_dims))
    return jnp.where(lane < HEAD_DIM, outs[0], outs[1])


def _attn_prompt_kernel(q_ref, *refs):
    k_refs = refs[:ATTN_SLOTS]
    v_refs = refs[ATTN_SLOTS:2 * ATTN_SLOTS]
    rel_ref, o_ref, bias_ref = refs[2 * ATTN_SLOTS:]
    t = pl.program_id(1)

    @pl.when((pl.program_id(0) == 0) & (t == 0))
    def _():
        qi = lax.broadcasted_iota(jnp.int32, (ATTN_TQ, ATTN_NK), 0)
        kj = lax.broadcasted_iota(jnp.int32, (ATTN_TQ, ATTN_NK), 1)
        first = (qi // CHUNK) * CHUNK
        visible = (kj >= first) & (kj < first + BAND)
        for h in range(bias_ref.shape[0]):
            bias_ref[h] = jnp.where(visible, _toeplitz(rel_ref[h:h + 1, :], ATTN_TQ)[:, :ATTN_NK], NEG)

    col = lax.broadcasted_iota(jnp.int32, (1, ATTN_NK), 1)
    left_pad = jnp.where(col + t * ATTN_TQ >= LEFT_LEN, 0.0, NEG).astype(F32)
    for hp in range(q_ref.shape[1] // LANES):
        sl = slice(hp * LANES, (hp + 1) * LANES)
        keys = jnp.concatenate([r[:, sl] for r in k_refs], axis=0)
        vals = jnp.concatenate([r[:, sl] for r in v_refs], axis=0)
        bias_fn = lambda hh, n, hp=hp: bias_ref[2 * hp + hh] + left_pad
        o_ref[:, sl] = _head_pair(q_ref[:, sl], [keys], [1], [vals], [0], bias_fn).astype(o_ref.dtype)


def _attn_prompt(qkv, rel_rows, batch, seq):
    t, width = qkv.shape[0], qkv.shape[1] // 3
    nt = seq // ATTN_TQ
    blk = (ATTN_TQ, width)

    def slot_spec(s, col):
        return pl.BlockSpec(blk, lambda b, i: (b * nt + jnp.maximum(i - (ATTN_SLOTS - 1) + s, 0), col))

    return pl.pallas_call(
        _attn_prompt_kernel,
        out_shape=jax.ShapeDtypeStruct((t, width), BF16),
        grid=(batch, nt),
        in_specs=[pl.BlockSpec(blk, lambda b, i: (b * nt + i, 0))]
        + [slot_spec(s, 1) for s in range(ATTN_SLOTS)]
        + [slot_spec(s, 2) for s in range(ATTN_SLOTS)]
        + [pl.BlockSpec(rel_rows.shape, lambda b, i: (0, 0))],
        out_specs=pl.BlockSpec(blk, lambda b, i: (b * nt + i, 0)),
        scratch_shapes=[pltpu.VMEM((rel_rows.shape[0], ATTN_TQ, ATTN_NK), F32)],
        compiler_params=_params(("arbitrary", "arbitrary"), 40),
    )(qkv, *([qkv] * (2 * ATTN_SLOTS)), rel_rows)


def _attn_sample_kernel(q_ref, kn_ref, vn_ref, kct_ref, vct_ref, rel_ref, o_ref, bias_ref):
    nq = q_ref.shape[0]
    lc = kct_ref.shape[1]

    @pl.when(pl.program_id(0) == 0)
    def _():
        for h in range(bias_ref.shape[0]):
            bias_ref[h] = _toeplitz(rel_ref[h:h + 1, :], nq)[:, :bias_ref.shape[2]]

    for hp in range(q_ref.shape[1] // LANES):
        sl = slice(hp * LANES, (hp + 1) * LANES)
        keys = [kct_ref[sl, :].astype(BF16), kn_ref[:, sl]]
        vals = [vct_ref[sl, :].astype(BF16), vn_ref[:, sl]]
        bias_fn = lambda hh, n, hp=hp: bias_ref[2 * hp + hh][:, (0, lc)[n]:(lc, lc + nq)[n]]
        o_ref[:, sl] = _head_pair(q_ref[:, sl], keys, [0, 1], vals, [1, 0], bias_fn).astype(o_ref.dtype)


def _attn_sample(qkv, cache_kt, cache_vt, rel_rows, batch, seq):
    t, width = qkv.shape[0], qkv.shape[1] // 3
    lc = cache_kt.shape[2]
    blk = (seq, width)
    bias_cols = -(-(lc + seq) // LANES) * LANES
    return pl.pallas_call(
        _attn_sample_kernel,
        out_shape=jax.ShapeDtypeStruct((t, width), BF16),
        grid=(batch,),
        in_specs=[
            pl.BlockSpec(blk, lambda b: (b, 0)),
            pl.BlockSpec(blk, lambda b: (b, 1)),
            pl.BlockSpec(blk, lambda b: (b, 2)),
            pl.BlockSpec((None, width, lc), lambda b: (b, 0, 0)),
            pl.BlockSpec((None, width, lc), lambda b: (b, 0, 0)),
            pl.BlockSpec(rel_rows.shape, lambda b: (0, 0)),
        ],
        out_specs=pl.BlockSpec(blk, lambda b: (b, 0)),
        scratch_shapes=[pltpu.VMEM((rel_rows.shape[0], seq, bias_cols), F32)],
        compiler_params=_params(("arbitrary",), 32),
    )(qkv, qkv, qkv, cache_kt, cache_vt, rel_rows)


def _mix_kernel(seq_rows, attn_ref, cb_ref, u_ref, halo_ref, gates_ref, x_ref, wa_ref, wc_ref, wo_ref,
                cw_ref, cbias_ref, g_ref, o_ref):
    tm, d = x_ref.shape
    u = u_ref[...]
    if halo_ref.shape[0] == 2:
        prev2, prev1 = halo_ref[0:1, :], halo_ref[1:2, :]
    else:
        keep = ((pl.program_id(0) * tm) % seq_rows != 0).astype(F32)
        nh = halo_ref.shape[0]
        prev2, prev1 = halo_ref[nh - 2:nh - 1, :] * keep, halo_ref[nh - 1:nh, :] * keep
    row = lax.broadcasted_iota(jnp.int32, u.shape, 0)
    um1 = jnp.where(row == 0, prev1, pltpu.roll(u, 1, axis=0))
    um2 = jnp.where(row == 0, prev2, jnp.where(row == 1, prev1, pltpu.roll(u, 2, axis=0)))
    conv = cbias_ref[...] + cw_ref[0:1, :] * um2 + cw_ref[1:2, :] * um1 + cw_ref[2:3, :] * u
    ya = jnp.dot(attn_ref[...], wa_ref[...], preferred_element_type=F32)
    yb = jnp.dot((cb_ref[...].astype(F32) * conv).astype(BF16), wc_ref[...], preferred_element_type=F32)
    merged = gates_ref[:, :d].astype(F32) * ya + gates_ref[:, d:].astype(F32) * yb
    mix = jnp.dot(merged.astype(BF16), wo_ref[...], preferred_element_type=F32)
    o_ref[...] = x_ref[...] + _rms(mix, g_ref[...])


def _mix(attn, cb, u, state, gates, x, wa, wc, wo, conv_w, conv_b, g, tm, seq_rows):
    t, d = x.shape
    c = u.shape[1]
    halo_rows = 8
    if state is None:
        halo, halo_spec = u, pl.BlockSpec(
            (halo_rows, c), lambda i: (jnp.maximum(i * (tm // halo_rows) - 1, 0), 0))
    else:
        assert tm == seq_rows
        halo, halo_spec = state, pl.BlockSpec((None,) + state.shape[1:], lambda i: (i, 0, 0))
    row_spec = lambda w: pl.BlockSpec((tm, w), lambda i: (i, 0))
    resident = lambda a: pl.BlockSpec(a.shape, lambda i: (0,) * a.ndim, pipeline_mode=pl.Buffered(1))
    return pl.pallas_call(
        functools.partial(_mix_kernel, seq_rows),
        out_shape=jax.ShapeDtypeStruct((t, d), F32),
        grid=(t // tm,),
        in_specs=[row_spec(attn.shape[1]), row_spec(c), row_spec(c), halo_spec, row_spec(2 * d), row_spec(d),
                  resident(wa), resident(wc), resident(wo), resident(conv_w), resident(conv_b), resident(g)],
        out_specs=row_spec(d),
        compiler_params=_params(("arbitrary",), 48),
    )(attn, cb, u, halo, gates, x, wa, wc, wo, conv_w, conv_b, g)


def _mlp_kernel(h_ref, g1_ref, wu_ref, wd_ref, g2_ref, o_ref, n_ref, acc_ref):
    j = pl.program_id(1)

    @pl.when(j == 0)
    def _():
        n_ref[...] = _rms(h_ref[...], g1_ref[...]).astype(BF16)

    a = jnp.dot(n_ref[...], wu_ref[...], preferred_element_type=F32)
    a = jnp.square(jnp.maximum(a, 0.0)).astype(BF16)
    part = jnp.dot(a, wd_ref[...], preferred_element_type=F32)

    @pl.when(j == 0)
    def _():
        acc_ref[...] = part

    @pl.when(j > 0)
    def _():
        acc_ref[...] += part

    @pl.when(j == pl.num_programs(1) - 1)
    def _():
        o_ref[...] = h_ref[...] + _rms(acc_ref[...], g2_ref[...])


def _mlp(h, g1, w_up, w_down, g2, tm, tf):
    t, d = h.shape
    ff = w_up.shape[1]
    return pl.pallas_call(
        _mlp_kernel,
        out_shape=jax.ShapeDtypeStruct((t, d), F32),
        grid=(t // tm, ff // tf),
        in_specs=[
            pl.BlockSpec((tm, d), lambda i, j: (i, 0)),
            pl.BlockSpec((1, d), lambda i, j: (0, 0)),
            pl.BlockSpec((d, tf), lambda i, j: (0, j)),
            pl.BlockSpec((tf, d), lambda i, j: (j, 0)),
            pl.BlockSpec((1, d), lambda i, j: (0, 0)),
        ],
        out_specs=pl.BlockSpec((tm, d), lambda i, j: (i, 0)),
        scratch_shapes=[pltpu.VMEM((tm, d), BF16), pltpu.VMEM((tm, d), F32)],
        compiler_params=_params(("arbitrary", "arbitrary"), 56),
    )(h, g1, w_up, w_down, g2)


def _pe_kernel(h_ref, p_ref, wg_ref, wp_ref, g_ref, o_ref):
    h = h_ref[...]
    gate = jax.nn.sigmoid(jnp.dot(h.astype(BF16), wg_ref[...], preferred_element_type=F32))
    pe = jnp.dot(p_ref[...].astype(BF16), wp_ref[...], preferred_element_type=F32)
    o_ref[...] = h + _rms(gate * pe, g_ref[...])


def _pe(h, p, wg, wp, g, tm):
    t, d = h.shape
    resident = lambda a: pl.BlockSpec(a.shape, lambda i: (0,) * a.ndim, pipeline_mode=pl.Buffered(1))
    return pl.pallas_call(
        _pe_kernel,
        out_shape=jax.ShapeDtypeStruct((t, d), F32),
        grid=(t // tm,),
        in_specs=[pl.BlockSpec((tm, d), lambda i: (i, 0)), pl.BlockSpec((tm, p.shape[1]), lambda i: (i, 0)),
                  resident(wg), resident(wp), resident(g)],
        out_specs=pl.BlockSpec((tm, d), lambda i: (i, 0)),
        compiler_params=_params(("arbitrary",), 48),
    )(h, p, wg, wp, g)


def _layer(x, pe_in, seq, tiles_per_seq, attn_fn, state, w, tm_proj, tm_mix, tm_mlp, tf_mlp, tm_pe):
    t = x.shape[0]
    qkv, *kv, cb, u, gates = _in_proj(x, w["g_pre_mix"], w["w_in"], min(tm_proj, t), tiles_per_seq)
    attn = attn_fn(qkv)
    h = _mix(attn, cb, u, state, gates, x, w["w_attn_out"], w["w_conv_out"], w["w_o"], w["conv_w"],
             w["conv_b"], w["g_post_mix"], tm_mix, seq)
    h = _mlp(h, w["g_pre_mlp"], w["w_up"], w["w_down"], w["g_post_mlp"], min(tm_mlp, t), tf_mlp)
    h = _pe(h, pe_in, w["w_pe_gate"], w["w_pe"], w["g_pe"], min(tm_pe, t))
    return h, kv, u.reshape(t // seq, seq, -1)[:, -2:]


def kernel(x_prompt, x_sample, cache_k, cache_v, state_conv, p_prompt, p_sample, w_in, rel_table, w_attn_out,
           conv_w, conv_b, w_conv_out, w_o, g_pre_mix, g_post_mix, g_pre_mlp, g_post_mlp, w_up, w_down, w_pe,
           w_pe_gate, g_pe):
    depth = w_in.shape[0]
    bp, sp, d = x_prompt.shape
    bs, ss, _ = x_sample.shape
    lc, heads = cache_k.shape[2], cache_k.shape[3]
    lp = min(LEFT_LEN, sp)
    assert sp % lp == 0 and sp % ATTN_TQ == 0 and lc == LEFT_LEN and ss <= CHUNK
    assert rel_table.shape[2] == 2 * REL_CLIP + 1

    hp = x_prompt.reshape(bp * sp, d)
    hs = x_sample.reshape(bs * ss, d)
    outs = [[] for _ in range(6)]
    for i in range(depth):
        w = {
            "w_in": w_in[i].astype(BF16), "w_attn_out": w_attn_out[i].astype(BF16),
            "w_conv_out": w_conv_out[i].astype(BF16), "w_o": w_o[i].astype(BF16),
            "w_up": w_up[i].astype(BF16), "w_down": w_down[i].astype(BF16),
            "w_pe": w_pe[i].astype(BF16), "w_pe_gate": w_pe_gate[i].astype(BF16),
            "conv_w": conv_w[i], "conv_b": conv_b[i][None],
            "g_pre_mix": g_pre_mix[i][None], "g_post_mix": g_post_mix[i][None],
            "g_pre_mlp": g_pre_mlp[i][None], "g_post_mlp": g_post_mlp[i][None], "g_pe": g_pe[i][None],
        }
        rel_rows = _rel_rows(rel_table[i])
        kct = cache_k[i].transpose(0, 2, 3, 1).reshape(bs, heads * HEAD_DIM, lc)
        vct = cache_v[i].transpose(0, 2, 3, 1).reshape(bs, heads * HEAD_DIM, lc)

        hp, (ktp, vtp), cpr = _layer(
            hp, p_prompt[i].reshape(bp * sp, -1), sp, sp // lp,
            lambda qkv: _attn_prompt(qkv, rel_rows, bp, sp), None, w,
            tm_proj=lp, tm_mix=256, tm_mlp=512, tf_mlp=1024, tm_pe=512)
        hs, (kvs,), csm = _layer(
            hs, p_sample[i].reshape(bs * ss, -1), ss, None,
            lambda qkv: _attn_sample(qkv, kct, vct, rel_rows, bs, ss), state_conv[i], w,
            tm_proj=512, tm_mix=ss, tm_mlp=512, tf_mlp=1024, tm_pe=512)
        kpr = ktp.reshape(bp, heads, HEAD_DIM, lp).transpose(0, 3, 1, 2)
        vpr = vtp.reshape(bp, heads, HEAD_DIM, lp).transpose(0, 3, 1, 2)
        ksm = kvs[:, :heads * HEAD_DIM].reshape(bs, ss, heads, HEAD_DIM)
        vsm = kvs[:, heads * HEAD_DIM:].reshape(bs, ss, heads, HEAD_DIM)
        for lst, val in zip(outs, (kpr, vpr, cpr, ksm, vsm, csm)):
            lst.append(val)
    k_prompt, v_prompt, conv_prompt, k_sample, v_sample, conv_sample = [jnp.stack(o) for o in outs]
    return (hp.reshape(bp, sp, d), hs.reshape(bs, ss, d), k_prompt, v_prompt, conv_prompt,
            k_sample, v_sample, conv_sample)
```

```python
import functools

import jax
import jax.numpy as jnp
from jax import lax
from jax.experimental import pallas as pl
from jax.experimental.pallas import tpu as pltpu

F32 = jnp.float32
BF16 = jnp.bfloat16

CHUNK = 64
LEFT_CHUNKS = 8
LEFT_LEN = LEFT_CHUNKS * CHUNK
BAND = (LEFT_CHUNKS + 1) * CHUNK
HEAD_DIM = 64
REL_CLIP = 128
EPS = 1e-6
NEG = -1e30

LANES = 128
ATTN_TQ = 256
ATTN_NK = LEFT_LEN + ATTN_TQ
ATTN_SLOTS = ATTN_NK // ATTN_TQ
REL_ROW = ATTN_NK + ATTN_TQ
MIB = 1024 * 1024


def _rms(x, g):
    return x * lax.rsqrt(jnp.mean(x * x, axis=-1, keepdims=True) + EPS) * g


def _params(sem, vmem_mib):
    return pltpu.CompilerParams(dimension_semantics=sem, vmem_limit_bytes=vmem_mib * MIB)


def _in_proj_kernel(tiles_per_seq, x_ref, g_ref, w_ref, qkv_ref, *refs):
    if tiles_per_seq is None:
        kv_ref, cb_ref, u_ref, gates_ref, n_ref, cc_ref = refs
    else:
        kt_ref, vt_ref, cb_ref, u_ref, gates_ref, n_ref, cc_ref = refs
    i, j = pl.program_id(0), pl.program_id(1)

    @pl.when(j == 0)
    def _():
        n_ref[...] = _rms(x_ref[...], g_ref[...]).astype(BF16)

    def proj():
        return jnp.dot(n_ref[...], w_ref[...], preferred_element_type=F32)

    @pl.when(j == 0)
    def _():
        qkv_ref[...] = (proj() * (HEAD_DIM ** -0.5)).astype(BF16)

    if tiles_per_seq is None:
        @pl.when((j == 1) | (j == 2))
        def _():
            z = proj()
            qkv_ref[...] = z.astype(BF16)
            kv_ref[...] = z
    else:
        is_tail = i % tiles_per_seq == tiles_per_seq - 1

        @pl.when(((j == 1) | (j == 2)) & jnp.logical_not(is_tail))
        def _():
            qkv_ref[...] = proj().astype(BF16)

        for jj, t_ref in ((1, kt_ref), (2, vt_ref)):
            @pl.when((j == jj) & is_tail)
            def _(t_ref=t_ref):
                z = proj()
                qkv_ref[...] = z.astype(BF16)
                t_ref[...] = z.T

    @pl.when(j == 3)
    def _():
        cb_ref[...] = proj().astype(BF16)

    @pl.when(j == 4)
    def _():
        cc_ref[...] = proj()

    @pl.when(j == 5)
    def _():
        u_ref[...] = cc_ref[...] * proj()

    @pl.when(j >= 6)
    def _():
        gates_ref[...] = jax.nn.sigmoid(proj()).astype(BF16)


def _in_proj(x, g, w_in, tm, tiles_per_seq):
    t, d = x.shape
    tn = 1024
    nj = w_in.shape[1] // tn
    assert nj == 10 and t % tm == 0
    if tiles_per_seq is None:
        kv_shapes = (jax.ShapeDtypeStruct((t, 2 * tn), F32),)
        kv_specs = (pl.BlockSpec((tm, tn), lambda i, j: (i, jnp.clip(j - 1, 0, 1))),)
    else:
        nseq = t // (tm * tiles_per_seq)
        kv_shapes = (jax.ShapeDtypeStruct((nseq, tn, tm), F32),) * 2
        kv_specs = (pl.BlockSpec((None, tn, tm), lambda i, j: (i // tiles_per_seq, 0, 0)),) * 2
    out_shape = (
        jax.ShapeDtypeStruct((t, 3 * tn), BF16),
        *kv_shapes,
        jax.ShapeDtypeStruct((t, tn), BF16),
        jax.ShapeDtypeStruct((t, tn), F32),
        jax.ShapeDtypeStruct((t, 4 * tn), BF16),
    )
    return pl.pallas_call(
        functools.partial(_in_proj_kernel, tiles_per_seq),
        out_shape=out_shape,
        grid=(t // tm, nj),
        in_specs=[
            pl.BlockSpec((tm, d), lambda i, j: (i, 0)),
            pl.BlockSpec((1, d), lambda i, j: (0, 0)),
            pl.BlockSpec((d, tn), lambda i, j: (0, j)),
        ],
        out_specs=(
            pl.BlockSpec((tm, tn), lambda i, j: (i, jnp.minimum(j, 2))),
            *kv_specs,
            pl.BlockSpec((tm, tn), lambda i, j: (i, 0)),
            pl.BlockSpec((tm, tn), lambda i, j: (i, 0)),
            pl.BlockSpec((tm, tn), lambda i, j: (i, jnp.clip(j - 6, 0, 3))),
        ),
        scratch_shapes=[pltpu.VMEM((tm, d), BF16), pltpu.VMEM((tm, tn), F32)],
        compiler_params=_params(("arbitrary", "arbitrary"), 56),
    )(x, g, w_in)


def _rel_rows(rel_table):
    h, n_rel = rel_table.shape
    lo = LEFT_LEN - REL_CLIP
    hi = ATTN_NK - lo - n_rel
    first = jnp.broadcast_to(rel_table[:, :1], (h, lo))
    last = jnp.broadcast_to(rel_table[:, -1:], (h, hi))
    negative = jnp.broadcast_to(rel_table[:, :1], (h, REL_ROW - ATTN_NK))
    return jnp.concatenate([first, rel_table, last, negative], axis=1).astype(F32)


def _toeplitz(row, nq):
    return pltpu.roll(jnp.broadcast_to(row, (nq, row.shape[1])), 0, 1, stride=1, stride_axis=0)


def _softmax_pv(score_blocks, value_blocks, value_dims):
    m = functools.reduce(jnp.maximum, [jnp.max(s, axis=-1, keepdims=True) for s in score_blocks])
    ps = [jnp.exp(s - m) for s in score_blocks]
    denom = functools.reduce(jnp.add, [jnp.sum(p, axis=-1, keepdims=True) for p in ps])
    acc = functools.reduce(jnp.add, [
        lax.dot_general(p.astype(BF16), v, (((1,), (vd,)), ((), ())), preferred_element_type=F32)
        for p, v, vd in zip(ps, value_blocks, value_dims)])
    return acc / denom


def _head_pair(q_pair, key_blocks, key_dims, value_blocks, value_dims, bias_fn):
    lane = lax.broadcasted_iota(jnp.int32, q_pair.shape, 1)
    outs = []
    for hh in range(2):
        own = (lane >= HEAD_DIM) if hh else (lane < HEAD_DIM)
        qm = jnp.where(own, q_pair, jnp.zeros_like(q_pair))
        scores = [
            lax.dot_general(qm, kb, (((1,), (kd,)), ((), ())), preferred_element_type=F32) + bias_fn(hh, n)
            for n, (kb, kd) in enumerate(zip(key_blocks, key_dims))]
        outs.append(_softmax_pv(scores, value_blocks, value_dims))
    return jnp.where(lane < HEAD_DIM, outs[0], outs[1])


def _attn_prompt_kernel(q_ref, *refs):
    k_refs = refs[:ATTN_SLOTS]
    v_refs = refs[ATTN_SLOTS:2 * ATTN_SLOTS]
    rel_ref, o_ref, bias_ref = refs[2 * ATTN_SLOTS:]
    t = pl.program_id(1)

    @pl.when((pl.program_id(0) == 0) & (t == 0))
    def _():
        qi = lax.broadcasted_iota(jnp.int32, (ATTN_TQ, ATTN_NK), 0)
        kj = lax.broadcasted_iota(jnp.int32, (ATTN_TQ, ATTN_NK), 1)
        first = (qi // CHUNK) * CHUNK
        visible = (kj >= first) & (kj < first + BAND)
        for h in range(bias_ref.shape[0]):
            bias_ref[h] = jnp.where(visible, _toeplitz(rel_ref[h:h + 1, :], ATTN_TQ)[:, :ATTN_NK], NEG)

    col = lax.broadcasted_iota(jnp.int32, (1, ATTN_NK), 1)
    left_pad = jnp.where(col + t * ATTN_TQ >= LEFT_LEN, 0.0, NEG).astype(F32)
    for hp in range(q_ref.shape[1] // LANES):
        sl = slice(hp * LANES, (hp + 1) * LANES)
        keys = jnp.concatenate([r[:, sl] for r in k_refs], axis=0)
        vals = jnp.concatenate([r[:, sl] for r in v_refs], axis=0)
        bias_fn = lambda hh, n, hp=hp: bias_ref[2 * hp + hh] + left_pad
        o_ref[:, sl] = _head_pair(q_ref[:, sl], [keys], [1], [vals], [0], bias_fn).astype(o_ref.dtype)


def _attn_prompt(qkv, rel_rows, batch, seq):
    t, width = qkv.shape[0], qkv.shape[1] // 3
    nt = seq // ATTN_TQ
    blk = (ATTN_TQ, width)

    def slot_spec(s, col):
        return pl.BlockSpec(blk, lambda b, i: (b * nt + jnp.maximum(i - (ATTN_SLOTS - 1) + s, 0), col))

    return pl.pallas_call(
        _attn_prompt_kernel,
        out_shape=jax.ShapeDtypeStruct((t, width), BF16),
        grid=(batch, nt),
        in_specs=[pl.BlockSpec(blk, lambda b, i: (b * nt + i, 0))]
        + [slot_spec(s, 1) for s in range(ATTN_SLOTS)]
        + [slot_spec(s, 2) for s in range(ATTN_SLOTS)]
        + [pl.BlockSpec(rel_rows.shape, lambda b, i: (0, 0))],
        out_specs=pl.BlockSpec(blk, lambda b, i: (b * nt + i, 0)),
        scratch_shapes=[pltpu.VMEM((rel_rows.shape[0], ATTN_TQ, ATTN_NK), F32)],
        compiler_params=_params(("arbitrary", "arbitrary"), 40),
    )(qkv, *([qkv] * (2 * ATTN_SLOTS)), rel_rows)


def _attn_sample_kernel(q_ref, kn_ref, vn_ref, kct_ref, vct_ref, rel_ref, o_ref, bias_ref):
    nq = q_ref.shape[0]
    lc = kct_ref.shape[1]

    @pl.when(pl.program_id(0) == 0)
    def _():
        for h in range(bias_ref.shape[0]):
            bias_ref[h] = _toeplitz(rel_ref[h:h + 1, :], nq)[:, :bias_ref.shape[2]]

    for hp in range(q_ref.shape[1] // LANES):
        sl = slice(hp * LANES, (hp + 1) * LANES)
        keys = [kct_ref[sl, :].astype(BF16), kn_ref[:, sl]]
        vals = [vct_ref[sl, :].astype(BF16), vn_ref[:, sl]]
        bias_fn = lambda hh, n, hp=hp: bias_ref[2 * hp + hh][:, (0, lc)[n]:(lc, lc + nq)[n]]
        o_ref[:, sl] = _head_pair(q_ref[:, sl], keys, [0, 1], vals, [1, 0], bias_fn).astype(o_ref.dtype)


def _attn_sample(qkv, cache_kt, cache_vt, rel_rows, batch, seq):
    t, width = qkv.shape[0], qkv.shape[1] // 3
    lc = cache_kt.shape[2]
    blk = (seq, width)
    bias_cols = -(-(lc + seq) // LANES) * LANES
    return pl.pallas_call(
        _attn_sample_kernel,
        out_shape=jax.ShapeDtypeStruct((t, width), BF16),
        grid=(batch,),
        in_specs=[
            pl.BlockSpec(blk, lambda b: (b, 0)),
            pl.BlockSpec(blk, lambda b: (b, 1)),
            pl.BlockSpec(blk, lambda b: (b, 2)),
            pl.BlockSpec((None, width, lc), lambda b: (b, 0, 0)),
            pl.BlockSpec((None, width, lc), lambda b: (b, 0, 0)),
            pl.BlockSpec(rel_rows.shape, lambda b: (0, 0)),
        ],
        out_specs=pl.BlockSpec(blk, lambda b: (b, 0)),
        scratch_shapes=[pltpu.VMEM((rel_rows.shape[0], seq, bias_cols), F32)],
        compiler_params=_params(("arbitrary",), 32),
    )(qkv, qkv, qkv, cache_kt, cache_vt, rel_rows)


def _mix_kernel(seq_rows, attn_ref, cb_ref, u_ref, halo_ref, gates_ref, x_ref, wa_ref, wc_ref, wo_ref,
                cw_ref, cbias_ref, g_ref, o_ref):
    tm, d = x_ref.shape
    u = u_ref[...]
    if halo_ref.shape[0] == 2:
        prev2, prev1 = halo_ref[0:1, :], halo_ref[1:2, :]
    else:
        keep = ((pl.program_id(0) * tm) % seq_rows != 0).astype(F32)
        nh = halo_ref.shape[0]
        prev2, prev1 = halo_ref[nh - 2:nh - 1, :] * keep, halo_ref[nh - 1:nh, :] * keep
    row = lax.broadcasted_iota(jnp.int32, u.shape, 0)
    um1 = jnp.where(row == 0, prev1, pltpu.roll(u, 1, axis=0))
    um2 = jnp.where(row == 0, prev2, jnp.where(row == 1, prev1, pltpu.roll(u, 2, axis=0)))
    conv = cbias_ref[...] + cw_ref[0:1, :] * um2 + cw_ref[1:2, :] * um1 + cw_ref[2:3, :] * u
    ya = jnp.dot(attn_ref[...], wa_ref[...], preferred_element_type=F32)
    yb = jnp.dot((cb_ref[...].astype(F32) * conv).astype(BF16), wc_ref[...], preferred_element_type=F32)
    merged = gates_ref[:, :d].astype(F32) * ya + gates_ref[:, d:].astype(F32) * yb
    mix = jnp.dot(merged.astype(BF16), wo_ref[...], preferred_element_type=F32)
    o_ref[...] = x_ref[...] + _rms(mix, g_ref[...])


def _mix(attn, cb, u, state, gates, x, wa, wc, wo, conv_w, conv_b, g, tm, seq_rows):
    t, d = x.shape
    c = u.shape[1]
    halo_rows = 8
    if state is None:
        halo, halo_spec = u, pl.BlockSpec(
            (halo_rows, c), lambda i: (jnp.maximum(i * (tm // halo_rows) - 1, 0), 0))
    else:
        assert tm == seq_rows
        halo, halo_spec = state, pl.BlockSpec((None,) + state.shape[1:], lambda i: (i, 0, 0))
    row_spec = lambda w: pl.BlockSpec((tm, w), lambda i: (i, 0))
    resident = lambda a: pl.BlockSpec(a.shape, lambda i: (0,) * a.ndim, pipeline_mode=pl.Buffered(1))
    return pl.pallas_call(
        functools.partial(_mix_kernel, seq_rows),
        out_shape=jax.ShapeDtypeStruct((t, d), F32),
        grid=(t // tm,),
        in_specs=[row_spec(attn.shape[1]), row_spec(c), row_spec(c), halo_spec, row_spec(2 * d), row_spec(d),
                  resident(wa), resident(wc), resident(wo), resident(conv_w), resident(conv_b), resident(g)],
        out_specs=row_spec(d),
        compiler_params=_params(("arbitrary",), 48),
    )(attn, cb, u, halo, gates, x, wa, wc, wo, conv_w, conv_b, g)


def _mlp_kernel(h_ref, g1_ref, wu_ref, wd_ref, g2_ref, o_ref, n_ref, acc_ref):
    j = pl.program_id(1)

    @pl.when(j == 0)
    def _():
        n_ref[...] = _rms(h_ref[...], g1_ref[...]).astype(BF16)
        acc_ref[...] = jnp.zeros_like(acc_ref)

    a = jnp.dot(n_ref[...], wu_ref[...], preferred_element_type=F32)
    a = jnp.square(jnp.maximum(a, 0.0)).astype(BF16)
    acc_ref[...] += jnp.dot(a, wd_ref[...], preferred_element_type=F32)

    @pl.when(j == pl.num_programs(1) - 1)
    def _():
        o_ref[...] = h_ref[...] + _rms(acc_ref[...], g2_ref[...])


def _mlp(h, g1, w_up, w_down, g2, tm, tf):
    t, d = h.shape
    ff = w_up.shape[1]
    return pl.pallas_call(
        _mlp_kernel,
        out_shape=jax.ShapeDtypeStruct((t, d), F32),
        grid=(t // tm, ff // tf),
        in_specs=[
            pl.BlockSpec((tm, d), lambda i, j: (i, 0)),
            pl.BlockSpec((1, d), lambda i, j: (0, 0)),
            pl.BlockSpec((d, tf), lambda i, j: (0, j)),
            pl.BlockSpec((tf, d), lambda i, j: (j, 0)),
            pl.BlockSpec((1, d), lambda i, j: (0, 0)),
        ],
        out_specs=pl.BlockSpec((tm, d), lambda i, j: (i, 0)),
        scratch_shapes=[pltpu.VMEM((tm, d), BF16), pltpu.VMEM((tm, d), F32)],
        compiler_params=_params(("arbitrary", "arbitrary"), 56),
    )(h, g1, w_up, w_down, g2)


def _pe_kernel(row_splits, h_ref, p_ref, wg_ref, wp_ref, g_ref, o_ref):
    rows = h_ref.shape[0] // row_splits
    for r in range(row_splits):
        sl = slice(r * rows, (r + 1) * rows)
        h = h_ref[sl, :]
        gate = jax.nn.sigmoid(jnp.dot(h.astype(BF16), wg_ref[...], preferred_element_type=F32))
        pe = jnp.dot(p_ref[sl, :].astype(BF16), wp_ref[...], preferred_element_type=F32)
        o_ref[sl, :] = h + _rms(gate * pe, g_ref[...])


def _pe(h, p, wg, wp, g, tm):
    t, d = h.shape
    resident = lambda a: pl.BlockSpec(a.shape, lambda i: (0,) * a.ndim, pipeline_mode=pl.Buffered(1))
    return pl.pallas_call(
        functools.partial(_pe_kernel, 2 if tm % 512 == 0 else 1),
        out_shape=jax.ShapeDtypeStruct((t, d), F32),
        grid=(t // tm,),
        in_specs=[pl.BlockSpec((tm, d), lambda i: (i, 0)), pl.BlockSpec((tm, p.shape[1]), lambda i: (i, 0)),
                  resident(wg), resident(wp), resident(g)],
        out_specs=pl.BlockSpec((tm, d), lambda i: (i, 0)),
        compiler_params=_params(("arbitrary",), 48),
    )(h, p, wg, wp, g)


def _layer(x, pe_in, seq, tiles_per_seq, attn_fn, state, w, tm_proj, tm_mix, tm_mlp, tf_mlp, tm_pe):
    t = x.shape[0]
    qkv, *kv, cb, u, gates = _in_proj(x, w["g_pre_mix"], w["w_in"], min(tm_proj, t), tiles_per_seq)
    attn = attn_fn(qkv)
    h = _mix(attn, cb, u, state, gates, x, w["w_attn_out"], w["w_conv_out"], w["w_o"], w["conv_w"],
             w["conv_b"], w["g_post_mix"], tm_mix, seq)
    h = _mlp(h, w["g_pre_mlp"], w["w_up"], w["w_down"], w["g_post_mlp"], min(tm_mlp, t), tf_mlp)
    h = _pe(h, pe_in, w["w_pe_gate"], w["w_pe"], w["g_pe"], min(tm_pe, t))
    return h, kv, u.reshape(t // seq, seq, -1)[:, -2:]


def kernel(x_prompt, x_sample, cache_k, cache_v, state_conv, p_prompt, p_sample, w_in, rel_table, w_attn_out,
           conv_w, conv_b, w_conv_out, w_o, g_pre_mix, g_post_mix, g_pre_mlp, g_post_mlp, w_up, w_down, w_pe,
           w_pe_gate, g_pe):
    depth = w_in.shape[0]
    bp, sp, d = x_prompt.shape
    bs, ss, _ = x_sample.shape
    lc, heads = cache_k.shape[2], cache_k.shape[3]
    lp = min(LEFT_LEN, sp)
    assert sp % lp == 0 and sp % ATTN_TQ == 0 and lc == LEFT_LEN and ss <= CHUNK
    assert rel_table.shape[2] == 2 * REL_CLIP + 1

    hp = x_prompt.reshape(bp * sp, d)
    hs = x_sample.reshape(bs * ss, d)
    outs = [[] for _ in range(6)]
    for i in range(depth):
        w = {
            "w_in": w_in[i].astype(BF16), "w_attn_out": w_attn_out[i].astype(BF16),
            "w_conv_out": w_conv_out[i].astype(BF16), "w_o": w_o[i].astype(BF16),
            "w_up": w_up[i].astype(BF16), "w_down": w_down[i].astype(BF16),
            "w_pe": w_pe[i].astype(BF16), "w_pe_gate": w_pe_gate[i].astype(BF16),
            "conv_w": conv_w[i], "conv_b": conv_b[i][None],
            "g_pre_mix": g_pre_mix[i][None], "g_post_mix": g_post_mix[i][None],
            "g_pre_mlp": g_pre_mlp[i][None], "g_post_mlp": g_post_mlp[i][None], "g_pe": g_pe[i][None],
        }
        rel_rows = _rel_rows(rel_table[i])
        kct = cache_k[i].transpose(0, 2, 3, 1).reshape(bs, heads * HEAD_DIM, lc)
        vct = cache_v[i].transpose(0, 2, 3, 1).reshape(bs, heads * HEAD_DIM, lc)

        hp, (ktp, vtp), cpr = _layer(
            hp, p_prompt[i].reshape(bp * sp, -1), sp, sp // lp,
            lambda qkv: _attn_prompt(qkv, rel_rows, bp, sp), None, w,
            tm_proj=lp, tm_mix=256, tm_mlp=512, tf_mlp=1024, tm_pe=512)
        hs, (kvs,), csm = _layer(
            hs, p_sample[i].reshape(bs * ss, -1), ss, None,
            lambda qkv: _attn_sample(qkv, kct, vct, rel_rows, bs, ss), state_conv[i], w,
            tm_proj=512, tm_mix=ss, tm_mlp=512, tf_mlp=1024, tm_pe=512)
        kpr = ktp.reshape(bp, heads, HEAD_DIM, lp).transpose(0, 3, 1, 2)
        vpr = vtp.reshape(bp, heads, HEAD_DIM, lp).transpose(0, 3, 1, 2)
        ksm = kvs[:, :heads * HEAD_DIM].reshape(bs, ss, heads, HEAD_DIM)
        vsm = kvs[:, heads * HEAD_DIM:].reshape(bs, ss, heads, HEAD_DIM)
        for lst, val in zip(outs, (kpr, vpr, cpr, ksm, vsm, csm)):
            lst.append(val)
    k_prompt, v_prompt, conv_prompt, k_sample, v_sample, conv_sample = [jnp.stack(o) for o in outs]
    return (hp.reshape(bp, sp, d), hs.reshape(bs, ss, d), k_prompt, v_prompt, conv_prompt,
            k_sample, v_sample, conv_sample)
```

```python
import functools

import jax
import jax.numpy as jnp
from jax import lax
from jax.experimental import pallas as pl
from jax.experimental.pallas import tpu as pltpu

F32 = jnp.float32
BF16 = jnp.bfloat16

CHUNK = 64
LEFT_CHUNKS = 8
LEFT_LEN = LEFT_CHUNKS * CHUNK
BAND = (LEFT_CHUNKS + 1) * CHUNK
HEAD_DIM = 64
REL_CLIP = 128
EPS = 1e-6
NEG = -1e30

LANES = 128
ATTN_TQ = 256
ATTN_NK = LEFT_LEN + ATTN_TQ
ATTN_SLOTS = ATTN_NK // ATTN_TQ
REL_ROW = ATTN_NK + ATTN_TQ
MIB = 1024 * 1024


def _rms(x, g):
    return x * lax.rsqrt(jnp.mean(x * x, axis=-1, keepdims=True) + EPS) * g


def _params(sem, vmem_mib, flags=None):
    return pltpu.CompilerParams(dimension_semantics=sem, vmem_limit_bytes=vmem_mib * MIB, flags=flags)


def _in_proj_kernel(tiles_per_seq, x_ref, g_ref, w_ref, qkv_ref, *refs):
    if tiles_per_seq is None:
        kv_ref, cb_ref, u_ref, gates_ref, n_ref, cc_ref = refs
    else:
        vtb_ref, kt_ref, vt_ref, cb_ref, u_ref, gates_ref, n_ref, cc_ref = refs
    i, j = pl.program_id(0), pl.program_id(1)

    @pl.when(j == 0)
    def _():
        n_ref[...] = _rms(x_ref[...], g_ref[...]).astype(BF16)

    def proj():
        return jnp.dot(n_ref[...], w_ref[...], preferred_element_type=F32)

    @pl.when(j == 0)
    def _():
        qkv_ref[...] = (proj() * (HEAD_DIM ** -0.5)).astype(BF16)

    if tiles_per_seq is None:
        @pl.when((j == 1) | (j == 2))
        def _():
            z = proj()
            qkv_ref[...] = z.astype(BF16)
            kv_ref[...] = z
    else:
        is_tail = i % tiles_per_seq == tiles_per_seq - 1

        @pl.when((j == 1) & jnp.logical_not(is_tail))
        def _():
            qkv_ref[...] = proj().astype(BF16)

        @pl.when((j == 1) & is_tail)
        def _():
            z = proj()
            qkv_ref[...] = z.astype(BF16)
            kt_ref[...] = z.T

        @pl.when((j == 2) & jnp.logical_not(is_tail))
        def _():
            vtb_ref[...] = proj().T.astype(BF16)

        @pl.when((j == 2) & is_tail)
        def _():
            zt = proj().T
            vtb_ref[...] = zt.astype(BF16)
            vt_ref[...] = zt

    @pl.when(j == 3)
    def _():
        cb_ref[...] = proj().astype(BF16)

    @pl.when(j == 4)
    def _():
        cc_ref[...] = proj()

    @pl.when(j == 5)
    def _():
        u_ref[...] = cc_ref[...] * proj()

    @pl.when(j >= 6)
    def _():
        gates_ref[...] = jax.nn.sigmoid(proj()).astype(BF16)


def _in_proj(x, g, w_in, tm, tiles_per_seq):
    t, d = x.shape
    tn = 1024
    nj = w_in.shape[1] // tn
    assert nj == 10 and t % tm == 0
    if tiles_per_seq is None:
        row_major = 3
        kv_shapes = (jax.ShapeDtypeStruct((t, 2 * tn), F32),)
        kv_specs = (pl.BlockSpec((tm, tn), lambda i, j: (i, jnp.clip(j - 1, 0, 1))),)
    else:
        row_major = 2
        nseq = t // (tm * tiles_per_seq)
        tail_spec = pl.BlockSpec((None, tn, tm), lambda i, j: (i // tiles_per_seq, 0, 0))
        kv_shapes = (jax.ShapeDtypeStruct((tn, t), BF16),
                     jax.ShapeDtypeStruct((nseq, tn, tm), F32),
                     jax.ShapeDtypeStruct((nseq, tn, tm), F32))
        kv_specs = (pl.BlockSpec((tn, tm), lambda i, j: (0, i)), tail_spec, tail_spec)
    out_shape = (
        jax.ShapeDtypeStruct((t, row_major * tn), BF16),
        *kv_shapes,
        jax.ShapeDtypeStruct((t, tn), BF16),
        jax.ShapeDtypeStruct((t, tn), F32),
        jax.ShapeDtypeStruct((t, 4 * tn), BF16),
    )
    return pl.pallas_call(
        functools.partial(_in_proj_kernel, tiles_per_seq),
        out_shape=out_shape,
        grid=(t // tm, nj),
        in_specs=[
            pl.BlockSpec((tm, d), lambda i, j: (i, 0)),
            pl.BlockSpec((1, d), lambda i, j: (0, 0)),
            pl.BlockSpec((d, tn), lambda i, j: (0, j)),
        ],
        out_specs=(
            pl.BlockSpec((tm, tn), lambda i, j: (i, jnp.minimum(j, row_major - 1))),
            *kv_specs,
            pl.BlockSpec((tm, tn), lambda i, j: (i, 0)),
            pl.BlockSpec((tm, tn), lambda i, j: (i, 0)),
            pl.BlockSpec((tm, tn), lambda i, j: (i, jnp.clip(j - 6, 0, 3))),
        ),
        scratch_shapes=[pltpu.VMEM((tm, d), BF16), pltpu.VMEM((tm, tn), F32)],
        compiler_params=_params(("arbitrary", "arbitrary"), 56),
    )(x, g, w_in)


def _rel_rows(rel_table):
    h, n_rel = rel_table.shape
    lo = LEFT_LEN - REL_CLIP
    hi = ATTN_NK - lo - n_rel
    first = jnp.broadcast_to(rel_table[:, :1], (h, lo))
    last = jnp.broadcast_to(rel_table[:, -1:], (h, hi))
    negative = jnp.broadcast_to(rel_table[:, :1], (h, REL_ROW - ATTN_NK))
    return jnp.concatenate([first, rel_table, last, negative], axis=1).astype(F32)


def _toeplitz(row, nq):
    return pltpu.roll(jnp.broadcast_to(row, (nq, row.shape[1])), 0, 1, stride=1, stride_axis=0)


def _softmax_pv(score_blocks, value_blocks, value_dims):
    m = functools.reduce(jnp.maximum, [jnp.max(s, axis=-1, keepdims=True) for s in score_blocks])
    ps = [jnp.exp(s - m) for s in score_blocks]
    denom = functools.reduce(jnp.add, [jnp.sum(p, axis=-1, keepdims=True) for p in ps])
    acc = functools.reduce(jnp.add, [
        lax.dot_general(p.astype(BF16), v, (((1,), (vd,)), ((), ())), preferred_element_type=F32)
        for p, v, vd in zip(ps, value_blocks, value_dims)])
    return acc / denom


def _head_pair(q_pair, key_blocks, key_dims, value_blocks, value_dims, bias_fn):
    lane = lax.broadcasted_iota(jnp.int32, q_pair.shape, 1)
    outs = []
    for hh in range(2):
        own = (lane >= HEAD_DIM) if hh else (lane < HEAD_DIM)
        qm = jnp.where(own, q_pair, jnp.zeros_like(q_pair))
        scores = [
            lax.dot_general(qm, kb, (((1,), (kd,)), ((), ())), preferred_element_type=F32) + bias_fn(hh, n)
            for n, (kb, kd) in enumerate(zip(key_blocks, key_dims))]
        outs.append(_softmax_pv(scores, value_blocks, value_dims))
    return jnp.where(lane < HEAD_DIM, outs[0], outs[1])


def _attn_prompt_kernel(q_ref, *refs):
    k_refs = refs[:ATTN_SLOTS]
    vt_refs = refs[ATTN_SLOTS:2 * ATTN_SLOTS]
    rel_ref, o_ref, bias_ref = refs[2 * ATTN_SLOTS:]
    t = pl.program_id(1)

    @pl.when((pl.program_id(0) == 0) & (t == 0))
    def _():
        qi = lax.broadcasted_iota(jnp.int32, (ATTN_TQ, ATTN_NK), 0)
        kj = lax.broadcasted_iota(jnp.int32, (ATTN_TQ, ATTN_NK), 1)
        first = (qi // CHUNK) * CHUNK
        visible = (kj >= first) & (kj < first + BAND)
        for h in range(bias_ref.shape[0]):
            bias_ref[h] = jnp.where(visible, _toeplitz(rel_ref[h:h + 1, :], ATTN_TQ)[:, :ATTN_NK], NEG).T

    for first_slot in range(ATTN_SLOTS):
        tiles_before = ATTN_SLOTS - 1 - first_slot
        cond = (t >= tiles_before) if first_slot == 0 else (t == tiles_before)

        @pl.when(cond)
        def _(first_slot=first_slot):
            _band_heads(q_ref, k_refs[first_slot:], vt_refs[first_slot:], bias_ref, first_slot * ATTN_TQ, o_ref)


def _band_heads(q_ref, k_refs, vt_refs, bias_ref, key0, o_ref):
    nk = len(k_refs) * ATTN_TQ
    heads = 2 * (q_ref.shape[1] // LANES)
    lane = lax.broadcasted_iota(jnp.int32, (ATTN_TQ, LANES), 1)
    feat = lax.broadcasted_iota(jnp.int32, (LANES, nk), 0)

    def scores(h):
        sl = slice(h // 2 * LANES, (h // 2 + 1) * LANES)
        own_lane = (lane >= HEAD_DIM) if h % 2 else (lane < HEAD_DIM)
        q_pair = q_ref[:, sl]
        qm = jnp.where(own_lane, q_pair, jnp.zeros_like(q_pair))
        keys = jnp.concatenate([r[:, sl] for r in k_refs], axis=0)
        st = lax.dot_general(keys, qm, (((1,), (1,)), ((), ())), preferred_element_type=F32)
        return st + bias_ref[h, key0:key0 + nk, :]

    def weighted_values(h, st):
        sl = slice(h // 2 * LANES, (h // 2 + 1) * LANES)
        own_feat = (feat >= HEAD_DIM) if h % 2 else (feat < HEAD_DIM)
        pt = jnp.exp(st - jnp.max(st, axis=0, keepdims=True)).astype(BF16)
        vals_t = jnp.concatenate([r[sl, :] for r in vt_refs], axis=1)
        vm = jnp.where(own_feat, vals_t, jnp.ones_like(vals_t))
        acc = jnp.dot(vm, pt, preferred_element_type=F32)
        lo, hi = acc[:HEAD_DIM], acc[HEAD_DIM:]
        return hi / lo if h % 2 else lo / hi

    halves = []
    st = scores(0)
    for h in range(heads):
        st_next = scores(h + 1) if h + 1 < heads else None
        halves.append(weighted_values(h, st))
        st = st_next
        if h % 2:
            sl = slice(h // 2 * LANES, (h // 2 + 1) * LANES)
            o_ref[:, sl] = jnp.concatenate(halves, axis=0).T.astype(o_ref.dtype)
            halves = []


def _attn_prompt(qk, v_t, rel_rows, batch, seq):
    t, width = v_t.shape[1], v_t.shape[0]
    nt = seq // ATTN_TQ
    blk = (ATTN_TQ, width)
    slot_row = lambda b, i, s: b * nt + jnp.maximum(i - (ATTN_SLOTS - 1) + s, 0)
    return pl.pallas_call(
        _attn_prompt_kernel,
        out_shape=jax.ShapeDtypeStruct((t, width), BF16),
        grid=(batch, nt),
        in_specs=[pl.BlockSpec(blk, lambda b, i: (b * nt + i, 0))]
        + [pl.BlockSpec(blk, lambda b, i, s=s: (slot_row(b, i, s), 1)) for s in range(ATTN_SLOTS)]
        + [pl.BlockSpec((width, ATTN_TQ), lambda b, i, s=s: (0, slot_row(b, i, s))) for s in range(ATTN_SLOTS)]
        + [pl.BlockSpec(rel_rows.shape, lambda b, i: (0, 0))],
        out_specs=pl.BlockSpec(blk, lambda b, i: (b * nt + i, 0)),
        scratch_shapes=[pltpu.VMEM((rel_rows.shape[0], ATTN_NK, ATTN_TQ), F32)],
        compiler_params=_params(("arbitrary", "arbitrary"), 40, ),
    )(qk, *([qk] * ATTN_SLOTS), *([v_t] * ATTN_SLOTS), rel_rows)


def _attn_sample_kernel(q_ref, kn_ref, vn_ref, kct_ref, vct_ref, rel_ref, o_ref, bias_ref):
    nq = q_ref.shape[0]
    lc = kct_ref.shape[1]

    @pl.when(pl.program_id(0) == 0)
    def _():
        for h in range(bias_ref.shape[0]):
            bias_ref[h] = _toeplitz(rel_ref[h:h + 1, :], nq)[:, :bias_ref.shape[2]]

    for hp in range(q_ref.shape[1] // LANES):
        sl = slice(hp * LANES, (hp + 1) * LANES)
        keys = [kct_ref[sl, :].astype(BF16), kn_ref[:, sl]]
        vals = [vct_ref[sl, :].astype(BF16), vn_ref[:, sl]]
        bias_fn = lambda hh, n, hp=hp: bias_ref[2 * hp + hh][:, (0, lc)[n]:(lc, lc + nq)[n]]
        o_ref[:, sl] = _head_pair(q_ref[:, sl], keys, [0, 1], vals, [1, 0], bias_fn).astype(o_ref.dtype)


def _attn_sample(qkv, cache_kt, cache_vt, rel_rows, batch, seq):
    t, width = qkv.shape[0], qkv.shape[1] // 3
    lc = cache_kt.shape[2]
    blk = (seq, width)
    bias_cols = -(-(lc + seq) // LANES) * LANES
    return pl.pallas_call(
        _attn_sample_kernel,
        out_shape=jax.ShapeDtypeStruct((t, width), BF16),
        grid=(batch,),
        in_specs=[
            pl.BlockSpec(blk, lambda b: (b, 0)),
            pl.BlockSpec(blk, lambda b: (b, 1)),
            pl.BlockSpec(blk, lambda b: (b, 2)),
            pl.BlockSpec((None, width, lc), lambda b: (b, 0, 0)),
            pl.BlockSpec((None, width, lc), lambda b: (b, 0, 0)),
            pl.BlockSpec(rel_rows.shape, lambda b: (0, 0)),
        ],
        out_specs=pl.BlockSpec(blk, lambda b: (b, 0)),
        scratch_shapes=[pltpu.VMEM((rel_rows.shape[0], seq, bias_cols), F32)],
        compiler_params=_params(("arbitrary",), 32),
    )(qkv, qkv, qkv, cache_kt, cache_vt, rel_rows)


def _mix_kernel(seq_rows, attn_ref, cb_ref, u_ref, halo_ref, gates_ref, x_ref, wa_ref, wc_ref, wo_ref,
                cw_ref, cbias_ref, g_ref, o_ref):
    tm, d = x_ref.shape
    u = u_ref[...]
    if halo_ref.shape[0] == 2:
        prev2, prev1 = halo_ref[0:1, :], halo_ref[1:2, :]
    else:
        keep = ((pl.program_id(0) * tm) % seq_rows != 0).astype(F32)
        nh = halo_ref.shape[0]
        prev2, prev1 = halo_ref[nh - 2:nh - 1, :] * keep, halo_ref[nh - 1:nh, :] * keep
    row = lax.broadcasted_iota(jnp.int32, u.shape, 0)
    um1 = jnp.where(row == 0, prev1, pltpu.roll(u, 1, axis=0))
    um2 = jnp.where(row == 0, prev2, jnp.where(row == 1, prev1, pltpu.roll(u, 2, axis=0)))
    conv = cbias_ref[...] + cw_ref[0:1, :] * um2 + cw_ref[1:2, :] * um1 + cw_ref[2:3, :] * u
    ya = jnp.dot(attn_ref[...], wa_ref[...], preferred_element_type=F32)
    yb = jnp.dot((cb_ref[...].astype(F32) * conv).astype(BF16), wc_ref[...], preferred_element_type=F32)
    merged = gates_ref[:, :d].astype(F32) * ya + gates_ref[:, d:].astype(F32) * yb
    mix = jnp.dot(merged.astype(BF16), wo_ref[...], preferred_element_type=F32)
    o_ref[...] = x_ref[...] + _rms(mix, g_ref[...])


def _mix(attn, cb, u, state, gates, x, wa, wc, wo, conv_w, conv_b, g, tm, seq_rows):
    t, d = x.shape
    c = u.shape[1]
    halo_rows = 8
    if state is None:
        halo, halo_spec = u, pl.BlockSpec(
            (halo_rows, c), lambda i: (jnp.maximum(i * (tm // halo_rows) - 1, 0), 0))
    else:
        assert tm == seq_rows
        halo, halo_spec = state, pl.BlockSpec((None,) + state.shape[1:], lambda i: (i, 0, 0))
    row_spec = lambda w: pl.BlockSpec((tm, w), lambda i: (i, 0))
    resident = lambda a: pl.BlockSpec(a.shape, lambda i: (0,) * a.ndim, pipeline_mode=pl.Buffered(1))
    return pl.pallas_call(
        functools.partial(_mix_kernel, seq_rows),
        out_shape=jax.ShapeDtypeStruct((t, d), F32),
        grid=(t // tm,),
        in_specs=[row_spec(attn.shape[1]), row_spec(c), row_spec(c), halo_spec, row_spec(2 * d), row_spec(d),
                  resident(wa), resident(wc), resident(wo), resident(conv_w), resident(conv_b), resident(g)],
        out_specs=row_spec(d),
        compiler_params=_params(("arbitrary",), 48),
    )(attn, cb, u, halo, gates, x, wa, wc, wo, conv_w, conv_b, g)


def _mlp_kernel(h_ref, g1_ref, wu_ref, wd_ref, g2_ref, o_ref, n_ref, acc_ref):
    j = pl.program_id(1)

    @pl.when(j == 0)
    def _():
        n_ref[...] = _rms(h_ref[...], g1_ref[...]).astype(BF16)
        acc_ref[...] = jnp.zeros_like(acc_ref)

    a = jnp.dot(n_ref[...], wu_ref[...], preferred_element_type=F32)
    a = jnp.square(jnp.maximum(a, 0.0)).astype(BF16)
    acc_ref[...] += jnp.dot(a, wd_ref[...], preferred_element_type=F32)

    @pl.when(j == pl.num_programs(1) - 1)
    def _():
        o_ref[...] = h_ref[...] + _rms(acc_ref[...], g2_ref[...])


def _mlp(h, g1, w_up, w_down, g2, tm, tf):
    t, d = h.shape
    ff = w_up.shape[1]
    return pl.pallas_call(
        _mlp_kernel,
        out_shape=jax.ShapeDtypeStruct((t, d), F32),
        grid=(t // tm, ff // tf),
        in_specs=[
            pl.BlockSpec((tm, d), lambda i, j: (i, 0)),
            pl.BlockSpec((1, d), lambda i, j: (0, 0)),
            pl.BlockSpec((d, tf), lambda i, j: (0, j)),
            pl.BlockSpec((tf, d), lambda i, j: (j, 0)),
            pl.BlockSpec((1, d), lambda i, j: (0, 0)),
        ],
        out_specs=pl.BlockSpec((tm, d), lambda i, j: (i, 0)),
        scratch_shapes=[pltpu.VMEM((tm, d), BF16), pltpu.VMEM((tm, d), F32)],
        compiler_params=_params(("arbitrary", "arbitrary"), 56),
    )(h, g1, w_up, w_down, g2)


def _pe_kernel(row_splits, h_ref, p_ref, wg_ref, wp_ref, g_ref, o_ref):
    rows = h_ref.shape[0] // row_splits
    for r in range(row_splits):
        sl = slice(r * rows, (r + 1) * rows)
        h = h_ref[sl, :]
        gate = jax.nn.sigmoid(jnp.dot(h.astype(BF16), wg_ref[...], preferred_element_type=F32))
        pe = jnp.dot(p_ref[sl, :].astype(BF16), wp_ref[...], preferred_element_type=F32)
        o_ref[sl, :] = h + _rms(gate * pe, g_ref[...])


def _pe(h, p, wg, wp, g, tm):
    t, d = h.shape
    resident = lambda a: pl.BlockSpec(a.shape, lambda i: (0,) * a.ndim, pipeline_mode=pl.Buffered(1))
    return pl.pallas_call(
        functools.partial(_pe_kernel, 2 if tm % 512 == 0 else 1),
        out_shape=jax.ShapeDtypeStruct((t, d), F32),
        grid=(t // tm,),
        in_specs=[pl.BlockSpec((tm, d), lambda i: (i, 0)), pl.BlockSpec((tm, p.shape[1]), lambda i: (i, 0)),
                  resident(wg), resident(wp), resident(g)],
        out_specs=pl.BlockSpec((tm, d), lambda i: (i, 0)),
        compiler_params=_params(("arbitrary",), 48),
    )(h, p, wg, wp, g)


def _layer(x, pe_in, seq, tiles_per_seq, attn_fn, state, w, tm_proj, tm_mix, tm_mlp, tf_mlp, tm_pe):
    t = x.shape[0]
    qkv, *kv, cb, u, gates = _in_proj(x, w["g_pre_mix"], w["w_in"], min(tm_proj, t), tiles_per_seq)
    attn = attn_fn(qkv, kv)
    h = _mix(attn, cb, u, state, gates, x, w["w_attn_out"], w["w_conv_out"], w["w_o"], w["conv_w"],
             w["conv_b"], w["g_post_mix"], tm_mix, seq)
    h = _mlp(h, w["g_pre_mlp"], w["w_up"], w["w_down"], w["g_post_mlp"], min(tm_mlp, t), tf_mlp)
    h = _pe(h, pe_in, w["w_pe_gate"], w["w_pe"], w["g_pe"], min(tm_pe, t))
    return h, kv, u.reshape(t // seq, seq, -1)[:, -2:]


def kernel(x_prompt, x_sample, cache_k, cache_v, state_conv, p_prompt, p_sample, w_in, rel_table, w_attn_out,
           conv_w, conv_b, w_conv_out, w_o, g_pre_mix, g_post_mix, g_pre_mlp, g_post_mlp, w_up, w_down, w_pe,
           w_pe_gate, g_pe):
    depth = w_in.shape[0]
    bp, sp, d = x_prompt.shape
    bs, ss, _ = x_sample.shape
    lc, heads = cache_k.shape[2], cache_k.shape[3]
    lp = min(LEFT_LEN, sp)
    assert sp % lp == 0 and sp % ATTN_TQ == 0 and lc == LEFT_LEN and ss <= CHUNK
    assert rel_table.shape[2] == 2 * REL_CLIP + 1

    hp = x_prompt.reshape(bp * sp, d)
    hs = x_sample.reshape(bs * ss, d)
    outs = [[] for _ in range(6)]
    for i in range(depth):
        w = {
            "w_in": w_in[i].astype(BF16), "w_attn_out": w_attn_out[i].astype(BF16),
            "w_conv_out": w_conv_out[i].astype(BF16), "w_o": w_o[i].astype(BF16),
            "w_up": w_up[i].astype(BF16), "w_down": w_down[i].astype(BF16),
            "w_pe": w_pe[i].astype(BF16), "w_pe_gate": w_pe_gate[i].astype(BF16),
            "conv_w": conv_w[i], "conv_b": conv_b[i][None],
            "g_pre_mix": g_pre_mix[i][None], "g_post_mix": g_post_mix[i][None],
            "g_pre_mlp": g_pre_mlp[i][None], "g_post_mlp": g_post_mlp[i][None], "g_pe": g_pe[i][None],
        }
        rel_rows = _rel_rows(rel_table[i])
        kct = cache_k[i].transpose(0, 2, 3, 1).reshape(bs, heads * HEAD_DIM, lc)
        vct = cache_v[i].transpose(0, 2, 3, 1).reshape(bs, heads * HEAD_DIM, lc)

        hp, (_, ktp, vtp), cpr = _layer(
            hp, p_prompt[i].reshape(bp * sp, -1), sp, sp // lp,
            lambda qk, kv: _attn_prompt(qk, kv[0], rel_rows, bp, sp), None, w,
            tm_proj=lp, tm_mix=256, tm_mlp=512, tf_mlp=1024, tm_pe=512)
        hs, (kvs,), csm = _layer(
            hs, p_sample[i].reshape(bs * ss, -1), ss, None,
            lambda qkv, kv: _attn_sample(qkv, kct, vct, rel_rows, bs, ss), state_conv[i], w,
            tm_proj=512, tm_mix=ss, tm_mlp=512, tf_mlp=1024, tm_pe=512)
        kpr = ktp.reshape(bp, heads, HEAD_DIM, lp).transpose(0, 3, 1, 2)
        vpr = vtp.reshape(bp, heads, HEAD_DIM, lp).transpose(0, 3, 1, 2)
        ksm = kvs[:, :heads * HEAD_DIM].reshape(bs, ss, heads, HEAD_DIM)
        vsm = kvs[:, heads * HEAD_DIM:].reshape(bs, ss, heads, HEAD_DIM)
        for lst, val in zip(outs, (kpr, vpr, cpr, ksm, vsm, csm)):
            lst.append(val)
    k_prompt, v_prompt, conv_prompt, k_sample, v_sample, conv_sample = [jnp.stack(o) for o in outs]
    return (hp.reshape(bp, sp, d), hs.reshape(bs, ss, d), k_prompt, v_prompt, conv_prompt,
            k_sample, v_sample, conv_sample)
```

```python
import functools

import jax
import jax.numpy as jnp
from jax import lax
from jax.experimental import pallas as pl
from jax.experimental.pallas import tpu as pltpu

F32 = jnp.float32
BF16 = jnp.bfloat16

CHUNK = 64
LEFT_CHUNKS = 8
LEFT_LEN = LEFT_CHUNKS * CHUNK
BAND = (LEFT_CHUNKS + 1) * CHUNK
HEAD_DIM = 64
REL_CLIP = 128
EPS = 1e-6
NEG = -1e30

LANES = 128
ATTN_TQ = 256
ATTN_NK = LEFT_LEN + ATTN_TQ
ATTN_SLOTS = ATTN_NK // ATTN_TQ
REL_ROW = ATTN_NK + ATTN_TQ
MIB = 1024 * 1024


def _rms(x, g):
    return x * lax.rsqrt(jnp.mean(x * x, axis=-1, keepdims=True) + EPS) * g


def _params(sem, vmem_mib, flags=None):
    return pltpu.CompilerParams(dimension_semantics=sem, vmem_limit_bytes=vmem_mib * MIB, flags=flags)


def _in_proj_kernel(tiles_per_seq, x_ref, g_ref, w_ref, qkv_ref, *refs):
    if tiles_per_seq is None:
        kv_ref, cb_ref, u_ref, gates_ref, n_ref, cc_ref = refs
    else:
        vtb_ref, kt_ref, vt_ref, cb_ref, u_ref, gates_ref, n_ref, cc_ref = refs
    i, j = pl.program_id(0), pl.program_id(1)

    @pl.when(j == 0)
    def _():
        n_ref[...] = _rms(x_ref[...], g_ref[...]).astype(BF16)

    def proj():
        return jnp.dot(n_ref[...], w_ref[...], preferred_element_type=F32)

    @pl.when(j == 0)
    def _():
        qkv_ref[...] = (proj() * (HEAD_DIM ** -0.5)).astype(BF16)

    if tiles_per_seq is None:
        @pl.when((j == 1) | (j == 2))
        def _():
            z = proj()
            qkv_ref[...] = z.astype(BF16)
            kv_ref[...] = z
    else:
        is_tail = i % tiles_per_seq == tiles_per_seq - 1

        @pl.when((j == 1) & jnp.logical_not(is_tail))
        def _():
            qkv_ref[...] = proj().astype(BF16)

        @pl.when((j == 1) & is_tail)
        def _():
            z = proj()
            qkv_ref[...] = z.astype(BF16)
            kt_ref[...] = z.T

        @pl.when((j == 2) & jnp.logical_not(is_tail))
        def _():
            vtb_ref[...] = proj().T.astype(BF16)

        @pl.when((j == 2) & is_tail)
        def _():
            zt = proj().T
            vtb_ref[...] = zt.astype(BF16)
            vt_ref[...] = zt

    @pl.when(j == 3)
    def _():
        cb_ref[...] = proj().astype(BF16)

    @pl.when(j == 4)
    def _():
        cc_ref[...] = proj()

    @pl.when(j == 5)
    def _():
        u_ref[...] = cc_ref[...] * proj()

    @pl.when(j >= 6)
    def _():
        gates_ref[...] = jax.nn.sigmoid(proj()).astype(BF16)


def _in_proj(x, g, w_in, tm, tiles_per_seq):
    t, d = x.shape
    tn = 1024
    nj = w_in.shape[1] // tn
    assert nj == 10 and t % tm == 0

    def lagged(i, j, jw):
        return jnp.where(j >= jw, i, jnp.maximum(i - 1, 0))

    if tiles_per_seq is None:
        row_major = 3
        kv_shapes = (jax.ShapeDtypeStruct((t, 2 * tn), F32),)
        kv_specs = (pl.BlockSpec((tm, tn), lambda i, j: (i, jnp.clip(j - 1, 0, 1))),)
    else:
        row_major = 2
        nseq = t // (tm * tiles_per_seq)
        tail_spec = pl.BlockSpec((None, tn, tm), lambda i, j: (i // tiles_per_seq, 0, 0))
        kv_shapes = (jax.ShapeDtypeStruct((tn, t), BF16),
                     jax.ShapeDtypeStruct((nseq, tn, tm), F32),
                     jax.ShapeDtypeStruct((nseq, tn, tm), F32))
        kv_specs = (pl.BlockSpec((tn, tm), lambda i, j: (0, lagged(i, j, 2))), tail_spec, tail_spec)
    out_shape = (
        jax.ShapeDtypeStruct((t, row_major * tn), BF16),
        *kv_shapes,
        jax.ShapeDtypeStruct((t, tn), BF16),
        jax.ShapeDtypeStruct((t, tn), F32),
        jax.ShapeDtypeStruct((t, 4 * tn), BF16),
    )
    return pl.pallas_call(
        functools.partial(_in_proj_kernel, tiles_per_seq),
        out_shape=out_shape,
        grid=(t // tm, nj),
        in_specs=[
            pl.BlockSpec((tm, d), lambda i, j: (i, 0)),
            pl.BlockSpec((1, d), lambda i, j: (0, 0)),
            pl.BlockSpec((d, tn), lambda i, j: (0, j)),
        ],
        out_specs=(
            pl.BlockSpec((tm, tn), lambda i, j: (i, jnp.minimum(j, row_major - 1))),
            *kv_specs,
            pl.BlockSpec((tm, tn), lambda i, j: (lagged(i, j, 3), 0)),
            pl.BlockSpec((tm, tn), lambda i, j: (lagged(i, j, 5), 0)),
            pl.BlockSpec((tm, tn), lambda i, j: (i, jnp.clip(j - 6, 0, 3))),
        ),
        scratch_shapes=[pltpu.VMEM((tm, d), BF16), pltpu.VMEM((tm, tn), F32)],
        compiler_params=_params(("arbitrary", "arbitrary"), 56),
    )(x, g, w_in)


def _rel_rows(rel_table):
    h, n_rel = rel_table.shape
    lo = LEFT_LEN - REL_CLIP
    hi = ATTN_NK - lo - n_rel
    first = jnp.broadcast_to(rel_table[:, :1], (h, lo))
    last = jnp.broadcast_to(rel_table[:, -1:], (h, hi))
    negative = jnp.broadcast_to(rel_table[:, :1], (h, REL_ROW - ATTN_NK))
    return jnp.concatenate([first, rel_table, last, negative], axis=1).astype(F32)


def _toeplitz(row, nq):
    return pltpu.roll(jnp.broadcast_to(row, (nq, row.shape[1])), 0, 1, stride=1, stride_axis=0)


def _attn_prompt_kernel(q_ref, *refs):
    k_refs = refs[:ATTN_SLOTS]
    vt_refs = refs[ATTN_SLOTS:2 * ATTN_SLOTS]
    rel_ref, o_ref, bias_ref = refs[2 * ATTN_SLOTS:]
    t = pl.program_id(1)

    @pl.when((pl.program_id(0) == 0) & (t == 0))
    def _():
        qi = lax.broadcasted_iota(jnp.int32, (ATTN_TQ, ATTN_NK), 0)
        kj = lax.broadcasted_iota(jnp.int32, (ATTN_TQ, ATTN_NK), 1)
        first = (qi // CHUNK) * CHUNK
        visible = (kj >= first) & (kj < first + BAND)
        for h in range(bias_ref.shape[0]):
            bias_ref[h] = jnp.where(visible, _toeplitz(rel_ref[h:h + 1, :], ATTN_TQ)[:, :ATTN_NK], NEG).T

    for first_slot in range(ATTN_SLOTS):
        tiles_before = ATTN_SLOTS - 1 - first_slot
        cond = (t >= tiles_before) if first_slot == 0 else (t == tiles_before)

        @pl.when(cond)
        def _(first_slot=first_slot):
            _band_heads(q_ref, k_refs[first_slot:], vt_refs[first_slot:], bias_ref, first_slot * ATTN_TQ, o_ref)


def _band_heads(q_ref, k_refs, vt_refs, bias_ref, key0, o_ref):
    nk = len(k_refs) * ATTN_TQ
    heads = 2 * (q_ref.shape[1] // LANES)
    lane = lax.broadcasted_iota(jnp.int32, (ATTN_TQ, LANES), 1)
    feat = lax.broadcasted_iota(jnp.int32, (LANES, nk), 0)

    def scores(h):
        sl = slice(h // 2 * LANES, (h // 2 + 1) * LANES)
        own_lane = (lane >= HEAD_DIM) if h % 2 else (lane < HEAD_DIM)
        q_pair = q_ref[:, sl]
        qm = jnp.where(own_lane, q_pair, jnp.zeros_like(q_pair))
        keys = jnp.concatenate([r[:, sl] for r in k_refs], axis=0)
        st = lax.dot_general(keys, qm, (((1,), (1,)), ((), ())), preferred_element_type=F32)
        return st + bias_ref[h, key0:key0 + nk, :]

    def weighted_values(h, st):
        sl = slice(h // 2 * LANES, (h // 2 + 1) * LANES)
        own_feat = (feat >= HEAD_DIM) if h % 2 else (feat < HEAD_DIM)
        pt = jnp.exp(st - jnp.max(st, axis=0, keepdims=True)).astype(BF16)
        vals_t = jnp.concatenate([r[sl, :] for r in vt_refs], axis=1)
        vm = jnp.where(own_feat, vals_t, jnp.ones_like(vals_t))
        acc = jnp.dot(vm, pt, preferred_element_type=F32)
        lo, hi = acc[:HEAD_DIM], acc[HEAD_DIM:]
        return hi / lo if h % 2 else lo / hi

    halves = []
    ahead = 3
    pending = [scores(h) for h in range(ahead)]
    for h in range(heads):
        if h + ahead < heads:
            pending.append(scores(h + ahead))
        halves.append(weighted_values(h, pending.pop(0)))
        if h % 2:
            sl = slice(h // 2 * LANES, (h // 2 + 1) * LANES)
            o_ref[:, sl] = jnp.concatenate(halves, axis=0).T.astype(o_ref.dtype)
            halves = []


def _attn_prompt(qk, v_t, rel_rows, batch, seq):
    t, width = v_t.shape[1], v_t.shape[0]
    nt = seq // ATTN_TQ
    blk = (ATTN_TQ, width)
    slot_row = lambda b, i, s: b * nt + jnp.maximum(i - (ATTN_SLOTS - 1) + s, 0)
    return pl.pallas_call(
        _attn_prompt_kernel,
        out_shape=jax.ShapeDtypeStruct((t, width), BF16),
        grid=(batch, nt),
        in_specs=[pl.BlockSpec(blk, lambda b, i: (b * nt + i, 0))]
        + [pl.BlockSpec(blk, lambda b, i, s=s: (slot_row(b, i, s), 1)) for s in range(ATTN_SLOTS)]
        + [pl.BlockSpec((width, ATTN_TQ), lambda b, i, s=s: (0, slot_row(b, i, s))) for s in range(ATTN_SLOTS)]
        + [pl.BlockSpec(rel_rows.shape, lambda b, i: (0, 0))],
        out_specs=pl.BlockSpec(blk, lambda b, i: (b * nt + i, 0)),
        scratch_shapes=[pltpu.VMEM((rel_rows.shape[0], ATTN_NK, ATTN_TQ), F32)],
        compiler_params=_params(("arbitrary", "arbitrary"), 40, ),
    )(qk, *([qk] * ATTN_SLOTS), *([v_t] * ATTN_SLOTS), rel_rows)


def _attn_sample_kernel(q_ref, kn_ref, vn_ref, kct_ref, vct_ref, rel_ref, o_ref, bias_ref):
    nq = q_ref.shape[0]
    lc = kct_ref.shape[1]

    @pl.when(pl.program_id(0) == 0)
    def _():
        for h in range(bias_ref.shape[0]):
            bias_ref[h] = _toeplitz(rel_ref[h:h + 1, :], nq)[:, :bias_ref.shape[2]]

    lane = lax.broadcasted_iota(jnp.int32, (nq, LANES), 1)
    nt_dims = (((1,), (1,)), ((), ()))

    def scores(h):
        sl = slice(h // 2 * LANES, (h // 2 + 1) * LANES)
        own = (lane >= HEAD_DIM) if h % 2 else (lane < HEAD_DIM)
        qm = jnp.where(own, q_ref[:, sl], jnp.zeros((nq, LANES), BF16))
        s_cache = jnp.dot(qm, kct_ref[sl, :].astype(BF16), preferred_element_type=F32)
        s_new = lax.dot_general(qm, kn_ref[:, sl], nt_dims, preferred_element_type=F32)
        return s_cache + bias_ref[h, :, :lc], s_new + bias_ref[h, :, lc:lc + nq]

    def weighted_values(h, s_cache, s_new):
        sl = slice(h // 2 * LANES, (h // 2 + 1) * LANES)
        m = jnp.maximum(jnp.max(s_cache, axis=-1, keepdims=True), jnp.max(s_new, axis=-1, keepdims=True))
        p_cache, p_new = jnp.exp(s_cache - m), jnp.exp(s_new - m)
        denom = jnp.sum(p_cache, axis=-1, keepdims=True) + jnp.sum(p_new, axis=-1, keepdims=True)
        acc = lax.dot_general(p_cache.astype(BF16), vct_ref[sl, :].astype(BF16), nt_dims,
                              preferred_element_type=F32)
        acc = acc + jnp.dot(p_new.astype(BF16), vn_ref[:, sl], preferred_element_type=F32)
        return acc / denom

    heads = bias_ref.shape[0]
    st = scores(0)
    for h in range(heads):
        st_next = scores(h + 1) if h + 1 < heads else None
        out = weighted_values(h, *st)
        st = st_next
        if h % 2:
            sl = slice(h // 2 * LANES, (h // 2 + 1) * LANES)
            o_ref[:, sl] = jnp.where(lane < HEAD_DIM, prev_out, out).astype(o_ref.dtype)
        prev_out = out


def _attn_sample(qkv, cache_kt, cache_vt, rel_rows, batch, seq):
    t, width = qkv.shape[0], qkv.shape[1] // 3
    lc = cache_kt.shape[2]
    blk = (seq, width)
    bias_cols = -(-(lc + seq) // LANES) * LANES
    return pl.pallas_call(
        _attn_sample_kernel,
        out_shape=jax.ShapeDtypeStruct((t, width), BF16),
        grid=(batch,),
        in_specs=[
            pl.BlockSpec(blk, lambda b: (b, 0)),
            pl.BlockSpec(blk, lambda b: (b, 1)),
            pl.BlockSpec(blk, lambda b: (b, 2)),
            pl.BlockSpec((None, width, lc), lambda b: (b, 0, 0)),
            pl.BlockSpec((None, width, lc), lambda b: (b, 0, 0)),
            pl.BlockSpec(rel_rows.shape, lambda b: (0, 0)),
        ],
        out_specs=pl.BlockSpec(blk, lambda b: (b, 0)),
        scratch_shapes=[pltpu.VMEM((rel_rows.shape[0], seq, bias_cols), F32)],
        compiler_params=_params(("arbitrary",), 32),
    )(qkv, qkv, qkv, cache_kt, cache_vt, rel_rows)


def _mix_kernel(seq_rows, attn_ref, cb_ref, u_ref, halo_ref, gates_ref, x_ref, wa_ref, wc_ref, wo_ref,
                cw_ref, cbias_ref, g_ref, o_ref):
    tm, d = x_ref.shape
    u = u_ref[...]
    row = lax.broadcasted_iota(jnp.int32, u.shape, 0)
    if len(halo_ref.shape) == 3:
        nseq = halo_ref.shape[0]
        state = halo_ref[...]
        per_row = lambda r: jnp.broadcast_to(state[:, r:r + 1, :], (nseq, seq_rows, u.shape[1])).reshape(u.shape)
        prev2, prev1 = per_row(0), per_row(1)
        row = row % seq_rows
    else:
        keep = ((pl.program_id(0) * tm) % seq_rows != 0).astype(F32)
        nh = halo_ref.shape[0]
        prev2, prev1 = halo_ref[nh - 2:nh - 1, :] * keep, halo_ref[nh - 1:nh, :] * keep
    um1 = jnp.where(row == 0, prev1, pltpu.roll(u, 1, axis=0))
    um2 = jnp.where(row == 0, prev2, jnp.where(row == 1, prev1, pltpu.roll(u, 2, axis=0)))
    conv = cbias_ref[...] + cw_ref[0:1, :] * um2 + cw_ref[1:2, :] * um1 + cw_ref[2:3, :] * u
    ya = jnp.dot(attn_ref[...], wa_ref[...], preferred_element_type=F32)
    yb = jnp.dot((cb_ref[...].astype(F32) * conv).astype(BF16), wc_ref[...], preferred_element_type=F32)
    merged = gates_ref[:, :d].astype(F32) * ya + gates_ref[:, d:].astype(F32) * yb
    mix = jnp.dot(merged.astype(BF16), wo_ref[...], preferred_element_type=F32)
    o_ref[...] = x_ref[...] + _rms(mix, g_ref[...])


def _mix(attn, cb, u, state, gates, x, wa, wc, wo, conv_w, conv_b, g, tm, seq_rows):
    t, d = x.shape
    c = u.shape[1]
    halo_rows = 8
    if state is None:
        halo, halo_spec = u, pl.BlockSpec(
            (halo_rows, c), lambda i: (jnp.maximum(i * (tm // halo_rows) - 1, 0), 0))
    else:
        assert tm % seq_rows == 0
        halo, halo_spec = state, pl.BlockSpec((tm // seq_rows,) + state.shape[1:], lambda i: (i, 0, 0))
    row_spec = lambda w: pl.BlockSpec((tm, w), lambda i: (i, 0))
    resident = lambda a: pl.BlockSpec(a.shape, lambda i: (0,) * a.ndim, pipeline_mode=pl.Buffered(1))
    return pl.pallas_call(
        functools.partial(_mix_kernel, seq_rows),
        out_shape=jax.ShapeDtypeStruct((t, d), F32),
        grid=(t // tm,),
        in_specs=[row_spec(attn.shape[1]), row_spec(c), row_spec(c), halo_spec, row_spec(2 * d), row_spec(d),
                  resident(wa), resident(wc), resident(wo), resident(conv_w), resident(conv_b), resident(g)],
        out_specs=row_spec(d),
        compiler_params=_params(("arbitrary",), 48),
    )(attn, cb, u, halo, gates, x, wa, wc, wo, conv_w, conv_b, g)


def _mlp_kernel(h_ref, g1_ref, wu_ref, wd_ref, g2_ref, o_ref, n_ref, acc_ref):
    j = pl.program_id(1)

    @pl.when(j == 0)
    def _():
        n_ref[...] = _rms(h_ref[...], g1_ref[...]).astype(BF16)
        acc_ref[...] = jnp.zeros_like(acc_ref)

    a = jnp.dot(n_ref[...], wu_ref[...], preferred_element_type=F32)
    a = jnp.square(jnp.maximum(a, 0.0)).astype(BF16)
    acc_ref[...] += jnp.dot(a, wd_ref[...], preferred_element_type=F32)

    @pl.when(j == pl.num_programs(1) - 1)
    def _():
        o_ref[...] = h_ref[...] + _rms(acc_ref[...], g2_ref[...])


def _mlp(h, g1, w_up, w_down, g2, tm, tf):
    t, d = h.shape
    ff = w_up.shape[1]
    return pl.pallas_call(
        _mlp_kernel,
        out_shape=jax.ShapeDtypeStruct((t, d), F32),
        grid=(t // tm, ff // tf),
        in_specs=[
            pl.BlockSpec((tm, d), lambda i, j: (i, 0)),
            pl.BlockSpec((1, d), lambda i, j: (0, 0)),
            pl.BlockSpec((d, tf), lambda i, j: (0, j)),
            pl.BlockSpec((tf, d), lambda i, j: (j, 0)),
            pl.BlockSpec((1, d), lambda i, j: (0, 0)),
        ],
        out_specs=pl.BlockSpec((tm, d), lambda i, j: (i, 0)),
        scratch_shapes=[pltpu.VMEM((tm, d), BF16), pltpu.VMEM((tm, d), F32)],
        compiler_params=_params(("arbitrary", "arbitrary"), 56),
    )(h, g1, w_up, w_down, g2)


def _pe_kernel(row_splits, h_ref, p_ref, wg_ref, wp_ref, g_ref, o_ref):
    rows = h_ref.shape[0] // row_splits
    for r in range(row_splits):
        sl = slice(r * rows, (r + 1) * rows)
        h = h_ref[sl, :]
        gate = jax.nn.sigmoid(jnp.dot(h.astype(BF16), wg_ref[...], preferred_element_type=F32))
        pe = jnp.dot(p_ref[sl, :].astype(BF16), wp_ref[...], preferred_element_type=F32)
        o_ref[sl, :] = h + _rms(gate * pe, g_ref[...])


def _pe(h, p, wg, wp, g, tm):
    t, d = h.shape
    resident = lambda a: pl.BlockSpec(a.shape, lambda i: (0,) * a.ndim, pipeline_mode=pl.Buffered(1))
    return pl.pallas_call(
        functools.partial(_pe_kernel, 2 if tm % 512 == 0 else 1),
        out_shape=jax.ShapeDtypeStruct((t, d), F32),
        grid=(t // tm,),
        in_specs=[pl.BlockSpec((tm, d), lambda i: (i, 0)), pl.BlockSpec((tm, p.shape[1]), lambda i: (i, 0)),
                  resident(wg), resident(wp), resident(g)],
        out_specs=pl.BlockSpec((tm, d), lambda i: (i, 0)),
        compiler_params=_params(("arbitrary",), 48),
    )(h, p, wg, wp, g)


def _layer(x, pe_in, seq, tiles_per_seq, attn_fn, state, w, tm_proj, tm_mix, tm_mlp, tf_mlp, tm_pe):
    t = x.shape[0]
    qkv, *kv, cb, u, gates = _in_proj(x, w["g_pre_mix"], w["w_in"], min(tm_proj, t), tiles_per_seq)
    attn = attn_fn(qkv, kv)
    h = _mix(attn, cb, u, state, gates, x, w["w_attn_out"], w["w_conv_out"], w["w_o"], w["conv_w"],
             w["conv_b"], w["g_post_mix"], min(tm_mix, t), seq)
    h = _mlp(h, w["g_pre_mlp"], w["w_up"], w["w_down"], w["g_post_mlp"], min(tm_mlp, t), tf_mlp)
    h = _pe(h, pe_in, w["w_pe_gate"], w["w_pe"], w["g_pe"], min(tm_pe, t))
    return h, kv, u.reshape(t // seq, seq, -1)[:, -2:]


def kernel(x_prompt, x_sample, cache_k, cache_v, state_conv, p_prompt, p_sample, w_in, rel_table, w_attn_out,
           conv_w, conv_b, w_conv_out, w_o, g_pre_mix, g_post_mix, g_pre_mlp, g_post_mlp, w_up, w_down, w_pe,
           w_pe_gate, g_pe):
    depth = w_in.shape[0]
    bp, sp, d = x_prompt.shape
    bs, ss, _ = x_sample.shape
    lc, heads = cache_k.shape[2], cache_k.shape[3]
    lp = min(LEFT_LEN, sp)
    assert sp % lp == 0 and sp % ATTN_TQ == 0 and lc == LEFT_LEN and ss <= CHUNK
    assert rel_table.shape[2] == 2 * REL_CLIP + 1

    hp = x_prompt.reshape(bp * sp, d)
    hs = x_sample.reshape(bs * ss, d)
    outs = [[] for _ in range(6)]
    for i in range(depth):
        w = {
            "w_in": w_in[i].astype(BF16), "w_attn_out": w_attn_out[i].astype(BF16),
            "w_conv_out": w_conv_out[i].astype(BF16), "w_o": w_o[i].astype(BF16),
            "w_up": w_up[i].astype(BF16), "w_down": w_down[i].astype(BF16),
            "w_pe": w_pe[i].astype(BF16), "w_pe_gate": w_pe_gate[i].astype(BF16),
            "conv_w": conv_w[i], "conv_b": conv_b[i][None],
            "g_pre_mix": g_pre_mix[i][None], "g_post_mix": g_post_mix[i][None],
            "g_pre_mlp": g_pre_mlp[i][None], "g_post_mlp": g_post_mlp[i][None], "g_pe": g_pe[i][None],
        }
        rel_rows = _rel_rows(rel_table[i])
        kct = cache_k[i].transpose(0, 2, 3, 1).reshape(bs, heads * HEAD_DIM, lc)
        vct = cache_v[i].transpose(0, 2, 3, 1).reshape(bs, heads * HEAD_DIM, lc)

        hp, (_, ktp, vtp), cpr = _layer(
            hp, p_prompt[i].reshape(bp * sp, -1), sp, sp // lp,
            lambda qk, kv: _attn_prompt(qk, kv[0], rel_rows, bp, sp), None, w,
            tm_proj=lp, tm_mix=256, tm_mlp=512, tf_mlp=1024, tm_pe=512)
        hs, (kvs,), csm = _layer(
            hs, p_sample[i].reshape(bs * ss, -1), ss, None,
            lambda qkv, kv: _attn_sample(qkv, kct, vct, rel_rows, bs, ss), state_conv[i], w,
            tm_proj=512, tm_mix=256, tm_mlp=512, tf_mlp=1024, tm_pe=512)
        kpr = ktp.reshape(bp, heads, HEAD_DIM, lp).transpose(0, 3, 1, 2)
        vpr = vtp.reshape(bp, heads, HEAD_DIM, lp).transpose(0, 3, 1, 2)
        ksm = kvs[:, :heads * HEAD_DIM].reshape(bs, ss, heads, HEAD_DIM)
        vsm = kvs[:, heads * HEAD_DIM:].reshape(bs, ss, heads, HEAD_DIM)
        for lst, val in zip(outs, (kpr, vpr, cpr, ksm, vsm, csm)):
            lst.append(val)
    k_prompt, v_prompt, conv_prompt, k_sample, v_sample, conv_sample = [jnp.stack(o) for o in outs]
    return (hp.reshape(bp, sp, d), hs.reshape(bs, ss, d), k_prompt, v_prompt, conv_prompt,
            k_sample, v_sample, conv_sample)
```

```python
import functools

import jax
import jax.numpy as jnp
from jax import lax
from jax.experimental import pallas as pl
from jax.experimental.pallas import tpu as pltpu

F32 = jnp.float32
BF16 = jnp.bfloat16

CHUNK = 64
LEFT_CHUNKS = 8
LEFT_LEN = LEFT_CHUNKS * CHUNK
BAND = (LEFT_CHUNKS + 1) * CHUNK
HEAD_DIM = 64
REL_CLIP = 128
EPS = 1e-6
NEG = -1e30

LANES = 128
ATTN_TQ = 256
ATTN_NK = LEFT_LEN + ATTN_TQ
ATTN_SLOTS = ATTN_NK // ATTN_TQ
REL_ROW = ATTN_NK + ATTN_TQ
IN_PROJ_TN = 2048
MLP_TF = 1024
MIB = 1024 * 1024


def _rms(x, g):
    return x * lax.rsqrt(jnp.mean(x * x, axis=-1, keepdims=True) + EPS) * g


def _params(sem, vmem_mib, flags=None):
    return pltpu.CompilerParams(dimension_semantics=sem, vmem_limit_bytes=vmem_mib * MIB, flags=flags)


def _in_proj_kernel(tiles_per_seq, x_ref, g_ref, w_ref, qk_ref, *refs):
    if tiles_per_seq is None:
        v_ref, kv_ref, cb_ref, u_ref, gates_ref, n_ref = refs
    else:
        v_ref, kt_ref, vt_ref, cb_ref, u_ref, gates_ref, n_ref = refs
        is_tail = pl.program_id(0) % tiles_per_seq == tiles_per_seq - 1
    j = pl.program_id(1)
    half = w_ref.shape[1] // 2

    @pl.when(j == 0)
    def _():
        n_ref[...] = _rms(x_ref[...], g_ref[...]).astype(BF16)

    def proj():
        z = jnp.dot(n_ref[...], w_ref[...], preferred_element_type=F32)
        return z[:, :half], z[:, half:]

    def step(jj, tail, body):
        cond = j == jj
        if tail is not None and tiles_per_seq is not None:
            cond = cond & (is_tail if tail else jnp.logical_not(is_tail))
        pl.when(cond)(body)

    def q_k(tail):
        def body():
            q, k = proj()
            qk_ref[:, :half] = (q * (HEAD_DIM ** -0.5)).astype(BF16)
            qk_ref[:, half:] = k.astype(BF16)
            if tiles_per_seq is None:
                kv_ref[:, :half] = k
            elif tail:
                kt_ref[...] = k.T
        return body

    def v_bg(tail):
        def body():
            v, bg = proj()
            cb_ref[...] = bg.astype(BF16)
            if tiles_per_seq is None:
                v_ref[...] = v.astype(BF16)
                kv_ref[:, half:] = v
            else:
                vt = v.T
                v_ref[...] = vt.astype(BF16)
                if tail:
                    vt_ref[...] = vt
        return body

    def conv_in():
        cg, xin = proj()
        u_ref[...] = cg * xin

    def gate():
        za, zb = proj()
        gates_ref[:, :half] = jax.nn.sigmoid(za).astype(BF16)
        gates_ref[:, half:] = jax.nn.sigmoid(zb).astype(BF16)

    for tail in ((None,) if tiles_per_seq is None else (False, True)):
        step(0, tail, q_k(tail))
        step(1, tail, v_bg(tail))
    step(2, None, conv_in)
    pl.when(j >= 3)(gate)


def _in_proj(x, g, w_steps, tm, tiles_per_seq):
    t, d = x.shape
    nj, _, tn = w_steps.shape
    half = tn // 2
    assert nj == 5 and d == tn and t % tm == 0
    rows = lambda w: pl.BlockSpec((tm, w), lambda i, j: (i, 0))
    if tiles_per_seq is None:
        kv_shapes = (jax.ShapeDtypeStruct((t, half), BF16),
                     jax.ShapeDtypeStruct((t, tn), F32))
        kv_specs = (rows(half), rows(tn))
    else:
        nseq = t // (tm * tiles_per_seq)
        tail_spec = pl.BlockSpec((None, half, tm), lambda i, j: (i // tiles_per_seq, 0, 0))
        kv_shapes = (jax.ShapeDtypeStruct((half, t), BF16),
                     jax.ShapeDtypeStruct((nseq, half, tm), F32),
                     jax.ShapeDtypeStruct((nseq, half, tm), F32))
        kv_specs = (pl.BlockSpec((half, tm), lambda i, j: (0, i)), tail_spec, tail_spec)
    out_shape = (
        jax.ShapeDtypeStruct((t, tn), BF16),
        *kv_shapes,
        jax.ShapeDtypeStruct((t, half), BF16),
        jax.ShapeDtypeStruct((t, half), F32),
        jax.ShapeDtypeStruct((t, 2 * tn), BF16),
    )
    return pl.pallas_call(
        functools.partial(_in_proj_kernel, tiles_per_seq),
        out_shape=out_shape,
        grid=(t // tm, nj),
        in_specs=[
            pl.BlockSpec((tm, d), lambda i, j: (i, 0)),
            pl.BlockSpec((1, d), lambda i, j: (0, 0)),
            pl.BlockSpec((None, d, tn), lambda i, j: (j, 0, 0)),
        ],
        out_specs=(
            rows(tn),
            *kv_specs,
            rows(half),
            rows(half),
            pl.BlockSpec((tm, tn), lambda i, j: (i, jnp.clip(j - 3, 0, 1))),
        ),
        scratch_shapes=[pltpu.VMEM((tm, d), BF16)],
        compiler_params=_params(("arbitrary", "arbitrary"), 56),
    )(x, g, w_steps)


def _rel_rows(rel_table):
    h, n_rel = rel_table.shape
    lo = LEFT_LEN - REL_CLIP
    hi = ATTN_NK - lo - n_rel
    first = jnp.broadcast_to(rel_table[:, :1], (h, lo))
    last = jnp.broadcast_to(rel_table[:, -1:], (h, hi))
    negative = jnp.broadcast_to(rel_table[:, :1], (h, REL_ROW - ATTN_NK))
    return jnp.concatenate([first, rel_table, last, negative], axis=1).astype(F32)


def _toeplitz(row, nq):
    return pltpu.roll(jnp.broadcast_to(row, (nq, row.shape[1])), 0, 1, stride=1, stride_axis=0)


def _attn_prompt_kernel(q_ref, *refs):
    k_refs = refs[:ATTN_SLOTS]
    vt_refs = refs[ATTN_SLOTS:2 * ATTN_SLOTS]
    rel_ref, o_ref, bias_ref = refs[2 * ATTN_SLOTS:]
    t = pl.program_id(1)

    @pl.when((pl.program_id(0) == 0) & (t == 0))
    def _():
        qi = lax.broadcasted_iota(jnp.int32, (ATTN_TQ, ATTN_NK), 0)
        kj = lax.broadcasted_iota(jnp.int32, (ATTN_TQ, ATTN_NK), 1)
        first = (qi // CHUNK) * CHUNK
        visible = (kj >= first) & (kj < first + BAND)
        for h in range(bias_ref.shape[0]):
            bias_ref[h] = jnp.where(visible, _toeplitz(rel_ref[h:h + 1, :], ATTN_TQ)[:, :ATTN_NK], NEG).T

    for first_slot in range(ATTN_SLOTS):
        tiles_before = ATTN_SLOTS - 1 - first_slot
        cond = (t >= tiles_before) if first_slot == 0 else (t == tiles_before)

        @pl.when(cond)
        def _(first_slot=first_slot):
            _band_heads(q_ref, k_refs[first_slot:], vt_refs[first_slot:], bias_ref, first_slot * ATTN_TQ, o_ref)


def _band_heads(q_ref, k_refs, vt_refs, bias_ref, key0, o_ref):
    nk = len(k_refs) * ATTN_TQ
    heads = 2 * (q_ref.shape[1] // LANES)
    lane = lax.broadcasted_iota(jnp.int32, (ATTN_TQ, LANES), 1)
    feat = lax.broadcasted_iota(jnp.int32, (LANES, nk), 0)

    def scores(h):
        sl = slice(h // 2 * LANES, (h // 2 + 1) * LANES)
        own_lane = (lane >= HEAD_DIM) if h % 2 else (lane < HEAD_DIM)
        q_pair = q_ref[:, sl]
        qm = jnp.where(own_lane, q_pair, jnp.zeros_like(q_pair))
        keys = jnp.concatenate([r[:, sl] for r in k_refs], axis=0)
        st = lax.dot_general(keys, qm, (((1,), (1,)), ((), ())), preferred_element_type=F32)
        return st + bias_ref[h, key0:key0 + nk, :]

    def weighted_values(h, st):
        sl = slice(h // 2 * LANES, (h // 2 + 1) * LANES)
        own_feat = (feat >= HEAD_DIM) if h % 2 else (feat < HEAD_DIM)
        pt = jnp.exp(st - jnp.max(st, axis=0, keepdims=True)).astype(BF16)
        vals_t = jnp.concatenate([r[sl, :] for r in vt_refs], axis=1)
        vm = jnp.where(own_feat, vals_t, jnp.ones_like(vals_t))
        acc = jnp.dot(vm, pt, preferred_element_type=F32)
        lo, hi = acc[:HEAD_DIM], acc[HEAD_DIM:]
        return hi / lo if h % 2 else lo / hi

    halves = []
    ahead = 3
    pending = [scores(h) for h in range(ahead)]
    for h in range(heads):
        if h + ahead < heads:
            pending.append(scores(h + ahead))
        halves.append(weighted_values(h, pending.pop(0)))
        if h % 2:
            sl = slice(h // 2 * LANES, (h // 2 + 1) * LANES)
            o_ref[:, sl] = jnp.concatenate(halves, axis=0).T.astype(o_ref.dtype)
            halves = []


def _attn_prompt(qk, v_t, rel_rows, batch, seq):
    t, width = v_t.shape[1], v_t.shape[0]
    nt = seq // ATTN_TQ
    blk = (ATTN_TQ, width)
    slot_row = lambda b, i, s: b * nt + jnp.maximum(i - (ATTN_SLOTS - 1) + s, 0)
    return pl.pallas_call(
        _attn_prompt_kernel,
        out_shape=jax.ShapeDtypeStruct((t, width), BF16),
        grid=(batch, nt),
        in_specs=[pl.BlockSpec(blk, lambda b, i: (b * nt + i, 0))]
        + [pl.BlockSpec(blk, lambda b, i, s=s: (slot_row(b, i, s), 1)) for s in range(ATTN_SLOTS)]
        + [pl.BlockSpec((width, ATTN_TQ), lambda b, i, s=s: (0, slot_row(b, i, s))) for s in range(ATTN_SLOTS)]
        + [pl.BlockSpec(rel_rows.shape, lambda b, i: (0, 0))],
        out_specs=pl.BlockSpec(blk, lambda b, i: (b * nt + i, 0)),
        scratch_shapes=[pltpu.VMEM((rel_rows.shape[0], ATTN_NK, ATTN_TQ), F32)],
        compiler_params=_params(("arbitrary", "arbitrary"), 40, ),
    )(qk, *([qk] * ATTN_SLOTS), *([v_t] * ATTN_SLOTS), rel_rows)


def _attn_sample_kernel(q_ref, kn_ref, vn_ref, kct_ref, vct_ref, rel_ref, o_ref, bias_ref):
    nq = q_ref.shape[0]
    lc = kct_ref.shape[1]

    @pl.when(pl.program_id(0) == 0)
    def _():
        for h in range(bias_ref.shape[0]):
            bias_ref[h] = _toeplitz(rel_ref[h:h + 1, :], nq)[:, :bias_ref.shape[2]]

    lane = lax.broadcasted_iota(jnp.int32, (nq, LANES), 1)
    nt_dims = (((1,), (1,)), ((), ()))

    def scores(h):
        sl = slice(h // 2 * LANES, (h // 2 + 1) * LANES)
        own = (lane >= HEAD_DIM) if h % 2 else (lane < HEAD_DIM)
        qm = jnp.where(own, q_ref[:, sl], jnp.zeros((nq, LANES), BF16))
        s_cache = jnp.dot(qm, kct_ref[sl, :].astype(BF16), preferred_element_type=F32)
        s_new = lax.dot_general(qm, kn_ref[:, sl], nt_dims, preferred_element_type=F32)
        return s_cache + bias_ref[h, :, :lc], s_new + bias_ref[h, :, lc:lc + nq]

    def weighted_values(h, s_cache, s_new):
        sl = slice(h // 2 * LANES, (h // 2 + 1) * LANES)
        m = jnp.maximum(jnp.max(s_cache, axis=-1, keepdims=True), jnp.max(s_new, axis=-1, keepdims=True))
        p_cache, p_new = jnp.exp(s_cache - m), jnp.exp(s_new - m)
        denom = jnp.sum(p_cache, axis=-1, keepdims=True) + jnp.sum(p_new, axis=-1, keepdims=True)
        acc = lax.dot_general(p_cache.astype(BF16), vct_ref[sl, :].astype(BF16), nt_dims,
                              preferred_element_type=F32)
        acc = acc + jnp.dot(p_new.astype(BF16), vn_ref[:, sl], preferred_element_type=F32)
        return acc / denom

    heads = bias_ref.shape[0]
    st = scores(0)
    for h in range(heads):
        st_next = scores(h + 1) if h + 1 < heads else None
        out = weighted_values(h, *st)
        st = st_next
        if h % 2:
            sl = slice(h // 2 * LANES, (h // 2 + 1) * LANES)
            o_ref[:, sl] = jnp.where(lane < HEAD_DIM, prev_out, out).astype(o_ref.dtype)
        prev_out = out


def _attn_sample(qk, v, cache_kt, cache_vt, rel_rows, batch, seq):
    t, width = v.shape
    lc = cache_kt.shape[2]
    blk = (seq, width)
    bias_cols = -(-(lc + seq) // LANES) * LANES
    return pl.pallas_call(
        _attn_sample_kernel,
        out_shape=jax.ShapeDtypeStruct((t, width), BF16),
        grid=(batch,),
        in_specs=[
            pl.BlockSpec(blk, lambda b: (b, 0)),
            pl.BlockSpec(blk, lambda b: (b, 1)),
            pl.BlockSpec(blk, lambda b: (b, 0)),
            pl.BlockSpec((None, width, lc), lambda b: (b, 0, 0)),
            pl.BlockSpec((None, width, lc), lambda b: (b, 0, 0)),
            pl.BlockSpec(rel_rows.shape, lambda b: (0, 0)),
        ],
        out_specs=pl.BlockSpec(blk, lambda b: (b, 0)),
        scratch_shapes=[pltpu.VMEM((rel_rows.shape[0], seq, bias_cols), F32)],
        compiler_params=_params(("arbitrary",), 32),
    )(qk, qk, v, cache_kt, cache_vt, rel_rows)


def _mix_kernel(seq_rows, attn_ref, cb_ref, u_ref, halo_ref, gates_ref, x_ref, wa_ref, wc_ref, wo_ref,
                cw_ref, cbias_ref, g_ref, o_ref):
    tm, d = x_ref.shape
    u = u_ref[...]
    row = lax.broadcasted_iota(jnp.int32, u.shape, 0)
    if len(halo_ref.shape) == 3:
        nseq = halo_ref.shape[0]
        state = halo_ref[...]
        per_row = lambda r: jnp.broadcast_to(state[:, r:r + 1, :], (nseq, seq_rows, u.shape[1])).reshape(u.shape)
        prev2, prev1 = per_row(0), per_row(1)
        row = row % seq_rows
    else:
        keep = ((pl.program_id(0) * tm) % seq_rows != 0).astype(F32)
        nh = halo_ref.shape[0]
        prev2, prev1 = halo_ref[nh - 2:nh - 1, :] * keep, halo_ref[nh - 1:nh, :] * keep
    um1 = jnp.where(row == 0, prev1, pltpu.roll(u, 1, axis=0))
    um2 = jnp.where(row == 0, prev2, jnp.where(row == 1, prev1, pltpu.roll(u, 2, axis=0)))
    conv = cbias_ref[...] + cw_ref[0:1, :] * um2 + cw_ref[1:2, :] * um1 + cw_ref[2:3, :] * u
    ya = jnp.dot(attn_ref[...], wa_ref[...], preferred_element_type=F32)
    yb = jnp.dot((cb_ref[...].astype(F32) * conv).astype(BF16), wc_ref[...], preferred_element_type=F32)
    merged = gates_ref[:, :d].astype(F32) * ya + gates_ref[:, d:].astype(F32) * yb
    mix = jnp.dot(merged.astype(BF16), wo_ref[...], preferred_element_type=F32)
    o_ref[...] = x_ref[...] + _rms(mix, g_ref[...])


def _mix(attn, cb, u, state, gates, x, wa, wc, wo, conv_w, conv_b, g, tm, seq_rows):
    t, d = x.shape
    c = u.shape[1]
    halo_rows = 8
    if state is None:
        halo, halo_spec = u, pl.BlockSpec(
            (halo_rows, c), lambda i: (jnp.maximum(i * (tm // halo_rows) - 1, 0), 0))
    else:
        assert tm % seq_rows == 0
        halo, halo_spec = state, pl.BlockSpec((tm // seq_rows,) + state.shape[1:], lambda i: (i, 0, 0))
    row_spec = lambda w: pl.BlockSpec((tm, w), lambda i: (i, 0))
    resident = lambda a: pl.BlockSpec(a.shape, lambda i: (0,) * a.ndim, pipeline_mode=pl.Buffered(1))
    return pl.pallas_call(
        functools.partial(_mix_kernel, seq_rows),
        out_shape=jax.ShapeDtypeStruct((t, d), F32),
        grid=(t // tm,),
        in_specs=[row_spec(attn.shape[1]), row_spec(c), row_spec(c), halo_spec, row_spec(2 * d), row_spec(d),
                  resident(wa), resident(wc), resident(wo), resident(conv_w), resident(conv_b), resident(g)],
        out_specs=row_spec(d),
        compiler_params=_params(("arbitrary",), 48),
    )(attn, cb, u, halo, gates, x, wa, wc, wo, conv_w, conv_b, g)


def _mlp_kernel(h_ref, g1_ref, wu_ref, wd_ref, g2_ref, o_ref, n_ref, acc_ref):
    j = pl.program_id(1)

    @pl.when(j == 0)
    def _():
        n_ref[...] = _rms(h_ref[...], g1_ref[...]).astype(BF16)
        acc_ref[...] = jnp.zeros_like(acc_ref)

    a = jnp.dot(n_ref[...], wu_ref[...], preferred_element_type=F32)
    a = jnp.square(jnp.maximum(a, 0.0)).astype(BF16)
    acc_ref[...] += jnp.dot(a, wd_ref[...], preferred_element_type=F32)

    @pl.when(j == pl.num_programs(1) - 1)
    def _():
        o_ref[...] = h_ref[...] + _rms(acc_ref[...], g2_ref[...])


def _mlp(h, g1, w_up_steps, w_down, g2, tm):
    t, d = h.shape
    nj, _, tf = w_up_steps.shape
    return pl.pallas_call(
        _mlp_kernel,
        out_shape=jax.ShapeDtypeStruct((t, d), F32),
        grid=(t // tm, nj),
        in_specs=[
            pl.BlockSpec((tm, d), lambda i, j: (i, 0)),
            pl.BlockSpec((1, d), lambda i, j: (0, 0)),
            pl.BlockSpec((None, d, tf), lambda i, j: (j, 0, 0)),
            pl.BlockSpec((tf, d), lambda i, j: (j, 0)),
            pl.BlockSpec((1, d), lambda i, j: (0, 0)),
        ],
        out_specs=pl.BlockSpec((tm, d), lambda i, j: (i, 0)),
        scratch_shapes=[pltpu.VMEM((tm, d), BF16), pltpu.VMEM((tm, d), F32)],
        compiler_params=_params(("arbitrary", "arbitrary"), 56),
    )(h, g1, w_up_steps, w_down, g2)


def _pe_kernel(row_splits, h_ref, p_ref, wg_ref, wp_ref, g_ref, o_ref):
    rows = h_ref.shape[0] // row_splits
    for r in range(row_splits):
        sl = slice(r * rows, (r + 1) * rows)
        h = h_ref[sl, :]
        gate = jax.nn.sigmoid(jnp.dot(h.astype(BF16), wg_ref[...], preferred_element_type=F32))
        pe = jnp.dot(p_ref[sl, :].astype(BF16), wp_ref[...], preferred_element_type=F32)
        o_ref[sl, :] = h + _rms(gate * pe, g_ref[...])


def _pe(h, p, wg, wp, g, tm):
    t, d = h.shape
    resident = lambda a: pl.BlockSpec(a.shape, lambda i: (0,) * a.ndim, pipeline_mode=pl.Buffered(1))
    return pl.pallas_call(
        functools.partial(_pe_kernel, 2 if tm % 512 == 0 else 1),
        out_shape=jax.ShapeDtypeStruct((t, d), F32),
        grid=(t // tm,),
        in_specs=[pl.BlockSpec((tm, d), lambda i: (i, 0)), pl.BlockSpec((tm, p.shape[1]), lambda i: (i, 0)),
                  resident(wg), resident(wp), resident(g)],
        out_specs=pl.BlockSpec((tm, d), lambda i: (i, 0)),
        compiler_params=_params(("arbitrary",), 48),
    )(h, p, wg, wp, g)


def _layer(x, pe_in, seq, tiles_per_seq, attn_fn, state, w, tm_proj, tm_mix, tm_mlp, tm_pe):
    t = x.shape[0]
    qk, *kv, cb, u, gates = _in_proj(x, w["g_pre_mix"], w["w_in"], min(tm_proj, t), tiles_per_seq)
    attn = attn_fn(qk, kv[0])
    h = _mix(attn, cb, u, state, gates, x, w["w_attn_out"], w["w_conv_out"], w["w_o"], w["conv_w"],
             w["conv_b"], w["g_post_mix"], min(tm_mix, t), seq)
    h = _mlp(h, w["g_pre_mlp"], w["w_up"], w["w_down"], w["g_post_mlp"], min(tm_mlp, t))
    h = _pe(h, pe_in, w["w_pe_gate"], w["w_pe"], w["g_pe"], min(tm_pe, t))
    return h, kv, u.reshape(t // seq, seq, -1)[:, -2:]


def kernel(x_prompt, x_sample, cache_k, cache_v, state_conv, p_prompt, p_sample, w_in, rel_table, w_attn_out,
           conv_w, conv_b, w_conv_out, w_o, g_pre_mix, g_post_mix, g_pre_mlp, g_post_mlp, w_up, w_down, w_pe,
           w_pe_gate, g_pe):
    depth = w_in.shape[0]
    bp, sp, d = x_prompt.shape
    bs, ss, _ = x_sample.shape
    lc, heads = cache_k.shape[2], cache_k.shape[3]
    lp = min(LEFT_LEN, sp)
    assert sp % lp == 0 and sp % ATTN_TQ == 0 and lc == LEFT_LEN and ss <= CHUNK
    assert rel_table.shape[2] == 2 * REL_CLIP + 1

    hp = x_prompt.reshape(bp * sp, d)
    hs = x_sample.reshape(bs * ss, d)
    outs = [[] for _ in range(6)]
    for i in range(depth):
        col_steps = lambda a, width: a.astype(BF16).reshape(a.shape[0], -1, width).transpose(1, 0, 2)
        w = {
            "w_in": col_steps(w_in[i], IN_PROJ_TN), "w_attn_out": w_attn_out[i].astype(BF16),
            "w_conv_out": w_conv_out[i].astype(BF16), "w_o": w_o[i].astype(BF16),
            "w_up": col_steps(w_up[i], MLP_TF), "w_down": w_down[i].astype(BF16),
            "w_pe": w_pe[i].astype(BF16), "w_pe_gate": w_pe_gate[i].astype(BF16),
            "conv_w": conv_w[i], "conv_b": conv_b[i][None],
            "g_pre_mix": g_pre_mix[i][None], "g_post_mix": g_post_mix[i][None],
            "g_pre_mlp": g_pre_mlp[i][None], "g_post_mlp": g_post_mlp[i][None], "g_pe": g_pe[i][None],
        }
        rel_rows = _rel_rows(rel_table[i])
        kct = cache_k[i].transpose(0, 2, 3, 1).reshape(bs, heads * HEAD_DIM, lc)
        vct = cache_v[i].transpose(0, 2, 3, 1).reshape(bs, heads * HEAD_DIM, lc)

        hp, (_, ktp, vtp), cpr = _layer(
            hp, p_prompt[i].reshape(bp * sp, -1), sp, sp // lp,
            lambda qk, v_t: _attn_prompt(qk, v_t, rel_rows, bp, sp), None, w,
            tm_proj=lp, tm_mix=256, tm_mlp=512, tm_pe=512)
        hs, (_, kvs), csm = _layer(
            hs, p_sample[i].reshape(bs * ss, -1), ss, None,
            lambda qk, v: _attn_sample(qk, v, kct, vct, rel_rows, bs, ss), state_conv[i], w,
            tm_proj=512, tm_mix=256, tm_mlp=512, tm_pe=512)
        kpr = ktp.reshape(bp, heads, HEAD_DIM, lp).transpose(0, 3, 1, 2)
        vpr = vtp.reshape(bp, heads, HEAD_DIM, lp).transpose(0, 3, 1, 2)
        ksm = kvs[:, :heads * HEAD_DIM].reshape(bs, ss, heads, HEAD_DIM)
        vsm = kvs[:, heads * HEAD_DIM:].reshape(bs, ss, heads, HEAD_DIM)
        for lst, val in zip(outs, (kpr, vpr, cpr, ksm, vsm, csm)):
            lst.append(val)
    k_prompt, v_prompt, conv_prompt, k_sample, v_sample, conv_sample = [jnp.stack(o) for o in outs]
    return (hp.reshape(bp, sp, d), hs.reshape(bs, ss, d), k_prompt, v_prompt, conv_prompt,
            k_sample, v_sample, conv_sample)
```

```python
import functools

import jax
import jax.numpy as jnp
from jax import lax
from jax.experimental import pallas as pl
from jax.experimental.pallas import tpu as pltpu

F32 = jnp.float32
BF16 = jnp.bfloat16

CHUNK = 64
LEFT_CHUNKS = 8
LEFT_LEN = LEFT_CHUNKS * CHUNK
BAND = (LEFT_CHUNKS + 1) * CHUNK
HEAD_DIM = 64
REL_CLIP = 128
EPS = 1e-6
NEG = -1e30

LANES = 128
ATTN_TQ = 256
ATTN_NK = LEFT_LEN + ATTN_TQ
ATTN_SLOTS = ATTN_NK // ATTN_TQ
REL_ROW = ATTN_NK + ATTN_TQ
IN_PROJ_TN = 2048
MLP_TF = 1024
MIB = 1024 * 1024


def _rms(x, g):
    return x * lax.rsqrt(jnp.mean(x * x, axis=-1, keepdims=True) + EPS) * g


def _params(sem, vmem_mib, flags=None):
    return pltpu.CompilerParams(dimension_semantics=sem, vmem_limit_bytes=vmem_mib * MIB, flags=flags)


STEP_QK = 2


def _in_proj_kernel(tiles_per_seq, x_ref, g_ref, w_ref, qk_ref, *refs):
    if tiles_per_seq is None:
        v_ref, kv_ref, cb_ref, u_ref, gates_ref, n_ref = refs
    else:
        v_ref, kt_ref, vt_ref, cb_ref, u_ref, gates_ref, n_ref = refs
        is_tail = pl.program_id(0) % tiles_per_seq == tiles_per_seq - 1
    j = pl.program_id(1)
    half = w_ref.shape[1] // 2

    @pl.when(j == 0)
    def _():
        n_ref[...] = _rms(x_ref[...], g_ref[...]).astype(BF16)

    def proj():
        z = jnp.dot(n_ref[...], w_ref[...], preferred_element_type=F32)
        return z[:, :half], z[:, half:]

    def step(jj, tail, body):
        cond = j == jj
        if tail is not None and tiles_per_seq is not None:
            cond = cond & (is_tail if tail else jnp.logical_not(is_tail))
        pl.when(cond)(body)

    def q_k(tail):
        def body():
            q, k = proj()
            qk_ref[:, :half] = (q * (HEAD_DIM ** -0.5)).astype(BF16)
            qk_ref[:, half:] = k.astype(BF16)
            if tiles_per_seq is None:
                kv_ref[:, :half] = k
            elif tail:
                kt_ref[...] = k.T
        return body

    def v_bg(tail):
        def body():
            v, bg = proj()
            cb_ref[...] = bg.astype(BF16)
            if tiles_per_seq is None:
                v_ref[...] = v.astype(BF16)
                kv_ref[:, half:] = v
            else:
                vt = v.T
                v_ref[...] = vt.astype(BF16)
                if tail:
                    vt_ref[...] = vt
        return body

    def conv_in():
        cg, xin = proj()
        u_ref[...] = cg * xin

    def gate():
        za, zb = proj()
        gates_ref[:, :half] = jax.nn.sigmoid(za).astype(BF16)
        gates_ref[:, half:] = jax.nn.sigmoid(zb).astype(BF16)

    pl.when(j < STEP_QK)(gate)
    for tail in ((None,) if tiles_per_seq is None else (False, True)):
        step(STEP_QK, tail, q_k(tail))
        step(STEP_QK + 1, tail, v_bg(tail))
    step(STEP_QK + 2, None, conv_in)


def _in_proj(x, g, w_in, tm, tiles_per_seq):
    t, d = x.shape
    tn = IN_PROJ_TN
    nj = w_in.shape[1] // tn
    half = tn // 2
    assert nj == 5 and d == tn and t % tm == 0

    def rows(w, jw):
        return pl.BlockSpec((tm, w), lambda i, j: (jnp.where(j >= jw, i, jnp.maximum(i - 1, 0)), 0))

    if tiles_per_seq is None:
        kv_shapes = (jax.ShapeDtypeStruct((t, half), BF16),
                     jax.ShapeDtypeStruct((t, tn), F32))
        kv_specs = (rows(half, STEP_QK + 1), rows(tn, STEP_QK))
    else:
        nseq = t // (tm * tiles_per_seq)
        tail_spec = pl.BlockSpec((None, half, tm), lambda i, j: (i // tiles_per_seq, 0, 0))
        kv_shapes = (jax.ShapeDtypeStruct((half, t), BF16),
                     jax.ShapeDtypeStruct((nseq, half, tm), F32),
                     jax.ShapeDtypeStruct((nseq, half, tm), F32))
        v_t_spec = pl.BlockSpec(
            (half, tm), lambda i, j: (0, jnp.where(j >= STEP_QK + 1, i, jnp.maximum(i - 1, 0))))
        kv_specs = (v_t_spec, tail_spec, tail_spec)
    out_shape = (
        jax.ShapeDtypeStruct((t, tn), BF16),
        *kv_shapes,
        jax.ShapeDtypeStruct((t, half), BF16),
        jax.ShapeDtypeStruct((t, half), F32),
        jax.ShapeDtypeStruct((t, 2 * tn), BF16),
    )
    return pl.pallas_call(
        functools.partial(_in_proj_kernel, tiles_per_seq),
        out_shape=out_shape,
        grid=(t // tm, nj),
        in_specs=[
            pl.BlockSpec((tm, d), lambda i, j: (i, 0)),
            pl.BlockSpec((1, d), lambda i, j: (0, 0)),
            pl.BlockSpec((d, tn), lambda i, j: (0, (j + nj - STEP_QK) % nj)),
        ],
        out_specs=(
            rows(tn, STEP_QK),
            *kv_specs,
            rows(half, STEP_QK + 1),
            rows(half, STEP_QK + 2),
            pl.BlockSpec((tm, tn), lambda i, j: (i, jnp.minimum(j, STEP_QK - 1))),
        ),
        scratch_shapes=[pltpu.VMEM((tm, d), BF16)],
        compiler_params=_params(("arbitrary", "arbitrary"), 56),
    )(x, g, w_in)


def _rel_rows(rel_table):
    h, n_rel = rel_table.shape
    lo = LEFT_LEN - REL_CLIP
    hi = ATTN_NK - lo - n_rel
    first = jnp.broadcast_to(rel_table[:, :1], (h, lo))
    last = jnp.broadcast_to(rel_table[:, -1:], (h, hi))
    negative = jnp.broadcast_to(rel_table[:, :1], (h, REL_ROW - ATTN_NK))
    return jnp.concatenate([first, rel_table, last, negative], axis=1).astype(F32)


def _toeplitz(row, nq):
    return pltpu.roll(jnp.broadcast_to(row, (nq, row.shape[1])), 0, 1, stride=1, stride_axis=0)


def _attn_prompt_kernel(q_ref, *refs):
    k_refs = refs[:ATTN_SLOTS]
    vt_refs = refs[ATTN_SLOTS:2 * ATTN_SLOTS]
    rel_ref, o_ref, bias_ref = refs[2 * ATTN_SLOTS:]
    t = pl.program_id(1)

    @pl.when((pl.program_id(0) == 0) & (t == 0))
    def _():
        qi = lax.broadcasted_iota(jnp.int32, (ATTN_TQ, ATTN_NK), 0)
        kj = lax.broadcasted_iota(jnp.int32, (ATTN_TQ, ATTN_NK), 1)
        first = (qi // CHUNK) * CHUNK
        visible = (kj >= first) & (kj < first + BAND)
        for h in range(bias_ref.shape[0]):
            bias_ref[h] = jnp.where(visible, _toeplitz(rel_ref[h:h + 1, :], ATTN_TQ)[:, :ATTN_NK], NEG).T

    for first_slot in range(ATTN_SLOTS):
        tiles_before = ATTN_SLOTS - 1 - first_slot
        cond = (t >= tiles_before) if first_slot == 0 else (t == tiles_before)

        @pl.when(cond)
        def _(first_slot=first_slot):
            _band_heads(q_ref, k_refs[first_slot:], vt_refs[first_slot:], bias_ref, first_slot * ATTN_TQ, o_ref)


def _band_heads(q_ref, k_refs, vt_refs, bias_ref, key0, o_ref):
    nk = len(k_refs) * ATTN_TQ
    heads = 2 * (q_ref.shape[1] // LANES)
    lane = lax.broadcasted_iota(jnp.int32, (ATTN_TQ, LANES), 1)
    feat = lax.broadcasted_iota(jnp.int32, (LANES, nk), 0)

    def scores(h):
        sl = slice(h // 2 * LANES, (h // 2 + 1) * LANES)
        own_lane = (lane >= HEAD_DIM) if h % 2 else (lane < HEAD_DIM)
        q_pair = q_ref[:, sl]
        qm = jnp.where(own_lane, q_pair, jnp.zeros_like(q_pair))
        keys = jnp.concatenate([r[:, sl] for r in k_refs], axis=0)
        st = lax.dot_general(keys, qm, (((1,), (1,)), ((), ())), preferred_element_type=F32)
        return st + bias_ref[h, key0:key0 + nk, :]

    def weighted_values(h, st):
        sl = slice(h // 2 * LANES, (h // 2 + 1) * LANES)
        own_feat = (feat >= HEAD_DIM) if h % 2 else (feat < HEAD_DIM)
        pt = jnp.exp(st - jnp.max(st, axis=0, keepdims=True)).astype(BF16)
        vals_t = jnp.concatenate([r[sl, :] for r in vt_refs], axis=1)
        vm = jnp.where(own_feat, vals_t, jnp.ones_like(vals_t))
        acc = jnp.dot(vm, pt, preferred_element_type=F32)
        lo, hi = acc[:HEAD_DIM], acc[HEAD_DIM:]
        return hi / lo if h % 2 else lo / hi

    halves = []
    ahead = 3
    pending = [scores(h) for h in range(ahead)]
    for h in range(heads):
        if h + ahead < heads:
            pending.append(scores(h + ahead))
        halves.append(weighted_values(h, pending.pop(0)))
        if h % 2:
            sl = slice(h // 2 * LANES, (h // 2 + 1) * LANES)
            o_ref[:, sl] = jnp.concatenate(halves, axis=0).T.astype(o_ref.dtype)
            halves = []


def _attn_prompt(qk, v_t, rel_rows, batch, seq):
    t, width = v_t.shape[1], v_t.shape[0]
    nt = seq // ATTN_TQ
    blk = (ATTN_TQ, width)
    slot_row = lambda b, i, s: b * nt + jnp.maximum(i - (ATTN_SLOTS - 1) + s, 0)
    return pl.pallas_call(
        _attn_prompt_kernel,
        out_shape=jax.ShapeDtypeStruct((t, width), BF16),
        grid=(batch, nt),
        in_specs=[pl.BlockSpec(blk, lambda b, i: (b * nt + i, 0))]
        + [pl.BlockSpec(blk, lambda b, i, s=s: (slot_row(b, i, s), 1)) for s in range(ATTN_SLOTS)]
        + [pl.BlockSpec((width, ATTN_TQ), lambda b, i, s=s: (0, slot_row(b, i, s))) for s in range(ATTN_SLOTS)]
        + [pl.BlockSpec(rel_rows.shape, lambda b, i: (0, 0))],
        out_specs=pl.BlockSpec(blk, lambda b, i: (b * nt + i, 0)),
        scratch_shapes=[pltpu.VMEM((rel_rows.shape[0], ATTN_NK, ATTN_TQ), F32)],
        compiler_params=_params(("arbitrary", "arbitrary"), 40, ),
    )(qk, *([qk] * ATTN_SLOTS), *([v_t] * ATTN_SLOTS), rel_rows)


def _attn_sample_kernel(q_ref, kn_ref, vn_ref, kct_ref, vct_ref, rel_ref, o_ref, bias_ref):
    nq = q_ref.shape[0]
    lc = kct_ref.shape[1]

    @pl.when(pl.program_id(0) == 0)
    def _():
        for h in range(bias_ref.shape[0]):
            bias_ref[h] = _toeplitz(rel_ref[h:h + 1, :], nq)[:, :bias_ref.shape[2]]

    lane = lax.broadcasted_iota(jnp.int32, (nq, LANES), 1)
    nt_dims = (((1,), (1,)), ((), ()))

    def scores(h):
        sl = slice(h // 2 * LANES, (h // 2 + 1) * LANES)
        own = (lane >= HEAD_DIM) if h % 2 else (lane < HEAD_DIM)
        qm = jnp.where(own, q_ref[:, sl], jnp.zeros((nq, LANES), BF16))
        s_cache = jnp.dot(qm, kct_ref[sl, :].astype(BF16), preferred_element_type=F32)
        s_new = lax.dot_general(qm, kn_ref[:, sl], nt_dims, preferred_element_type=F32)
        return s_cache + bias_ref[h, :, :lc], s_new + bias_ref[h, :, lc:lc + nq]

    def weighted_values(h, s_cache, s_new):
        sl = slice(h // 2 * LANES, (h // 2 + 1) * LANES)
        m = jnp.maximum(jnp.max(s_cache, axis=-1, keepdims=True), jnp.max(s_new, axis=-1, keepdims=True))
        p_cache, p_new = jnp.exp(s_cache - m), jnp.exp(s_new - m)
        denom = jnp.sum(p_cache, axis=-1, keepdims=True) + jnp.sum(p_new, axis=-1, keepdims=True)
        acc = lax.dot_general(p_cache.astype(BF16), vct_ref[sl, :].astype(BF16), nt_dims,
                              preferred_element_type=F32)
        acc = acc + jnp.dot(p_new.astype(BF16), vn_ref[:, sl], preferred_element_type=F32)
        return acc / denom

    heads = bias_ref.shape[0]
    st = scores(0)
    for h in range(heads):
        st_next = scores(h + 1) if h + 1 < heads else None
        out = weighted_values(h, *st)
        st = st_next
        if h % 2:
            sl = slice(h // 2 * LANES, (h // 2 + 1) * LANES)
            o_ref[:, sl] = jnp.where(lane < HEAD_DIM, prev_out, out).astype(o_ref.dtype)
        prev_out = out


def _attn_sample(qk, v, cache_kt, cache_vt, rel_rows, batch, seq):
    t, width = v.shape
    lc = cache_kt.shape[2]
    blk = (seq, width)
    bias_cols = -(-(lc + seq) // LANES) * LANES
    return pl.pallas_call(
        _attn_sample_kernel,
        out_shape=jax.ShapeDtypeStruct((t, width), BF16),
        grid=(batch,),
        in_specs=[
            pl.BlockSpec(blk, lambda b: (b, 0)),
            pl.BlockSpec(blk, lambda b: (b, 1)),
            pl.BlockSpec(blk, lambda b: (b, 0)),
            pl.BlockSpec((None, width, lc), lambda b: (b, 0, 0)),
            pl.BlockSpec((None, width, lc), lambda b: (b, 0, 0)),
            pl.BlockSpec(rel_rows.shape, lambda b: (0, 0)),
        ],
        out_specs=pl.BlockSpec(blk, lambda b: (b, 0)),
        scratch_shapes=[pltpu.VMEM((rel_rows.shape[0], seq, bias_cols), F32)],
        compiler_params=_params(("arbitrary",), 32),
    )(qk, qk, v, cache_kt, cache_vt, rel_rows)


def _mix_kernel(seq_rows, attn_ref, cb_ref, u_ref, halo_ref, gates_ref, x_ref, wa_ref, wc_ref, wo_ref,
                cw_ref, cbias_ref, g_ref, o_ref):
    tm, d = x_ref.shape
    u = u_ref[...]
    row = lax.broadcasted_iota(jnp.int32, u.shape, 0)
    if len(halo_ref.shape) == 3:
        nseq = halo_ref.shape[0]
        state = halo_ref[...]
        per_row = lambda r: jnp.broadcast_to(state[:, r:r + 1, :], (nseq, seq_rows, u.shape[1])).reshape(u.shape)
        prev2, prev1 = per_row(0), per_row(1)
        row = row % seq_rows
    else:
        keep = ((pl.program_id(0) * tm) % seq_rows != 0).astype(F32)
        nh = halo_ref.shape[0]
        prev2, prev1 = halo_ref[nh - 2:nh - 1, :] * keep, halo_ref[nh - 1:nh, :] * keep
    um1 = jnp.where(row == 0, prev1, pltpu.roll(u, 1, axis=0))
    um2 = jnp.where(row == 0, prev2, jnp.where(row == 1, prev1, pltpu.roll(u, 2, axis=0)))
    conv = cbias_ref[...] + cw_ref[0:1, :] * um2 + cw_ref[1:2, :] * um1 + cw_ref[2:3, :] * u
    ya = jnp.dot(attn_ref[...], wa_ref[...], preferred_element_type=F32)
    yb = jnp.dot((cb_ref[...].astype(F32) * conv).astype(BF16), wc_ref[...], preferred_element_type=F32)
    merged = gates_ref[:, :d].astype(F32) * ya + gates_ref[:, d:].astype(F32) * yb
    mix = jnp.dot(merged.astype(BF16), wo_ref[...], preferred_element_type=F32)
    o_ref[...] = x_ref[...] + _rms(mix, g_ref[...])


def _mix(attn, cb, u, state, gates, x, wa, wc, wo, conv_w, conv_b, g, tm, seq_rows):
    t, d = x.shape
    c = u.shape[1]
    halo_rows = 8
    if state is None:
        halo, halo_spec = u, pl.BlockSpec(
            (halo_rows, c), lambda i: (jnp.maximum(i * (tm // halo_rows) - 1, 0), 0))
    else:
        assert tm % seq_rows == 0
        halo, halo_spec = state, pl.BlockSpec((tm // seq_rows,) + state.shape[1:], lambda i: (i, 0, 0))
    row_spec = lambda w: pl.BlockSpec((tm, w), lambda i: (i, 0))
    resident = lambda a: pl.BlockSpec(a.shape, lambda i: (0,) * a.ndim, pipeline_mode=pl.Buffered(1))
    return pl.pallas_call(
        functools.partial(_mix_kernel, seq_rows),
        out_shape=jax.ShapeDtypeStruct((t, d), F32),
        grid=(t // tm,),
        in_specs=[row_spec(attn.shape[1]), row_spec(c), row_spec(c), halo_spec, row_spec(2 * d), row_spec(d),
                  resident(wa), resident(wc), resident(wo), resident(conv_w), resident(conv_b), resident(g)],
        out_specs=row_spec(d),
        compiler_params=_params(("arbitrary",), 58),
    )(attn, cb, u, halo, gates, x, wa, wc, wo, conv_w, conv_b, g)


def _mlp_kernel(h_ref, g1_ref, wu_ref, wd_ref, g2_ref, o_ref, n_ref, acc_ref):
    j = pl.program_id(1)

    @pl.when(j == 0)
    def _():
        n_ref[...] = _rms(h_ref[...], g1_ref[...]).astype(BF16)
        acc_ref[...] = jnp.zeros_like(acc_ref)

    a = jnp.dot(n_ref[...], wu_ref[...], preferred_element_type=F32)
    a = jnp.square(jnp.maximum(a, 0.0)).astype(BF16)
    acc_ref[...] += jnp.dot(a, wd_ref[...], preferred_element_type=F32)

    @pl.when(j == pl.num_programs(1) - 1)
    def _():
        o_ref[...] = h_ref[...] + _rms(acc_ref[...], g2_ref[...])


def _mlp(h, g1, w_up, w_down, g2, tm):
    t, d = h.shape
    tf = MLP_TF
    nj = w_up.shape[1] // tf
    return pl.pallas_call(
        _mlp_kernel,
        out_shape=jax.ShapeDtypeStruct((t, d), F32),
        grid=(t // tm, nj),
        in_specs=[
            pl.BlockSpec((tm, d), lambda i, j: (i, 0)),
            pl.BlockSpec((1, d), lambda i, j: (0, 0)),
            pl.BlockSpec((d, tf), lambda i, j: (0, j)),
            pl.BlockSpec((tf, d), lambda i, j: (j, 0)),
            pl.BlockSpec((1, d), lambda i, j: (0, 0)),
        ],
        out_specs=pl.BlockSpec((tm, d), lambda i, j: (i, 0)),
        scratch_shapes=[pltpu.VMEM((tm, d), BF16), pltpu.VMEM((tm, d), F32)],
        compiler_params=_params(("arbitrary", "arbitrary"), 56),
    )(h, g1, w_up, w_down, g2)


def _pe_kernel(row_splits, h_ref, p_ref, wg_ref, wp_ref, g_ref, o_ref):
    rows = h_ref.shape[0] // row_splits

    def products(r):
        sl = slice(r * rows, (r + 1) * rows)
        return (jnp.dot(h_ref[sl, :].astype(BF16), wg_ref[...], preferred_element_type=F32),
                jnp.dot(p_ref[sl, :].astype(BF16), wp_ref[...], preferred_element_type=F32))

    pending = products(0)
    for r in range(row_splits):
        gate_pre, pe = pending
        if r + 1 < row_splits:
            pending = products(r + 1)
        sl = slice(r * rows, (r + 1) * rows)
        o_ref[sl, :] = h_ref[sl, :] + _rms(jax.nn.sigmoid(gate_pre) * pe, g_ref[...])


def _pe(h, p, wg, wp, g, tm):
    t, d = h.shape
    resident = lambda a: pl.BlockSpec(a.shape, lambda i: (0,) * a.ndim, pipeline_mode=pl.Buffered(1))
    return pl.pallas_call(
        functools.partial(_pe_kernel, max(tm // 256, 1)),
        out_shape=jax.ShapeDtypeStruct((t, d), F32),
        grid=(t // tm,),
        in_specs=[pl.BlockSpec((tm, d), lambda i: (i, 0)), pl.BlockSpec((tm, p.shape[1]), lambda i: (i, 0)),
                  resident(wg), resident(wp), resident(g)],
        out_specs=pl.BlockSpec((tm, d), lambda i: (i, 0)),
        compiler_params=_params(("arbitrary",), 48),
    )(h, p, wg, wp, g)


def _layer(x, pe_in, seq, tiles_per_seq, attn_fn, state, w, tm_proj, tm_mix, tm_mlp, tm_pe):
    t = x.shape[0]
    qk, *kv, cb, u, gates = _in_proj(x, w["g_pre_mix"], w["w_in"], min(tm_proj, t), tiles_per_seq)
    attn = attn_fn(qk, kv[0])
    h = _mix(attn, cb, u, state, gates, x, w["w_attn_out"], w["w_conv_out"], w["w_o"], w["conv_w"],
             w["conv_b"], w["g_post_mix"], min(tm_mix, t), seq)
    h = _mlp(h, w["g_pre_mlp"], w["w_up"], w["w_down"], w["g_post_mlp"], min(tm_mlp, t))
    h = _pe(h, pe_in, w["w_pe_gate"], w["w_pe"], w["g_pe"], min(tm_pe, t))
    return h, kv, u.reshape(t // seq, seq, -1)[:, -2:]


def kernel(x_prompt, x_sample, cache_k, cache_v, state_conv, p_prompt, p_sample, w_in, rel_table, w_attn_out,
           conv_w, conv_b, w_conv_out, w_o, g_pre_mix, g_post_mix, g_pre_mlp, g_post_mlp, w_up, w_down, w_pe,
           w_pe_gate, g_pe):
    depth = w_in.shape[0]
    bp, sp, d = x_prompt.shape
    bs, ss, _ = x_sample.shape
    lc, heads = cache_k.shape[2], cache_k.shape[3]
    lp = min(LEFT_LEN, sp)
    assert sp % lp == 0 and sp % ATTN_TQ == 0 and lc == LEFT_LEN and ss <= CHUNK
    assert rel_table.shape[2] == 2 * REL_CLIP + 1

    hp = x_prompt.reshape(bp * sp, d)
    hs = x_sample.reshape(bs * ss, d)
    outs = [[] for _ in range(6)]
    for i in range(depth):
        w = {
            "w_in": w_in[i].astype(BF16), "w_attn_out": w_attn_out[i].astype(BF16),
            "w_conv_out": w_conv_out[i].astype(BF16), "w_o": w_o[i].astype(BF16),
            "w_up": w_up[i].astype(BF16), "w_down": w_down[i].astype(BF16),
            "w_pe": w_pe[i].astype(BF16), "w_pe_gate": w_pe_gate[i].astype(BF16),
            "conv_w": conv_w[i], "conv_b": conv_b[i][None],
            "g_pre_mix": g_pre_mix[i][None], "g_post_mix": g_post_mix[i][None],
            "g_pre_mlp": g_pre_mlp[i][None], "g_post_mlp": g_post_mlp[i][None], "g_pe": g_pe[i][None],
        }
        rel_rows = _rel_rows(rel_table[i])
        kct = cache_k[i].transpose(0, 2, 3, 1).reshape(bs, heads * HEAD_DIM, lc)
        vct = cache_v[i].transpose(0, 2, 3, 1).reshape(bs, heads * HEAD_DIM, lc)

        hp, (_, ktp, vtp), cpr = _layer(
            hp, p_prompt[i].reshape(bp * sp, -1), sp, sp // lp,
            lambda qk, v_t: _attn_prompt(qk, v_t, rel_rows, bp, sp), None, w,
            tm_proj=lp, tm_mix=512, tm_mlp=512, tm_pe=512)
        hs, (_, kvs), csm = _layer(
            hs, p_sample[i].reshape(bs * ss, -1), ss, None,
            lambda qk, v: _attn_sample(qk, v, kct, vct, rel_rows, bs, ss), state_conv[i], w,
            tm_proj=512, tm_mix=256, tm_mlp=512, tm_pe=512)
        kpr = ktp.reshape(bp, heads, HEAD_DIM, lp).transpose(0, 3, 1, 2)
        vpr = vtp.reshape(bp, heads, HEAD_DIM, lp).transpose(0, 3, 1, 2)
        ksm = kvs[:, :heads * HEAD_DIM].reshape(bs, ss, heads, HEAD_DIM)
        vsm = kvs[:, heads * HEAD_DIM:].reshape(bs, ss, heads, HEAD_DIM)
        for lst, val in zip(outs, (kpr, vpr, cpr, ksm, vsm, csm)):
            lst.append(val)
    k_prompt, v_prompt, conv_prompt, k_sample, v_sample, conv_sample = [jnp.stack(o) for o in outs]
    return (hp.reshape(bp, sp, d), hs.reshape(bs, ss, d), k_prompt, v_prompt, conv_prompt,
            k_sample, v_sample, conv_sample)
```

```python
import functools

import jax
import jax.numpy as jnp
from jax import lax
from jax.experimental import pallas as pl
from jax.experimental.pallas import tpu as pltpu

F32 = jnp.float32
BF16 = jnp.bfloat16

CHUNK = 64
LEFT_CHUNKS = 8
LEFT_LEN = LEFT_CHUNKS * CHUNK
BAND = (LEFT_CHUNKS + 1) * CHUNK
HEAD_DIM = 64
REL_CLIP = 128
EPS = 1e-6
NEG = -1e30

LANES = 128
ATTN_TQ = 256
ATTN_NK = LEFT_LEN + ATTN_TQ
ATTN_SLOTS = ATTN_NK // ATTN_TQ
REL_ROW = ATTN_NK + ATTN_TQ
IN_PROJ_TN = 2048
MLP_TF = 1024
MIB = 1024 * 1024


def _rms(x, g):
    return x * lax.rsqrt(jnp.mean(x * x, axis=-1, keepdims=True) + EPS) * g


def _params(sem, vmem_mib, flags=None):
    return pltpu.CompilerParams(dimension_semantics=sem, vmem_limit_bytes=vmem_mib * MIB, flags=flags)


STEP_QK = 2


def _in_proj_kernel(tiles_per_seq, x_ref, g_ref, w_ref, qk_ref, *refs):
    if tiles_per_seq is None:
        v_ref, kv_ref, cb_ref, u_ref, gates_ref, n_ref = refs
    else:
        v_ref, kt_ref, vt_ref, cb_ref, u_ref, gates_ref, n_ref = refs
        is_tail = pl.program_id(0) % tiles_per_seq == tiles_per_seq - 1
    j = pl.program_id(1)
    half = w_ref.shape[1] // 2

    def proj():
        z = jnp.dot(n_ref[...], w_ref[...], preferred_element_type=F32)
        return z[:, :half], z[:, half:]

    def step(jj, tail, body):
        cond = j == jj
        if tail is not None and tiles_per_seq is not None:
            cond = cond & (is_tail if tail else jnp.logical_not(is_tail))
        pl.when(cond)(body)

    def q_k(tail):
        def body():
            q, k = proj()
            qk_ref[:, :half] = (q * (HEAD_DIM ** -0.5)).astype(BF16)
            qk_ref[:, half:] = k.astype(BF16)
            if tiles_per_seq is None:
                kv_ref[:, :half] = k
            elif tail:
                kt_ref[...] = k.T
        return body

    def v_bg(tail):
        def body():
            v, bg = proj()
            cb_ref[...] = bg.astype(BF16)
            if tiles_per_seq is None:
                v_ref[...] = v.astype(BF16)
                kv_ref[:, half:] = v
            else:
                vt = v.T
                v_ref[...] = vt.astype(BF16)
                if tail:
                    vt_ref[...] = vt
        return body

    def conv_in():
        cg, xin = proj()
        u_ref[...] = cg * xin

    def gate(first):
        def body():
            if first:
                n_ref[...] = _rms(x_ref[...], g_ref[...]).astype(BF16)
            za, zb = proj()
            gates_ref[:, :half] = jax.nn.sigmoid(za).astype(BF16)
            gates_ref[:, half:] = jax.nn.sigmoid(zb).astype(BF16)
        return body

    step(0, None, gate(True))
    pl.when((j > 0) & (j < STEP_QK))(gate(False))
    for tail in ((None,) if tiles_per_seq is None else (False, True)):
        step(STEP_QK, tail, q_k(tail))
        step(STEP_QK + 1, tail, v_bg(tail))
    step(STEP_QK + 2, None, conv_in)


def _in_proj(x, g, w_in, tm, tiles_per_seq):
    t, d = x.shape
    tn = IN_PROJ_TN
    nj = w_in.shape[1] // tn
    half = tn // 2
    assert nj == 5 and d == tn and t % tm == 0

    rows = lambda w: pl.BlockSpec((tm, w), lambda i, j: (i, 0))
    if tiles_per_seq is None:
        kv_shapes = (jax.ShapeDtypeStruct((t, half), BF16),
                     jax.ShapeDtypeStruct((t, tn), F32))
        kv_specs = (rows(half), rows(tn))
    else:
        nseq = t // (tm * tiles_per_seq)
        tail_spec = pl.BlockSpec((None, half, tm), lambda i, j: (i // tiles_per_seq, 0, 0))
        kv_shapes = (jax.ShapeDtypeStruct((half, t), BF16),
                     jax.ShapeDtypeStruct((nseq, half, tm), F32),
                     jax.ShapeDtypeStruct((nseq, half, tm), F32))
        kv_specs = (pl.BlockSpec((half, tm), lambda i, j: (0, i)), tail_spec, tail_spec)
    out_shape = (
        jax.ShapeDtypeStruct((t, tn), BF16),
        *kv_shapes,
        jax.ShapeDtypeStruct((t, half), BF16),
        jax.ShapeDtypeStruct((t, half), F32),
        jax.ShapeDtypeStruct((t, 2 * tn), BF16),
    )
    return pl.pallas_call(
        functools.partial(_in_proj_kernel, tiles_per_seq),
        out_shape=out_shape,
        grid=(t // tm, nj),
        in_specs=[
            pl.BlockSpec((tm, d), lambda i, j: (i, 0)),
            pl.BlockSpec((1, d), lambda i, j: (0, 0)),
            pl.BlockSpec((d, tn), lambda i, j: (0, (j + nj - STEP_QK) % nj)),
        ],
        out_specs=(
            rows(tn),
            *kv_specs,
            rows(half),
            rows(half),
            pl.BlockSpec((tm, tn), lambda i, j: (i, jnp.minimum(j, STEP_QK - 1))),
        ),
        scratch_shapes=[pltpu.VMEM((tm, d), BF16)],
        compiler_params=_params(("arbitrary", "arbitrary"), 56),
    )(x, g, w_in)


def _rel_rows(rel_table):
    h, n_rel = rel_table.shape
    lo = LEFT_LEN - REL_CLIP
    hi = ATTN_NK - lo - n_rel
    first = jnp.broadcast_to(rel_table[:, :1], (h, lo))
    last = jnp.broadcast_to(rel_table[:, -1:], (h, hi))
    negative = jnp.broadcast_to(rel_table[:, :1], (h, REL_ROW - ATTN_NK))
    return jnp.concatenate([first, rel_table, last, negative], axis=1).astype(F32)


def _toeplitz(row, nq):
    return pltpu.roll(jnp.broadcast_to(row, (nq, row.shape[1])), 0, 1, stride=1, stride_axis=0)


def _attn_prompt_kernel(q_ref, *refs):
    k_refs = refs[:ATTN_SLOTS]
    vt_refs = refs[ATTN_SLOTS:2 * ATTN_SLOTS]
    rel_ref, o_ref, bias_ref = refs[2 * ATTN_SLOTS:]
    t = pl.program_id(1)

    @pl.when((pl.program_id(0) == 0) & (t == 0))
    def _():
        qi = lax.broadcasted_iota(jnp.int32, (ATTN_TQ, ATTN_NK), 0)
        kj = lax.broadcasted_iota(jnp.int32, (ATTN_TQ, ATTN_NK), 1)
        first = (qi // CHUNK) * CHUNK
        visible = (kj >= first) & (kj < first + BAND)
        for h in range(bias_ref.shape[0]):
            bias_ref[h] = jnp.where(visible, _toeplitz(rel_ref[h:h + 1, :], ATTN_TQ)[:, :ATTN_NK], NEG).T

    for first_slot in range(ATTN_SLOTS):
        tiles_before = ATTN_SLOTS - 1 - first_slot
        cond = (t >= tiles_before) if first_slot == 0 else (t == tiles_before)

        @pl.when(cond)
        def _(first_slot=first_slot):
            _band_heads(q_ref, k_refs[first_slot:], vt_refs[first_slot:], bias_ref, first_slot * ATTN_TQ, o_ref)


def _band_heads(q_ref, k_refs, vt_refs, bias_ref, key0, o_ref):
    nk = len(k_refs) * ATTN_TQ
    heads = 2 * (q_ref.shape[1] // LANES)
    lane = lax.broadcasted_iota(jnp.int32, (ATTN_TQ, LANES), 1)
    feat = lax.broadcasted_iota(jnp.int32, (LANES, nk), 0)

    def scores(h):
        sl = slice(h // 2 * LANES, (h // 2 + 1) * LANES)
        own_lane = (lane >= HEAD_DIM) if h % 2 else (lane < HEAD_DIM)
        q_pair = q_ref[:, sl]
        qm = jnp.where(own_lane, q_pair, jnp.zeros_like(q_pair))
        keys = jnp.concatenate([r[:, sl] for r in k_refs], axis=0)
        st = lax.dot_general(keys, qm, (((1,), (1,)), ((), ())), preferred_element_type=F32)
        return st + bias_ref[h, key0:key0 + nk, :]

    def weighted_values(h, st):
        sl = slice(h // 2 * LANES, (h // 2 + 1) * LANES)
        own_feat = (feat >= HEAD_DIM) if h % 2 else (feat < HEAD_DIM)
        pt = jnp.exp(st - jnp.max(st, axis=0, keepdims=True)).astype(BF16)
        vals_t = jnp.concatenate([r[sl, :] for r in vt_refs], axis=1)
        vm = jnp.where(own_feat, vals_t, jnp.ones_like(vals_t))
        acc = jnp.dot(vm, pt, preferred_element_type=F32)
        lo, hi = acc[:HEAD_DIM], acc[HEAD_DIM:]
        return hi / lo if h % 2 else lo / hi

    halves = []
    ahead = 3
    pending = [scores(h) for h in range(ahead)]
    for h in range(heads):
        if h + ahead < heads:
            pending.append(scores(h + ahead))
        halves.append(weighted_values(h, pending.pop(0)))
        if h % 2:
            sl = slice(h // 2 * LANES, (h // 2 + 1) * LANES)
            o_ref[:, sl] = jnp.concatenate(halves, axis=0).T.astype(o_ref.dtype)
            halves = []


def _attn_prompt(qk, v_t, rel_rows, batch, seq):
    t, width = v_t.shape[1], v_t.shape[0]
    nt = seq // ATTN_TQ
    blk = (ATTN_TQ, width)
    slot_row = lambda b, i, s: b * nt + jnp.maximum(i - (ATTN_SLOTS - 1) + s, 0)
    return pl.pallas_call(
        _attn_prompt_kernel,
        out_shape=jax.ShapeDtypeStruct((t, width), BF16),
        grid=(batch, nt),
        in_specs=[pl.BlockSpec(blk, lambda b, i: (b * nt + i, 0))]
        + [pl.BlockSpec(blk, lambda b, i, s=s: (slot_row(b, i, s), 1)) for s in range(ATTN_SLOTS)]
        + [pl.BlockSpec((width, ATTN_TQ), lambda b, i, s=s: (0, slot_row(b, i, s))) for s in range(ATTN_SLOTS)]
        + [pl.BlockSpec(rel_rows.shape, lambda b, i: (0, 0))],
        out_specs=pl.BlockSpec(blk, lambda b, i: (b * nt + i, 0)),
        scratch_shapes=[pltpu.VMEM((rel_rows.shape[0], ATTN_NK, ATTN_TQ), F32)],
        compiler_params=_params(("arbitrary", "arbitrary"), 40, ),
    )(qk, *([qk] * ATTN_SLOTS), *([v_t] * ATTN_SLOTS), rel_rows)


def _attn_sample_kernel(q_ref, kn_ref, vn_ref, kct_ref, vct_ref, rel_ref, o_ref, bias_ref):
    nq = q_ref.shape[0]
    lc = kct_ref.shape[1]

    @pl.when(pl.program_id(0) == 0)
    def _():
        for h in range(bias_ref.shape[0]):
            bias_ref[h] = _toeplitz(rel_ref[h:h + 1, :], nq)[:, :bias_ref.shape[2]]

    lane = lax.broadcasted_iota(jnp.int32, (nq, LANES), 1)
    nt_dims = (((1,), (1,)), ((), ()))

    def scores(h):
        sl = slice(h // 2 * LANES, (h // 2 + 1) * LANES)
        own = (lane >= HEAD_DIM) if h % 2 else (lane < HEAD_DIM)
        qm = jnp.where(own, q_ref[:, sl], jnp.zeros((nq, LANES), BF16))
        s_cache = jnp.dot(qm, kct_ref[sl, :].astype(BF16), preferred_element_type=F32)
        s_new = lax.dot_general(qm, kn_ref[:, sl], nt_dims, preferred_element_type=F32)
        return s_cache + bias_ref[h, :, :lc], s_new + bias_ref[h, :, lc:lc + nq]

    def weighted_values(h, s_cache, s_new):
        sl = slice(h // 2 * LANES, (h // 2 + 1) * LANES)
        m = jnp.maximum(jnp.max(s_cache, axis=-1, keepdims=True), jnp.max(s_new, axis=-1, keepdims=True))
        p_cache, p_new = jnp.exp(s_cache - m), jnp.exp(s_new - m)
        denom = jnp.sum(p_cache, axis=-1, keepdims=True) + jnp.sum(p_new, axis=-1, keepdims=True)
        acc = lax.dot_general(p_cache.astype(BF16), vct_ref[sl, :].astype(BF16), nt_dims,
                              preferred_element_type=F32)
        acc = acc + jnp.dot(p_new.astype(BF16), vn_ref[:, sl], preferred_element_type=F32)
        return acc / denom

    heads = bias_ref.shape[0]
    ahead = 3
    pending = [scores(h) for h in range(ahead)]
    for h in range(heads):
        if h + ahead < heads:
            pending.append(scores(h + ahead))
        out = weighted_values(h, *pending.pop(0))
        if h % 2:
            sl = slice(h // 2 * LANES, (h // 2 + 1) * LANES)
            o_ref[:, sl] = jnp.where(lane < HEAD_DIM, prev_out, out).astype(o_ref.dtype)
        prev_out = out


def _attn_sample(qk, v, cache_kt, cache_vt, rel_rows, batch, seq):
    t, width = v.shape
    lc = cache_kt.shape[2]
    blk = (seq, width)
    bias_cols = -(-(lc + seq) // LANES) * LANES
    return pl.pallas_call(
        _attn_sample_kernel,
        out_shape=jax.ShapeDtypeStruct((t, width), BF16),
        grid=(batch,),
        in_specs=[
            pl.BlockSpec(blk, lambda b: (b, 0)),
            pl.BlockSpec(blk, lambda b: (b, 1)),
            pl.BlockSpec(blk, lambda b: (b, 0)),
            pl.BlockSpec((None, width, lc), lambda b: (b, 0, 0)),
            pl.BlockSpec((None, width, lc), lambda b: (b, 0, 0)),
            pl.BlockSpec(rel_rows.shape, lambda b: (0, 0)),
        ],
        out_specs=pl.BlockSpec(blk, lambda b: (b, 0)),
        scratch_shapes=[pltpu.VMEM((rel_rows.shape[0], seq, bias_cols), F32)],
        compiler_params=_params(("arbitrary",), 32),
    )(qk, qk, v, cache_kt, cache_vt, rel_rows)


def _mix_kernel(seq_rows, attn_ref, cb_ref, u_ref, halo_ref, gates_ref, x_ref, wa_ref, wc_ref, wo_ref,
                cw_ref, cbias_ref, g_ref, o_ref):
    tm, d = x_ref.shape
    u = u_ref[...]
    row = lax.broadcasted_iota(jnp.int32, u.shape, 0)
    if len(halo_ref.shape) == 3:
        nseq = halo_ref.shape[0]
        state = halo_ref[...]
        per_row = lambda r: jnp.broadcast_to(state[:, r:r + 1, :], (nseq, seq_rows, u.shape[1])).reshape(u.shape)
        prev2, prev1 = per_row(0), per_row(1)
        row = row % seq_rows
    else:
        keep = ((pl.program_id(0) * tm) % seq_rows != 0).astype(F32)
        nh = halo_ref.shape[0]
        prev2, prev1 = halo_ref[nh - 2:nh - 1, :] * keep, halo_ref[nh - 1:nh, :] * keep
    um1 = jnp.where(row == 0, prev1, pltpu.roll(u, 1, axis=0))
    um2 = jnp.where(row == 0, prev2, jnp.where(row == 1, prev1, pltpu.roll(u, 2, axis=0)))
    conv = cbias_ref[...] + cw_ref[0:1, :] * um2 + cw_ref[1:2, :] * um1 + cw_ref[2:3, :] * u
    ya = jnp.dot(attn_ref[...], wa_ref[...], preferred_element_type=F32)
    yb = jnp.dot((cb_ref[...].astype(F32) * conv).astype(BF16), wc_ref[...], preferred_element_type=F32)
    merged = gates_ref[:, :d].astype(F32) * ya + gates_ref[:, d:].astype(F32) * yb
    mix = jnp.dot(merged.astype(BF16), wo_ref[...], preferred_element_type=F32)
    o_ref[...] = x_ref[...] + _rms(mix, g_ref[...])


def _mix(attn, cb, u, state, gates, x, wa, wc, wo, conv_w, conv_b, g, tm, seq_rows):
    t, d = x.shape
    c = u.shape[1]
    halo_rows = 8
    if state is None:
        halo, halo_spec = u, pl.BlockSpec(
            (halo_rows, c), lambda i: (jnp.maximum(i * (tm // halo_rows) - 1, 0), 0))
    else:
        assert tm % seq_rows == 0
        halo, halo_spec = state, pl.BlockSpec((tm // seq_rows,) + state.shape[1:], lambda i: (i, 0, 0))
    row_spec = lambda w: pl.BlockSpec((tm, w), lambda i: (i, 0))
    resident = lambda a: pl.BlockSpec(a.shape, lambda i: (0,) * a.ndim, pipeline_mode=pl.Buffered(1))
    return pl.pallas_call(
        functools.partial(_mix_kernel, seq_rows),
        out_shape=jax.ShapeDtypeStruct((t, d), F32),
        grid=(t // tm,),
        in_specs=[row_spec(attn.shape[1]), row_spec(c), row_spec(c), halo_spec, row_spec(2 * d), row_spec(d),
                  resident(wa), resident(wc), resident(wo), resident(conv_w), resident(conv_b), resident(g)],
        out_specs=row_spec(d),
        compiler_params=_params(("arbitrary",), 58),
    )(attn, cb, u, halo, gates, x, wa, wc, wo, conv_w, conv_b, g)


def _mlp_kernel(nj, h_ref, g1_ref, wu_ref, wd_ref, g2_ref, o_ref, n_ref, acc_ref):
    j = pl.program_id(1)

    def step(first, last):
        def body():
            if first:
                n_ref[...] = _rms(h_ref[...], g1_ref[...]).astype(BF16)
            a = jnp.dot(n_ref[...], wu_ref[...], preferred_element_type=F32)
            a = jnp.square(jnp.maximum(a, 0.0)).astype(BF16)
            part = jnp.dot(a, wd_ref[...], preferred_element_type=F32)
            if last:
                f = part if first else acc_ref[...] + part
                o_ref[...] = h_ref[...] + _rms(f, g2_ref[...])
            elif first:
                acc_ref[...] = part
            else:
                acc_ref[...] += part
        return body

    if nj == 1:
        step(True, True)()
    else:
        pl.when(j == 0)(step(True, False))
        pl.when(j == nj - 1)(step(False, True))
        if nj > 2:
            pl.when((j > 0) & (j < nj - 1))(step(False, False))


def _mlp(h, g1, w_up, w_down, g2, tm):
    t, d = h.shape
    tf = MLP_TF
    nj = w_up.shape[1] // tf
    return pl.pallas_call(
        functools.partial(_mlp_kernel, nj),
        out_shape=jax.ShapeDtypeStruct((t, d), F32),
        grid=(t // tm, nj),
        in_specs=[
            pl.BlockSpec((tm, d), lambda i, j: (i, 0)),
            pl.BlockSpec((1, d), lambda i, j: (0, 0)),
            pl.BlockSpec((d, tf), lambda i, j: (0, j)),
            pl.BlockSpec((tf, d), lambda i, j: (j, 0)),
            pl.BlockSpec((1, d), lambda i, j: (0, 0)),
        ],
        out_specs=pl.BlockSpec((tm, d), lambda i, j: (i, 0)),
        scratch_shapes=[pltpu.VMEM((tm, d), BF16), pltpu.VMEM((tm, d), F32)],
        compiler_params=_params(("arbitrary", "arbitrary"), 56),
    )(h, g1, w_up, w_down, g2)


def _pe_kernel(row_splits, h_ref, p_ref, wg_ref, wp_ref, g_ref, o_ref):
    rows = h_ref.shape[0] // row_splits

    def products(r):
        sl = slice(r * rows, (r + 1) * rows)
        return (jnp.dot(h_ref[sl, :].astype(BF16), wg_ref[...], preferred_element_type=F32),
                jnp.dot(p_ref[sl, :].astype(BF16), wp_ref[...], preferred_element_type=F32))

    pending = products(0)
    for r in range(row_splits):
        gate_pre, pe = pending
        if r + 1 < row_splits:
            pending = products(r + 1)
        sl = slice(r * rows, (r + 1) * rows)
        o_ref[sl, :] = h_ref[sl, :] + _rms(jax.nn.sigmoid(gate_pre) * pe, g_ref[...])


def _pe(h, p, wg, wp, g, tm):
    t, d = h.shape
    resident = lambda a: pl.BlockSpec(a.shape, lambda i: (0,) * a.ndim, pipeline_mode=pl.Buffered(1))
    return pl.pallas_call(
        functools.partial(_pe_kernel, max(tm // 256, 1)),
        out_shape=jax.ShapeDtypeStruct((t, d), F32),
        grid=(t // tm,),
        in_specs=[pl.BlockSpec((tm, d), lambda i: (i, 0)), pl.BlockSpec((tm, p.shape[1]), lambda i: (i, 0)),
                  resident(wg), resident(wp), resident(g)],
        out_specs=pl.BlockSpec((tm, d), lambda i: (i, 0)),
        compiler_params=_params(("arbitrary",), 48),
    )(h, p, wg, wp, g)


def _layer(x, pe_in, seq, tiles_per_seq, attn_fn, state, w, tm_proj, tm_mix, tm_mlp, tm_pe):
    t = x.shape[0]
    qk, *kv, cb, u, gates = _in_proj(x, w["g_pre_mix"], w["w_in"], min(tm_proj, t), tiles_per_seq)
    attn = attn_fn(qk, kv[0])
    h = _mix(attn, cb, u, state, gates, x, w["w_attn_out"], w["w_conv_out"], w["w_o"], w["conv_w"],
             w["conv_b"], w["g_post_mix"], min(tm_mix, t), seq)
    h = _mlp(h, w["g_pre_mlp"], w["w_up"], w["w_down"], w["g_post_mlp"], min(tm_mlp, t))
    h = _pe(h, pe_in, w["w_pe_gate"], w["w_pe"], w["g_pe"], min(tm_pe, t))
    return h, kv, u.reshape(t // seq, seq, -1)[:, -2:]


def kernel(x_prompt, x_sample, cache_k, cache_v, state_conv, p_prompt, p_sample, w_in, rel_table, w_attn_out,
           conv_w, conv_b, w_conv_out, w_o, g_pre_mix, g_post_mix, g_pre_mlp, g_post_mlp, w_up, w_down, w_pe,
           w_pe_gate, g_pe):
    depth = w_in.shape[0]
    bp, sp, d = x_prompt.shape
    bs, ss, _ = x_sample.shape
    lc, heads = cache_k.shape[2], cache_k.shape[3]
    lp = min(LEFT_LEN, sp)
    assert sp % lp == 0 and sp % ATTN_TQ == 0 and lc == LEFT_LEN and ss <= CHUNK
    assert rel_table.shape[2] == 2 * REL_CLIP + 1

    hp = x_prompt.reshape(bp * sp, d)
    hs = x_sample.reshape(bs * ss, d)
    outs = [[] for _ in range(6)]
    for i in range(depth):
        w = {
            "w_in": w_in[i].astype(BF16), "w_attn_out": w_attn_out[i].astype(BF16),
            "w_conv_out": w_conv_out[i].astype(BF16), "w_o": w_o[i].astype(BF16),
            "w_up": w_up[i].astype(BF16), "w_down": w_down[i].astype(BF16),
            "w_pe": w_pe[i].astype(BF16), "w_pe_gate": w_pe_gate[i].astype(BF16),
            "conv_w": conv_w[i], "conv_b": conv_b[i][None],
            "g_pre_mix": g_pre_mix[i][None], "g_post_mix": g_post_mix[i][None],
            "g_pre_mlp": g_pre_mlp[i][None], "g_post_mlp": g_post_mlp[i][None], "g_pe": g_pe[i][None],
        }
        rel_rows = _rel_rows(rel_table[i])
        kct = cache_k[i].transpose(0, 2, 3, 1).reshape(bs, heads * HEAD_DIM, lc)
        vct = cache_v[i].transpose(0, 2, 3, 1).reshape(bs, heads * HEAD_DIM, lc)

        hp, (_, ktp, vtp), cpr = _layer(
            hp, p_prompt[i].reshape(bp * sp, -1), sp, sp // lp,
            lambda qk, v_t: _attn_prompt(qk, v_t, rel_rows, bp, sp), None, w,
            tm_proj=lp, tm_mix=512, tm_mlp=512, tm_pe=512)
        hs, (_, kvs), csm = _layer(
            hs, p_sample[i].reshape(bs * ss, -1), ss, None,
            lambda qk, v: _attn_sample(qk, v, kct, vct, rel_rows, bs, ss), state_conv[i], w,
            tm_proj=512, tm_mix=256, tm_mlp=512, tm_pe=512)
        kpr = ktp.reshape(bp, heads, HEAD_DIM, lp).transpose(0, 3, 1, 2)
        vpr = vtp.reshape(bp, heads, HEAD_DIM, lp).transpose(0, 3, 1, 2)
        ksm = kvs[:, :heads * HEAD_DIM].reshape(bs, ss, heads, HEAD_DIM)
        vsm = kvs[:, heads * HEAD_DIM:].reshape(bs, ss, heads, HEAD_DIM)
        for lst, val in zip(outs, (kpr, vpr, cpr, ksm, vsm, csm)):
            lst.append(val)
    k_prompt, v_prompt, conv_prompt, k_sample, v_sample, conv_sample = [jnp.stack(o) for o in outs]
    return (hp.reshape(bp, sp, d), hs.reshape(bs, ss, d), k_prompt, v_prompt, conv_prompt,
            k_sample, v_sample, conv_sample)
```

```python
import functools

import jax
import jax.numpy as jnp
from jax import lax
from jax.experimental import pallas as pl
from jax.experimental.pallas import tpu as pltpu

F32 = jnp.float32
BF16 = jnp.bfloat16

CHUNK = 64
LEFT_CHUNKS = 8
LEFT_LEN = LEFT_CHUNKS * CHUNK
BAND = (LEFT_CHUNKS + 1) * CHUNK
HEAD_DIM = 64
REL_CLIP = 128
EPS = 1e-6
NEG = -1e30
LOG2E = 1.4426950408889634
SCORE_SCALE = HEAD_DIM ** -0.5 * LOG2E

LANES = 128
ATTN_TQ = 256
ATTN_NK = LEFT_LEN + ATTN_TQ
ATTN_SLOTS = ATTN_NK // ATTN_TQ
REL_ROW = ATTN_NK + ATTN_TQ
IN_PROJ_TN = 2048
MLP_TF = 1024
MIB = 1024 * 1024


def _rms(x, g):
    return x * lax.rsqrt(jnp.mean(x * x, axis=-1, keepdims=True) + EPS) * g


def _params(sem, vmem_mib, flags=None):
    return pltpu.CompilerParams(dimension_semantics=sem, vmem_limit_bytes=vmem_mib * MIB, flags=flags)


STEP_QK = 2


def _in_proj_kernel(tiles_per_seq, x_ref, g_ref, w_ref, qk_ref, *refs):
    if tiles_per_seq is None:
        v_ref, kv_ref, cb_ref, u_ref, gates_ref, n_ref = refs
    else:
        v_ref, kt_ref, vt_ref, cb_ref, u_ref, gates_ref, n_ref = refs
        is_tail = pl.program_id(0) % tiles_per_seq == tiles_per_seq - 1
    j = pl.program_id(1)
    half = w_ref.shape[1] // 2

    def proj():
        z = jnp.dot(n_ref[...], w_ref[...], preferred_element_type=F32)
        return z[:, :half], z[:, half:]

    def step(jj, tail, body):
        cond = j == jj
        if tail is not None and tiles_per_seq is not None:
            cond = cond & (is_tail if tail else jnp.logical_not(is_tail))
        pl.when(cond)(body)

    def q_k(tail):
        def body():
            q, k = proj()
            qk_ref[:, :half] = (q * SCORE_SCALE).astype(BF16)
            qk_ref[:, half:] = k.astype(BF16)
            if tiles_per_seq is None:
                kv_ref[:, :half] = k
            elif tail:
                kt_ref[...] = k.T
        return body

    def v_bg(tail):
        def body():
            v, bg = proj()
            cb_ref[...] = bg.astype(BF16)
            if tiles_per_seq is None:
                v_ref[...] = v.astype(BF16)
                kv_ref[:, half:] = v
            else:
                vt = v.T
                v_ref[...] = vt.astype(BF16)
                if tail:
                    vt_ref[...] = vt
        return body

    def conv_in():
        cg, xin = proj()
        u_ref[...] = cg * xin

    def gate(first):
        def body():
            if first:
                n_ref[...] = _rms(x_ref[...], g_ref[...]).astype(BF16)
            za, zb = proj()
            gates_ref[:, :half] = jax.nn.sigmoid(za).astype(BF16)
            gates_ref[:, half:] = jax.nn.sigmoid(zb).astype(BF16)
        return body

    step(0, None, gate(True))
    pl.when((j > 0) & (j < STEP_QK))(gate(False))
    for tail in ((None,) if tiles_per_seq is None else (False, True)):
        step(STEP_QK, tail, q_k(tail))
        step(STEP_QK + 1, tail, v_bg(tail))
    step(STEP_QK + 2, None, conv_in)


def _in_proj(x, g, w_in, tm, tiles_per_seq):
    t, d = x.shape
    tn = IN_PROJ_TN
    nj = w_in.shape[1] // tn
    half = tn // 2
    assert nj == 5 and d == tn and t % tm == 0

    rows = lambda w: pl.BlockSpec((tm, w), lambda i, j: (i, 0))
    if tiles_per_seq is None:
        kv_shapes = (jax.ShapeDtypeStruct((t, half), BF16),
                     jax.ShapeDtypeStruct((t, tn), F32))
        kv_specs = (rows(half), rows(tn))
    else:
        nseq = t // (tm * tiles_per_seq)
        tail_spec = pl.BlockSpec((None, half, tm), lambda i, j: (i // tiles_per_seq, 0, 0))
        kv_shapes = (jax.ShapeDtypeStruct((half, t), BF16),
                     jax.ShapeDtypeStruct((nseq, half, tm), F32),
                     jax.ShapeDtypeStruct((nseq, half, tm), F32))
        kv_specs = (pl.BlockSpec((half, tm), lambda i, j: (0, i)), tail_spec, tail_spec)
    out_shape = (
        jax.ShapeDtypeStruct((t, tn), BF16),
        *kv_shapes,
        jax.ShapeDtypeStruct((t, half), BF16),
        jax.ShapeDtypeStruct((t, half), F32),
        jax.ShapeDtypeStruct((t, 2 * tn), BF16),
    )
    return pl.pallas_call(
        functools.partial(_in_proj_kernel, tiles_per_seq),
        out_shape=out_shape,
        grid=(t // tm, nj),
        in_specs=[
            pl.BlockSpec((tm, d), lambda i, j: (i, 0)),
            pl.BlockSpec((1, d), lambda i, j: (0, 0)),
            pl.BlockSpec((d, tn), lambda i, j: (0, (j + nj - STEP_QK) % nj)),
        ],
        out_specs=(
            rows(tn),
            *kv_specs,
            rows(half),
            rows(half),
            pl.BlockSpec((tm, tn), lambda i, j: (i, jnp.minimum(j, STEP_QK - 1))),
        ),
        scratch_shapes=[pltpu.VMEM((tm, d), BF16)],
        compiler_params=_params(("arbitrary", "arbitrary"), 56),
    )(x, g, w_in)


def _rel_rows(rel_table):
    h, n_rel = rel_table.shape
    lo = LEFT_LEN - REL_CLIP
    hi = ATTN_NK - lo - n_rel
    first = jnp.broadcast_to(rel_table[:, :1], (h, lo))
    last = jnp.broadcast_to(rel_table[:, -1:], (h, hi))
    negative = jnp.broadcast_to(rel_table[:, :1], (h, REL_ROW - ATTN_NK))
    return jnp.concatenate([first, rel_table, last, negative], axis=1).astype(F32) * LOG2E


def _toeplitz(row, nq):
    return pltpu.roll(jnp.broadcast_to(row, (nq, row.shape[1])), 0, 1, stride=1, stride_axis=0)


def _attn_prompt_kernel(q_ref, kp_ref, kc_ref, vtp_ref, vtc_ref, rel_ref, o_ref, bias_ref):
    t = pl.program_id(1)

    @pl.when((pl.program_id(0) == 0) & (t == 0))
    def _():
        qi = lax.broadcasted_iota(jnp.int32, (ATTN_TQ, ATTN_NK), 0)
        kj = lax.broadcasted_iota(jnp.int32, (ATTN_TQ, ATTN_NK), 1)
        first = (qi // CHUNK) * CHUNK
        visible = (kj >= first) & (kj < first + BAND)
        for h in range(bias_ref.shape[0]):
            bias_ref[h] = jnp.where(visible, _toeplitz(rel_ref[h:h + 1, :], ATTN_TQ)[:, :ATTN_NK], NEG).T

    tiles = q_ref.shape[0] // ATTN_TQ
    prev_slots = kp_ref.shape[0] // ATTN_TQ

    def key_slot(s):
        k_ref, vt_ref = (kp_ref, vtp_ref) if s < prev_slots else (kc_ref, vtc_ref)
        r0 = (s % prev_slots) * ATTN_TQ
        return k_ref.at[pl.ds(r0, ATTN_TQ)], vt_ref.at[:, pl.ds(r0, ATTN_TQ)]

    for s in range(tiles):
        g = t * tiles + s
        rows = pl.ds(s * ATTN_TQ, ATTN_TQ)
        for first_slot in range(ATTN_SLOTS):
            tiles_before = ATTN_SLOTS - 1 - first_slot
            if first_slot and (tiles_before < s or (tiles_before - s) % tiles):
                continue
            cond = (g >= tiles_before) if first_slot == 0 else (g == tiles_before)

            @pl.when(cond)
            def _(s=s, rows=rows, first_slot=first_slot):
                slots = [key_slot(s + prev_slots - (ATTN_SLOTS - 1) + n) for n in range(first_slot, ATTN_SLOTS)]
                _band_heads(q_ref.at[rows], [k for k, _ in slots], [v for _, v in slots], bias_ref,
                            first_slot * ATTN_TQ, o_ref.at[rows])


def _band_heads(q_ref, k_refs, vt_refs, bias_ref, key0, o_ref):
    nk = len(k_refs) * ATTN_TQ
    heads = 2 * (q_ref.shape[1] // LANES)
    lane = lax.broadcasted_iota(jnp.int32, (ATTN_TQ, LANES), 1)
    feat = lax.broadcasted_iota(jnp.int32, (LANES, nk), 0)

    def scores(h):
        sl = slice(h // 2 * LANES, (h // 2 + 1) * LANES)
        own_lane = (lane >= HEAD_DIM) if h % 2 else (lane < HEAD_DIM)
        q_pair = q_ref[:, sl]
        qm = jnp.where(own_lane, q_pair, jnp.zeros_like(q_pair))
        keys = jnp.concatenate([r[:, sl] for r in k_refs], axis=0)
        st = lax.dot_general(keys, qm, (((1,), (1,)), ((), ())), preferred_element_type=F32)
        return st + bias_ref[h, key0:key0 + nk, :]

    def weighted_values(h, st):
        sl = slice(h // 2 * LANES, (h // 2 + 1) * LANES)
        own_feat = (feat >= HEAD_DIM) if h % 2 else (feat < HEAD_DIM)
        pt = jnp.exp2(st - jnp.max(st, axis=0, keepdims=True)).astype(BF16)
        vals_t = jnp.concatenate([r[sl, :] for r in vt_refs], axis=1)
        vm = jnp.where(own_feat, vals_t, jnp.ones_like(vals_t))
        acc = jnp.dot(vm, pt, preferred_element_type=F32)
        lo, hi = acc[:HEAD_DIM], acc[HEAD_DIM:]
        return hi / lo if h % 2 else lo / hi

    halves = []
    ahead = 3
    pending = [scores(h) for h in range(ahead)]
    for h in range(heads):
        if h + ahead < heads:
            pending.append(scores(h + ahead))
        halves.append(weighted_values(h, pending.pop(0)))
        if h % 2:
            sl = slice(h // 2 * LANES, (h // 2 + 1) * LANES)
            o_ref[:, sl] = jnp.concatenate(halves, axis=0).T.astype(o_ref.dtype)
            halves = []


def _attn_prompt(qk, v_t, rel_rows, batch, seq):
    t, width = v_t.shape[1], v_t.shape[0]
    step = LEFT_LEN
    assert seq % step == 0 and step % ATTN_TQ == 0
    nt = seq // step
    blk = (step, width)
    here = lambda b, i: b * nt + i
    before = lambda b, i: b * nt + jnp.maximum(i - 1, 0)
    return pl.pallas_call(
        _attn_prompt_kernel,
        out_shape=jax.ShapeDtypeStruct((t, width), BF16),
        grid=(batch, nt),
        in_specs=[
            pl.BlockSpec(blk, lambda b, i: (here(b, i), 0)),
            pl.BlockSpec(blk, lambda b, i: (before(b, i), 1)),
            pl.BlockSpec(blk, lambda b, i: (here(b, i), 1)),
            pl.BlockSpec((width, step), lambda b, i: (0, before(b, i))),
            pl.BlockSpec((width, step), lambda b, i: (0, here(b, i))),
            pl.BlockSpec(rel_rows.shape, lambda b, i: (0, 0)),
        ],
        out_specs=pl.BlockSpec(blk, lambda b, i: (here(b, i), 0)),
        scratch_shapes=[pltpu.VMEM((rel_rows.shape[0], ATTN_NK, ATTN_TQ), F32)],
        compiler_params=_params(("arbitrary", "arbitrary"), 40),
    )(qk, qk, qk, v_t, v_t, rel_rows)


def _attn_sample_kernel(q_ref, kn_ref, vn_ref, kct_ref, vct_ref, rel_ref, o_ref, bias_ref):
    nq = q_ref.shape[0]
    lc = kct_ref.shape[1]

    @pl.when(pl.program_id(0) == 0)
    def _():
        for h in range(bias_ref.shape[0]):
            bias_ref[h] = _toeplitz(rel_ref[h:h + 1, :], nq)[:, :bias_ref.shape[2]]

    lane = lax.broadcasted_iota(jnp.int32, (nq, LANES), 1)
    nt_dims = (((1,), (1,)), ((), ()))

    def scores(h):
        sl = slice(h // 2 * LANES, (h // 2 + 1) * LANES)
        own = (lane >= HEAD_DIM) if h % 2 else (lane < HEAD_DIM)
        qm = jnp.where(own, q_ref[:, sl], jnp.zeros((nq, LANES), BF16))
        s_cache = jnp.dot(qm, kct_ref[sl, :].astype(BF16), preferred_element_type=F32)
        s_new = lax.dot_general(qm, kn_ref[:, sl], nt_dims, preferred_element_type=F32)
        return s_cache + bias_ref[h, :, :lc], s_new + bias_ref[h, :, lc:lc + nq]

    def weighted_values(h, s_cache, s_new):
        sl = slice(h // 2 * LANES, (h // 2 + 1) * LANES)
        m = jnp.maximum(jnp.max(s_cache, axis=-1, keepdims=True), jnp.max(s_new, axis=-1, keepdims=True))
        p_cache, p_new = jnp.exp2(s_cache - m), jnp.exp2(s_new - m)
        denom = jnp.sum(p_cache, axis=-1, keepdims=True) + jnp.sum(p_new, axis=-1, keepdims=True)
        acc = lax.dot_general(p_cache.astype(BF16), vct_ref[sl, :].astype(BF16), nt_dims,
                              preferred_element_type=F32)
        acc = acc + jnp.dot(p_new.astype(BF16), vn_ref[:, sl], preferred_element_type=F32)
        return acc / denom

    heads = bias_ref.shape[0]
    ahead = 3
    pending = [scores(h) for h in range(ahead)]
    for h in range(heads):
        if h + ahead < heads:
            pending.append(scores(h + ahead))
        out = weighted_values(h, *pending.pop(0))
        if h % 2:
            sl = slice(h // 2 * LANES, (h // 2 + 1) * LANES)
            o_ref[:, sl] = jnp.where(lane < HEAD_DIM, prev_out, out).astype(o_ref.dtype)
        prev_out = out


def _attn_sample(qk, v, cache_kt, cache_vt, rel_rows, batch, seq):
    t, width = v.shape
    lc = cache_kt.shape[2]
    blk = (seq, width)
    bias_cols = -(-(lc + seq) // LANES) * LANES
    return pl.pallas_call(
        _attn_sample_kernel,
        out_shape=jax.ShapeDtypeStruct((t, width), BF16),
        grid=(batch,),
        in_specs=[
            pl.BlockSpec(blk, lambda b: (b, 0)),
            pl.BlockSpec(blk, lambda b: (b, 1)),
            pl.BlockSpec(blk, lambda b: (b, 0)),
            pl.BlockSpec((None, width, lc), lambda b: (b, 0, 0)),
            pl.BlockSpec((None, width, lc), lambda b: (b, 0, 0)),
            pl.BlockSpec(rel_rows.shape, lambda b: (0, 0)),
        ],
        out_specs=pl.BlockSpec(blk, lambda b: (b, 0)),
        scratch_shapes=[pltpu.VMEM((rel_rows.shape[0], seq, bias_cols), F32)],
        compiler_params=_params(("arbitrary",), 32),
    )(qk, qk, v, cache_kt, cache_vt, rel_rows)


def _mix_kernel(seq_rows, attn_ref, cb_ref, u_ref, halo_ref, gates_ref, x_ref, wa_ref, wc_ref, wo_ref,
                cw_ref, cbias_ref, g_ref, o_ref):
    tm, d = x_ref.shape
    u = u_ref[...]
    row = lax.broadcasted_iota(jnp.int32, u.shape, 0)
    if len(halo_ref.shape) == 3:
        nseq = halo_ref.shape[0]
        state = halo_ref[...]
        per_row = lambda r: jnp.broadcast_to(state[:, r:r + 1, :], (nseq, seq_rows, u.shape[1])).reshape(u.shape)
        prev2, prev1 = per_row(0), per_row(1)
        row = row % seq_rows
    else:
        keep = ((pl.program_id(0) * tm) % seq_rows != 0).astype(F32)
        nh = halo_ref.shape[0]
        prev2, prev1 = halo_ref[nh - 2:nh - 1, :] * keep, halo_ref[nh - 1:nh, :] * keep
    um1 = jnp.where(row == 0, prev1, pltpu.roll(u, 1, axis=0))
    um2 = jnp.where(row == 0, prev2, jnp.where(row == 1, prev1, pltpu.roll(u, 2, axis=0)))
    conv = cbias_ref[...] + cw_ref[0:1, :] * um2 + cw_ref[1:2, :] * um1 + cw_ref[2:3, :] * u
    ya = jnp.dot(attn_ref[...], wa_ref[...], preferred_element_type=F32)
    yb = jnp.dot((cb_ref[...].astype(F32) * conv).astype(BF16), wc_ref[...], preferred_element_type=F32)
    merged = gates_ref[:, :d].astype(F32) * ya + gates_ref[:, d:].astype(F32) * yb
    mix = jnp.dot(merged.astype(BF16), wo_ref[...], preferred_element_type=F32)
    o_ref[...] = x_ref[...] + _rms(mix, g_ref[...])


def _mix(attn, cb, u, state, gates, x, wa, wc, wo, conv_w, conv_b, g, tm, seq_rows):
    t, d = x.shape
    c = u.shape[1]
    halo_rows = 8
    if state is None:
        halo, halo_spec = u, pl.BlockSpec(
            (halo_rows, c), lambda i: (jnp.maximum(i * (tm // halo_rows) - 1, 0), 0))
    else:
        assert tm % seq_rows == 0
        halo, halo_spec = state, pl.BlockSpec((tm // seq_rows,) + state.shape[1:], lambda i: (i, 0, 0))
    row_spec = lambda w: pl.BlockSpec((tm, w), lambda i: (i, 0))
    resident = lambda a: pl.BlockSpec(a.shape, lambda i: (0,) * a.ndim, pipeline_mode=pl.Buffered(1))
    return pl.pallas_call(
        functools.partial(_mix_kernel, seq_rows),
        out_shape=jax.ShapeDtypeStruct((t, d), F32),
        grid=(t // tm,),
        in_specs=[row_spec(attn.shape[1]), row_spec(c), row_spec(c), halo_spec, row_spec(2 * d), row_spec(d),
                  resident(wa), resident(wc), resident(wo), resident(conv_w), resident(conv_b), resident(g)],
        out_specs=row_spec(d),
        compiler_params=_params(("arbitrary",), 58),
    )(attn, cb, u, halo, gates, x, wa, wc, wo, conv_w, conv_b, g)


def _mlp_kernel(nj, h_ref, g1_ref, wu_ref, wd_ref, g2_ref, o_ref, n_ref, acc_ref):
    j = pl.program_id(1)

    def step(first, last):
        def body():
            if first:
                n_ref[...] = _rms(h_ref[...], g1_ref[...]).astype(BF16)
            a = jnp.dot(n_ref[...], wu_ref[...], preferred_element_type=F32)
            a = jnp.square(jnp.maximum(a, 0.0)).astype(BF16)
            part = jnp.dot(a, wd_ref[...], preferred_element_type=F32)
            if last:
                f = part if first else acc_ref[...] + part
                o_ref[...] = h_ref[...] + _rms(f, g2_ref[...])
            elif first:
                acc_ref[...] = part
            else:
                acc_ref[...] += part
        return body

    if nj == 1:
        step(True, True)()
    else:
        pl.when(j == 0)(step(True, False))
        pl.when(j == nj - 1)(step(False, True))
        if nj > 2:
            pl.when((j > 0) & (j < nj - 1))(step(False, False))


def _mlp(h, g1, w_up, w_down, g2, tm):
    t, d = h.shape
    tf = MLP_TF
    nj = w_up.shape[1] // tf
    return pl.pallas_call(
        functools.partial(_mlp_kernel, nj),
        out_shape=jax.ShapeDtypeStruct((t, d), F32),
        grid=(t // tm, nj),
        in_specs=[
            pl.BlockSpec((tm, d), lambda i, j: (i, 0)),
            pl.BlockSpec((1, d), lambda i, j: (0, 0)),
            pl.BlockSpec((d, tf), lambda i, j: (0, j)),
            pl.BlockSpec((tf, d), lambda i, j: (j, 0)),
            pl.BlockSpec((1, d), lambda i, j: (0, 0)),
        ],
        out_specs=pl.BlockSpec((tm, d), lambda i, j: (i, 0)),
        scratch_shapes=[pltpu.VMEM((tm, d), BF16), pltpu.VMEM((tm, d), F32)],
        compiler_params=_params(("arbitrary", "arbitrary"), 56),
    )(h, g1, w_up, w_down, g2)


def _pe_kernel(row_splits, h_ref, p_ref, wg_ref, wp_ref, g_ref, o_ref):
    rows = h_ref.shape[0] // row_splits

    def products(r):
        sl = slice(r * rows, (r + 1) * rows)
        return (jnp.dot(h_ref[sl, :].astype(BF16), wg_ref[...], preferred_element_type=F32),
                jnp.dot(p_ref[sl, :].astype(BF16), wp_ref[...], preferred_element_type=F32))

    pending = products(0)
    for r in range(row_splits):
        gate_pre, pe = pending
        if r + 1 < row_splits:
            pending = products(r + 1)
        sl = slice(r * rows, (r + 1) * rows)
        o_ref[sl, :] = h_ref[sl, :] + _rms(jax.nn.sigmoid(gate_pre) * pe, g_ref[...])


def _pe(h, p, wg, wp, g, tm):
    t, d = h.shape
    resident = lambda a: pl.BlockSpec(a.shape, lambda i: (0,) * a.ndim, pipeline_mode=pl.Buffered(1))
    return pl.pallas_call(
        functools.partial(_pe_kernel, max(tm // 256, 1)),
        out_shape=jax.ShapeDtypeStruct((t, d), F32),
        grid=(t // tm,),
        in_specs=[pl.BlockSpec((tm, d), lambda i: (i, 0)), pl.BlockSpec((tm, p.shape[1]), lambda i: (i, 0)),
                  resident(wg), resident(wp), resident(g)],
        out_specs=pl.BlockSpec((tm, d), lambda i: (i, 0)),
        compiler_params=_params(("arbitrary",), 48),
    )(h, p, wg, wp, g)


def _layer(x, pe_in, seq, tiles_per_seq, attn_fn, state, w, tm_proj, tm_mix, tm_mlp, tm_pe):
    t = x.shape[0]
    qk, *kv, cb, u, gates = _in_proj(x, w["g_pre_mix"], w["w_in"], min(tm_proj, t), tiles_per_seq)
    attn = attn_fn(qk, kv[0])
    h = _mix(attn, cb, u, state, gates, x, w["w_attn_out"], w["w_conv_out"], w["w_o"], w["conv_w"],
             w["conv_b"], w["g_post_mix"], min(tm_mix, t), seq)
    h = _mlp(h, w["g_pre_mlp"], w["w_up"], w["w_down"], w["g_post_mlp"], min(tm_mlp, t))
    h = _pe(h, pe_in, w["w_pe_gate"], w["w_pe"], w["g_pe"], min(tm_pe, t))
    return h, kv, u.reshape(t // seq, seq, -1)[:, -2:]


def kernel(x_prompt, x_sample, cache_k, cache_v, state_conv, p_prompt, p_sample, w_in, rel_table, w_attn_out,
           conv_w, conv_b, w_conv_out, w_o, g_pre_mix, g_post_mix, g_pre_mlp, g_post_mlp, w_up, w_down, w_pe,
           w_pe_gate, g_pe):
    depth = w_in.shape[0]
    bp, sp, d = x_prompt.shape
    bs, ss, _ = x_sample.shape
    lc, heads = cache_k.shape[2], cache_k.shape[3]
    lp = min(LEFT_LEN, sp)
    assert sp % lp == 0 and sp % ATTN_TQ == 0 and lc == LEFT_LEN and ss <= CHUNK
    assert rel_table.shape[2] == 2 * REL_CLIP + 1

    hp = x_prompt.reshape(bp * sp, d)
    hs = x_sample.reshape(bs * ss, d)
    outs = [[] for _ in range(6)]
    for i in range(depth):
        w = {
            "w_in": w_in[i].astype(BF16), "w_attn_out": w_attn_out[i].astype(BF16),
            "w_conv_out": w_conv_out[i].astype(BF16), "w_o": w_o[i].astype(BF16),
            "w_up": w_up[i].astype(BF16), "w_down": w_down[i].astype(BF16),
            "w_pe": w_pe[i].astype(BF16), "w_pe_gate": w_pe_gate[i].astype(BF16),
            "conv_w": conv_w[i], "conv_b": conv_b[i][None],
            "g_pre_mix": g_pre_mix[i][None], "g_post_mix": g_post_mix[i][None],
            "g_pre_mlp": g_pre_mlp[i][None], "g_post_mlp": g_post_mlp[i][None], "g_pe": g_pe[i][None],
        }
        rel_rows = _rel_rows(rel_table[i])
        kct = cache_k[i].transpose(0, 2, 3, 1).reshape(bs, heads * HEAD_DIM, lc)
        vct = cache_v[i].transpose(0, 2, 3, 1).reshape(bs, heads * HEAD_DIM, lc)

        hp, (_, ktp, vtp), cpr = _layer(
            hp, p_prompt[i].reshape(bp * sp, -1), sp, sp // lp,
            lambda qk, v_t: _attn_prompt(qk, v_t, rel_rows, bp, sp), None, w,
            tm_proj=lp, tm_mix=512, tm_mlp=512, tm_pe=512)
        hs, (_, kvs), csm = _layer(
            hs, p_sample[i].reshape(bs * ss, -1), ss, None,
            lambda qk, v: _attn_sample(qk, v, kct, vct, rel_rows, bs, ss), state_conv[i], w,
            tm_proj=512, tm_mix=256, tm_mlp=512, tm_pe=512)
        kpr = ktp.reshape(bp, heads, HEAD_DIM, lp).transpose(0, 3, 1, 2)
        vpr = vtp.reshape(bp, heads, HEAD_DIM, lp).transpose(0, 3, 1, 2)
        ksm = kvs[:, :heads * HEAD_DIM].reshape(bs, ss, heads, HEAD_DIM)
        vsm = kvs[:, heads * HEAD_DIM:].reshape(bs, ss, heads, HEAD_DIM)
        for lst, val in zip(outs, (kpr, vpr, cpr, ksm, vsm, csm)):
            lst.append(val)
    k_prompt, v_prompt, conv_prompt, k_sample, v_sample, conv_sample = [jnp.stack(o) for o in outs]
    return (hp.reshape(bp, sp, d), hs.reshape(bs, ss, d), k_prompt, v_prompt, conv_prompt,
            k_sample, v_sample, conv_sample)
```

```python
import functools

import jax
import jax.numpy as jnp
from jax import lax
from jax.experimental import pallas as pl
from jax.experimental.pallas import tpu as pltpu

F32 = jnp.float32
BF16 = jnp.bfloat16

CHUNK = 64
LEFT_CHUNKS = 8
LEFT_LEN = LEFT_CHUNKS * CHUNK
BAND = (LEFT_CHUNKS + 1) * CHUNK
HEAD_DIM = 64
REL_CLIP = 128
EPS = 1e-6
NEG = -1e30
LOG2E = 1.4426950408889634
SCORE_SCALE = HEAD_DIM ** -0.5 * LOG2E

LANES = 128
ATTN_TQ = 256
ATTN_NK = LEFT_LEN + ATTN_TQ
ATTN_SLOTS = ATTN_NK // ATTN_TQ
REL_ROW = ATTN_NK + ATTN_TQ
IN_PROJ_TN = 2048
MLP_TF = 1024
MIB = 1024 * 1024


def _rms(x, g):
    return x * lax.rsqrt(jnp.mean(x * x, axis=-1, keepdims=True) + EPS) * g


def _sigmoid(x):
    return 0.5 * jnp.tanh(0.5 * x) + 0.5


VMEM_MIB = {"in_proj": 56, "attn_prompt": 40, "attn_sample": 32, "mix": 58, "mlp": 56, "pe": 48}


def _params(call, grid_rank):
    return pltpu.CompilerParams(dimension_semantics=("arbitrary",) * grid_rank,
                                vmem_limit_bytes=VMEM_MIB[call] * MIB)


STEP_QK = 2


def _in_proj_kernel(tiles_per_seq, x_ref, g_ref, w_ref, qk_ref, *refs):
    if tiles_per_seq is None:
        v_ref, kv_ref, cb_ref, u_ref, gates_ref, n_ref = refs
    else:
        v_ref, kt_ref, vt_ref, cb_ref, u_ref, gates_ref, n_ref = refs
        is_tail = pl.program_id(0) % tiles_per_seq == tiles_per_seq - 1
    j = pl.program_id(1)
    half = w_ref.shape[1] // 2

    def proj():
        z = jnp.dot(n_ref[...], w_ref[...], preferred_element_type=F32)
        return z[:, :half], z[:, half:]

    def step(jj, tail, body):
        cond = j == jj
        if tail is not None and tiles_per_seq is not None:
            cond = cond & (is_tail if tail else jnp.logical_not(is_tail))
        pl.when(cond)(body)

    def q_k(tail):
        def body():
            q, k = proj()
            qk_ref[:, :half] = (q * SCORE_SCALE).astype(BF16)
            qk_ref[:, half:] = k.astype(BF16)
            if tiles_per_seq is None:
                kv_ref[:, :half] = k
            elif tail:
                kt_ref[...] = k.T
        return body

    def v_bg(tail):
        def body():
            v, bg = proj()
            cb_ref[...] = bg.astype(BF16)
            if tiles_per_seq is None:
                v_ref[...] = v.astype(BF16)
                kv_ref[:, half:] = v
            else:
                vt = v.T
                v_ref[...] = vt.astype(BF16)
                if tail:
                    vt_ref[...] = vt
        return body

    def conv_in():
        cg, xin = proj()
        u_ref[...] = cg * xin

    def gate(first):
        def body():
            if first:
                n_ref[...] = _rms(x_ref[...], g_ref[...]).astype(BF16)
            za, zb = proj()
            gates_ref[:, :half] = _sigmoid(za).astype(BF16)
            gates_ref[:, half:] = _sigmoid(zb).astype(BF16)
        return body

    step(0, None, gate(True))
    pl.when((j > 0) & (j < STEP_QK))(gate(False))
    for tail in ((None,) if tiles_per_seq is None else (False, True)):
        step(STEP_QK, tail, q_k(tail))
        step(STEP_QK + 1, tail, v_bg(tail))
    step(STEP_QK + 2, None, conv_in)


def _in_proj(x, g, w_in, tm, tiles_per_seq):
    t, d = x.shape
    tn = IN_PROJ_TN
    nj = w_in.shape[1] // tn
    half = tn // 2
    assert nj == 5 and d == tn and t % tm == 0

    rows = lambda w: pl.BlockSpec((tm, w), lambda i, j: (i, 0))
    if tiles_per_seq is None:
        kv_shapes = (jax.ShapeDtypeStruct((t, half), BF16),
                     jax.ShapeDtypeStruct((t, tn), F32))
        kv_specs = (rows(half), rows(tn))
    else:
        nseq = t // (tm * tiles_per_seq)
        tail_spec = pl.BlockSpec((None, half, tm), lambda i, j: (i // tiles_per_seq, 0, 0))
        kv_shapes = (jax.ShapeDtypeStruct((half, t), BF16),
                     jax.ShapeDtypeStruct((nseq, half, tm), F32),
                     jax.ShapeDtypeStruct((nseq, half, tm), F32))
        kv_specs = (pl.BlockSpec((half, tm), lambda i, j: (0, i)), tail_spec, tail_spec)
    out_shape = (
        jax.ShapeDtypeStruct((t, tn), BF16),
        *kv_shapes,
        jax.ShapeDtypeStruct((t, half), BF16),
        jax.ShapeDtypeStruct((t, half), F32),
        jax.ShapeDtypeStruct((t, 2 * tn), BF16),
    )
    return pl.pallas_call(
        functools.partial(_in_proj_kernel, tiles_per_seq),
        out_shape=out_shape,
        grid=(t // tm, nj),
        in_specs=[
            pl.BlockSpec((tm, d), lambda i, j: (i, 0)),
            pl.BlockSpec((1, d), lambda i, j: (0, 0)),
            pl.BlockSpec((d, tn), lambda i, j: (0, (j + nj - STEP_QK) % nj)),
        ],
        out_specs=(
            rows(tn),
            *kv_specs,
            rows(half),
            rows(half),
            pl.BlockSpec((tm, tn), lambda i, j: (i, jnp.minimum(j, STEP_QK - 1))),
        ),
        scratch_shapes=[pltpu.VMEM((tm, d), BF16)],
        compiler_params=_params("in_proj", 2),
    )(x, g, w_in)


def _rel_rows(rel_table):
    h, n_rel = rel_table.shape
    lo = LEFT_LEN - REL_CLIP
    hi = ATTN_NK - lo - n_rel
    first = jnp.broadcast_to(rel_table[:, :1], (h, lo))
    last = jnp.broadcast_to(rel_table[:, -1:], (h, hi))
    negative = jnp.broadcast_to(rel_table[:, :1], (h, REL_ROW - ATTN_NK))
    return jnp.concatenate([first, rel_table, last, negative], axis=1).astype(F32) * LOG2E


def _toeplitz(row, nq):
    return pltpu.roll(jnp.broadcast_to(row, (nq, row.shape[1])), 0, 1, stride=1, stride_axis=0)


def _attn_prompt_kernel(q_ref, kp_ref, kc_ref, vtp_ref, vtc_ref, rel_ref, o_ref, bias_ref):
    t = pl.program_id(1)

    @pl.when((pl.program_id(0) == 0) & (t == 0))
    def _():
        qi = lax.broadcasted_iota(jnp.int32, (ATTN_TQ, ATTN_NK), 0)
        kj = lax.broadcasted_iota(jnp.int32, (ATTN_TQ, ATTN_NK), 1)
        first = (qi // CHUNK) * CHUNK
        visible = (kj >= first) & (kj < first + BAND)
        for h in range(bias_ref.shape[0]):
            bias_ref[h] = jnp.where(visible, _toeplitz(rel_ref[h:h + 1, :], ATTN_TQ)[:, :ATTN_NK], NEG).T

    tiles = q_ref.shape[0] // ATTN_TQ
    prev_slots = kp_ref.shape[0] // ATTN_TQ

    def key_slot(s):
        k_ref, vt_ref = (kp_ref, vtp_ref) if s < prev_slots else (kc_ref, vtc_ref)
        r0 = (s % prev_slots) * ATTN_TQ
        return k_ref.at[pl.ds(r0, ATTN_TQ)], vt_ref.at[:, pl.ds(r0, ATTN_TQ)]

    for s in range(tiles):
        g = t * tiles + s
        rows = pl.ds(s * ATTN_TQ, ATTN_TQ)
        for first_slot in range(ATTN_SLOTS):
            tiles_before = ATTN_SLOTS - 1 - first_slot
            if first_slot and (tiles_before < s or (tiles_before - s) % tiles):
                continue
            cond = (g >= tiles_before) if first_slot == 0 else (g == tiles_before)

            @pl.when(cond)
            def _(s=s, rows=rows, first_slot=first_slot):
                slots = [key_slot(s + prev_slots - (ATTN_SLOTS - 1) + n) for n in range(first_slot, ATTN_SLOTS)]
                _band_heads(q_ref.at[rows], [k for k, _ in slots], [v for _, v in slots], bias_ref,
                            first_slot * ATTN_TQ, o_ref.at[rows])


def _band_heads(q_ref, k_refs, vt_refs, bias_ref, key0, o_ref):
    nk = len(k_refs) * ATTN_TQ
    heads = 2 * (q_ref.shape[1] // LANES)
    lane = lax.broadcasted_iota(jnp.int32, (ATTN_TQ, LANES), 1)
    feat = lax.broadcasted_iota(jnp.int32, (LANES, nk), 0)

    def scores(h):
        sl = slice(h // 2 * LANES, (h // 2 + 1) * LANES)
        own_lane = (lane >= HEAD_DIM) if h % 2 else (lane < HEAD_DIM)
        q_pair = q_ref[:, sl]
        qm = jnp.where(own_lane, q_pair, jnp.zeros_like(q_pair))
        keys = jnp.concatenate([r[:, sl] for r in k_refs], axis=0)
        st = lax.dot_general(keys, qm, (((1,), (1,)), ((), ())), preferred_element_type=F32)
        return st + bias_ref[h, key0:key0 + nk, :]

    def weighted_values(h, st):
        sl = slice(h // 2 * LANES, (h // 2 + 1) * LANES)
        own_feat = (feat >= HEAD_DIM) if h % 2 else (feat < HEAD_DIM)
        pt = jnp.exp2(st - jnp.max(st, axis=0, keepdims=True)).astype(BF16)
        vals_t = jnp.concatenate([r[sl, :] for r in vt_refs], axis=1)
        vm = jnp.where(own_feat, vals_t, jnp.ones_like(vals_t))
        acc = jnp.dot(vm, pt, preferred_element_type=F32)
        lo, hi = acc[:HEAD_DIM], acc[HEAD_DIM:]
        return hi / lo if h % 2 else lo / hi

    halves = []
    ahead = 3
    pending = [scores(h) for h in range(ahead)]
    for h in range(heads):
        if h + ahead < heads:
            pending.append(scores(h + ahead))
        halves.append(weighted_values(h, pending.pop(0)))
        if h % 2:
            sl = slice(h // 2 * LANES, (h // 2 + 1) * LANES)
            o_ref[:, sl] = jnp.concatenate(halves, axis=0).T.astype(o_ref.dtype)
            halves = []


def _attn_prompt(qk, v_t, rel_rows, batch, seq):
    t, width = v_t.shape[1], v_t.shape[0]
    step = LEFT_LEN
    assert seq % step == 0 and step % ATTN_TQ == 0
    nt = seq // step
    blk = (step, width)
    here = lambda b, i: b * nt + i
    before = lambda b, i: b * nt + jnp.maximum(i - 1, 0)
    return pl.pallas_call(
        _attn_prompt_kernel,
        out_shape=jax.ShapeDtypeStruct((t, width), BF16),
        grid=(batch, nt),
        in_specs=[
            pl.BlockSpec(blk, lambda b, i: (here(b, i), 0)),
            pl.BlockSpec(blk, lambda b, i: (before(b, i), 1)),
            pl.BlockSpec(blk, lambda b, i: (here(b, i), 1)),
            pl.BlockSpec((width, step), lambda b, i: (0, before(b, i))),
            pl.BlockSpec((width, step), lambda b, i: (0, here(b, i))),
            pl.BlockSpec(rel_rows.shape, lambda b, i: (0, 0)),
        ],
        out_specs=pl.BlockSpec(blk, lambda b, i: (here(b, i), 0)),
        scratch_shapes=[pltpu.VMEM((rel_rows.shape[0], ATTN_NK, ATTN_TQ), F32)],
        compiler_params=_params("attn_prompt", 2),
    )(qk, qk, qk, v_t, v_t, rel_rows)


def _attn_sample_kernel(q_ref, kn_ref, vn_ref, kct_ref, vct_ref, rel_ref, o_ref, bias_ref):
    nq = q_ref.shape[0]
    lc = kct_ref.shape[1]

    @pl.when(pl.program_id(0) == 0)
    def _():
        for h in range(bias_ref.shape[0]):
            bias_ref[h] = _toeplitz(rel_ref[h:h + 1, :], nq)[:, :bias_ref.shape[2]]

    lane = lax.broadcasted_iota(jnp.int32, (nq, LANES), 1)
    nt_dims = (((1,), (1,)), ((), ()))

    def scores(h):
        sl = slice(h // 2 * LANES, (h // 2 + 1) * LANES)
        own = (lane >= HEAD_DIM) if h % 2 else (lane < HEAD_DIM)
        qm = jnp.where(own, q_ref[:, sl], jnp.zeros((nq, LANES), BF16))
        s_cache = jnp.dot(qm, kct_ref[sl, :].astype(BF16), preferred_element_type=F32)
        s_new = lax.dot_general(qm, kn_ref[:, sl], nt_dims, preferred_element_type=F32)
        return s_cache + bias_ref[h, :, :lc], s_new + bias_ref[h, :, lc:lc + nq]

    def weighted_values(h, s_cache, s_new):
        sl = slice(h // 2 * LANES, (h // 2 + 1) * LANES)
        m = jnp.maximum(jnp.max(s_cache, axis=-1, keepdims=True), jnp.max(s_new, axis=-1, keepdims=True))
        p_cache, p_new = jnp.exp2(s_cache - m), jnp.exp2(s_new - m)
        denom = jnp.sum(p_cache, axis=-1, keepdims=True) + jnp.sum(p_new, axis=-1, keepdims=True)
        acc = lax.dot_general(p_cache.astype(BF16), vct_ref[sl, :].astype(BF16), nt_dims,
                              preferred_element_type=F32)
        acc = acc + jnp.dot(p_new.astype(BF16), vn_ref[:, sl], preferred_element_type=F32)
        return acc / denom

    heads = bias_ref.shape[0]
    ahead = 3
    pending = [scores(h) for h in range(ahead)]
    for h in range(heads):
        if h + ahead < heads:
            pending.append(scores(h + ahead))
        out = weighted_values(h, *pending.pop(0))
        if h % 2:
            sl = slice(h // 2 * LANES, (h // 2 + 1) * LANES)
            o_ref[:, sl] = jnp.where(lane < HEAD_DIM, prev_out, out).astype(o_ref.dtype)
        prev_out = out


def _attn_sample(qk, v, cache_kt, cache_vt, rel_rows, batch, seq):
    t, width = v.shape
    lc = cache_kt.shape[2]
    blk = (seq, width)
    bias_cols = -(-(lc + seq) // LANES) * LANES
    return pl.pallas_call(
        _attn_sample_kernel,
        out_shape=jax.ShapeDtypeStruct((t, width), BF16),
        grid=(batch,),
        in_specs=[
            pl.BlockSpec(blk, lambda b: (b, 0)),
            pl.BlockSpec(blk, lambda b: (b, 1)),
            pl.BlockSpec(blk, lambda b: (b, 0)),
            pl.BlockSpec((None, width, lc), lambda b: (b, 0, 0)),
            pl.BlockSpec((None, width, lc), lambda b: (b, 0, 0)),
            pl.BlockSpec(rel_rows.shape, lambda b: (0, 0)),
        ],
        out_specs=pl.BlockSpec(blk, lambda b: (b, 0)),
        scratch_shapes=[pltpu.VMEM((rel_rows.shape[0], seq, bias_cols), F32)],
        compiler_params=_params("attn_sample", 1),
    )(qk, qk, v, cache_kt, cache_vt, rel_rows)


def _mix_kernel(seq_rows, attn_ref, cb_ref, u_ref, halo_ref, gates_ref, x_ref, wa_ref, wc_ref, wo_ref,
                cw_ref, cbias_ref, g_ref, o_ref):
    tm, d = x_ref.shape
    u = u_ref[...]
    row = lax.broadcasted_iota(jnp.int32, u.shape, 0)
    if len(halo_ref.shape) == 3:
        nseq = halo_ref.shape[0]
        state = halo_ref[...]
        per_row = lambda r: jnp.broadcast_to(state[:, r:r + 1, :], (nseq, seq_rows, u.shape[1])).reshape(u.shape)
        prev2, prev1 = per_row(0), per_row(1)
        row = row % seq_rows
    else:
        keep = ((pl.program_id(0) * tm) % seq_rows != 0).astype(F32)
        nh = halo_ref.shape[0]
        prev2, prev1 = halo_ref[nh - 2:nh - 1, :] * keep, halo_ref[nh - 1:nh, :] * keep
    um1 = jnp.where(row == 0, prev1, pltpu.roll(u, 1, axis=0))
    um2 = jnp.where(row == 0, prev2, jnp.where(row == 1, prev1, pltpu.roll(u, 2, axis=0)))
    conv = cbias_ref[...] + cw_ref[0:1, :] * um2 + cw_ref[1:2, :] * um1 + cw_ref[2:3, :] * u
    ya = jnp.dot(attn_ref[...], wa_ref[...], preferred_element_type=F32)
    yb = jnp.dot((cb_ref[...].astype(F32) * conv).astype(BF16), wc_ref[...], preferred_element_type=F32)
    merged = gates_ref[:, :d].astype(F32) * ya + gates_ref[:, d:].astype(F32) * yb
    mix = jnp.dot(merged.astype(BF16), wo_ref[...], preferred_element_type=F32)
    o_ref[...] = x_ref[...] + _rms(mix, g_ref[...])


def _mix(attn, cb, u, state, gates, x, wa, wc, wo, conv_w, conv_b, g, tm, seq_rows):
    t, d = x.shape
    c = u.shape[1]
    halo_rows = 8
    if state is None:
        halo, halo_spec = u, pl.BlockSpec(
            (halo_rows, c), lambda i: (jnp.maximum(i * (tm // halo_rows) - 1, 0), 0))
    else:
        assert tm % seq_rows == 0
        halo, halo_spec = state, pl.BlockSpec((tm // seq_rows,) + state.shape[1:], lambda i: (i, 0, 0))
    row_spec = lambda w: pl.BlockSpec((tm, w), lambda i: (i, 0))
    resident = lambda a: pl.BlockSpec(a.shape, lambda i: (0,) * a.ndim, pipeline_mode=pl.Buffered(1))
    return pl.pallas_call(
        functools.partial(_mix_kernel, seq_rows),
        out_shape=jax.ShapeDtypeStruct((t, d), F32),
        grid=(t // tm,),
        in_specs=[row_spec(attn.shape[1]), row_spec(c), row_spec(c), halo_spec, row_spec(2 * d), row_spec(d),
                  resident(wa), resident(wc), resident(wo), resident(conv_w), resident(conv_b), resident(g)],
        out_specs=row_spec(d),
        compiler_params=_params("mix", 1),
    )(attn, cb, u, halo, gates, x, wa, wc, wo, conv_w, conv_b, g)


def _mlp_kernel(nj, h_ref, g1_ref, wu_ref, wd_ref, g2_ref, o_ref, n_ref, acc_ref):
    j = pl.program_id(1)

    def step(first, last):
        def body():
            if first:
                n_ref[...] = _rms(h_ref[...], g1_ref[...]).astype(BF16)
            a = jnp.dot(n_ref[...], wu_ref[...], preferred_element_type=F32)
            a = jnp.square(jnp.maximum(a, 0.0)).astype(BF16)
            part = jnp.dot(a, wd_ref[...], preferred_element_type=F32)
            if last:
                f = part if first else acc_ref[...] + part
                o_ref[...] = h_ref[...] + _rms(f, g2_ref[...])
            elif first:
                acc_ref[...] = part
            else:
                acc_ref[...] += part
        return body

    if nj == 1:
        step(True, True)()
    else:
        pl.when(j == 0)(step(True, False))
        pl.when(j == nj - 1)(step(False, True))
        if nj > 2:
            pl.when((j > 0) & (j < nj - 1))(step(False, False))


def _mlp(h, g1, w_up, w_down, g2, tm):
    t, d = h.shape
    tf = MLP_TF
    nj = w_up.shape[1] // tf
    return pl.pallas_call(
        functools.partial(_mlp_kernel, nj),
        out_shape=jax.ShapeDtypeStruct((t, d), F32),
        grid=(t // tm, nj),
        in_specs=[
            pl.BlockSpec((tm, d), lambda i, j: (i, 0)),
            pl.BlockSpec((1, d), lambda i, j: (0, 0)),
            pl.BlockSpec((d, tf), lambda i, j: (0, j)),
            pl.BlockSpec((tf, d), lambda i, j: (j, 0)),
            pl.BlockSpec((1, d), lambda i, j: (0, 0)),
        ],
        out_specs=pl.BlockSpec((tm, d), lambda i, j: (i, 0)),
        scratch_shapes=[pltpu.VMEM((tm, d), BF16), pltpu.VMEM((tm, d), F32)],
        compiler_params=_params("mlp", 2),
    )(h, g1, w_up, w_down, g2)


def _pe_kernel(row_splits, h_ref, p_ref, wg_ref, wp_ref, g_ref, o_ref):
    rows = h_ref.shape[0] // row_splits

    def products(r):
        sl = slice(r * rows, (r + 1) * rows)
        return (jnp.dot(h_ref[sl, :].astype(BF16), wg_ref[...], preferred_element_type=F32),
                jnp.dot(p_ref[sl, :].astype(BF16), wp_ref[...], preferred_element_type=F32))

    pending = products(0)
    for r in range(row_splits):
        gate_pre, pe = pending
        if r + 1 < row_splits:
            pending = products(r + 1)
        sl = slice(r * rows, (r + 1) * rows)
        o_ref[sl, :] = h_ref[sl, :] + _rms(_sigmoid(gate_pre) * pe, g_ref[...])


def _pe(h, p, wg, wp, g, tm):
    t, d = h.shape
    resident = lambda a: pl.BlockSpec(a.shape, lambda i: (0,) * a.ndim, pipeline_mode=pl.Buffered(1))
    return pl.pallas_call(
        functools.partial(_pe_kernel, max(tm // 256, 1)),
        out_shape=jax.ShapeDtypeStruct((t, d), F32),
        grid=(t // tm,),
        in_specs=[pl.BlockSpec((tm, d), lambda i: (i, 0)), pl.BlockSpec((tm, p.shape[1]), lambda i: (i, 0)),
                  resident(wg), resident(wp), resident(g)],
        out_specs=pl.BlockSpec((tm, d), lambda i: (i, 0)),
        compiler_params=_params("pe", 1),
    )(h, p, wg, wp, g)


def _layer(x, pe_in, seq, tiles_per_seq, attn_fn, state, w, tm_proj, tm_mix, tm_mlp, tm_pe):
    t = x.shape[0]
    qk, *kv, cb, u, gates = _in_proj(x, w["g_pre_mix"], w["w_in"], min(tm_proj, t), tiles_per_seq)
    attn = attn_fn(qk, kv[0])
    h = _mix(attn, cb, u, state, gates, x, w["w_attn_out"], w["w_conv_out"], w["w_o"], w["conv_w"],
             w["conv_b"], w["g_post_mix"], min(tm_mix, t), seq)
    h = _mlp(h, w["g_pre_mlp"], w["w_up"], w["w_down"], w["g_post_mlp"], min(tm_mlp, t))
    h = _pe(h, pe_in, w["w_pe_gate"], w["w_pe"], w["g_pe"], min(tm_pe, t))
    return h, kv, u.reshape(t // seq, seq, -1)[:, -2:]


def kernel(x_prompt, x_sample, cache_k, cache_v, state_conv, p_prompt, p_sample, w_in, rel_table, w_attn_out,
           conv_w, conv_b, w_conv_out, w_o, g_pre_mix, g_post_mix, g_pre_mlp, g_post_mlp, w_up, w_down, w_pe,
           w_pe_gate, g_pe):
    depth = w_in.shape[0]
    bp, sp, d = x_prompt.shape
    bs, ss, _ = x_sample.shape
    lc, heads = cache_k.shape[2], cache_k.shape[3]
    lp = min(LEFT_LEN, sp)
    assert sp % lp == 0 and sp % ATTN_TQ == 0 and lc == LEFT_LEN and ss <= CHUNK
    assert rel_table.shape[2] == 2 * REL_CLIP + 1

    hp = x_prompt.reshape(bp * sp, d)
    hs = x_sample.reshape(bs * ss, d)
    outs = [[] for _ in range(6)]
    for i in range(depth):
        w = {
            "w_in": w_in[i].astype(BF16), "w_attn_out": w_attn_out[i].astype(BF16),
            "w_conv_out": w_conv_out[i].astype(BF16), "w_o": w_o[i].astype(BF16),
            "w_up": w_up[i].astype(BF16), "w_down": w_down[i].astype(BF16),
            "w_pe": w_pe[i].astype(BF16), "w_pe_gate": w_pe_gate[i].astype(BF16),
            "conv_w": conv_w[i], "conv_b": conv_b[i][None],
            "g_pre_mix": g_pre_mix[i][None], "g_post_mix": g_post_mix[i][None],
            "g_pre_mlp": g_pre_mlp[i][None], "g_post_mlp": g_post_mlp[i][None], "g_pe": g_pe[i][None],
        }
        rel_rows = _rel_rows(rel_table[i])
        kct = cache_k[i].transpose(0, 2, 3, 1).reshape(bs, heads * HEAD_DIM, lc)
        vct = cache_v[i].transpose(0, 2, 3, 1).reshape(bs, heads * HEAD_DIM, lc)

        hp, (_, ktp, vtp), cpr = _layer(
            hp, p_prompt[i].reshape(bp * sp, -1), sp, sp // lp,
            lambda qk, v_t: _attn_prompt(qk, v_t, rel_rows, bp, sp), None, w,
            tm_proj=lp, tm_mix=512, tm_mlp=512, tm_pe=512)
        hs, (_, kvs), csm = _layer(
            hs, p_sample[i].reshape(bs * ss, -1), ss, None,
            lambda qk, v: _attn_sample(qk, v, kct, vct, rel_rows, bs, ss), state_conv[i], w,
            tm_proj=512, tm_mix=256, tm_mlp=512, tm_pe=512)
        kpr = ktp.reshape(bp, heads, HEAD_DIM, lp).transpose(0, 3, 1, 2)
        vpr = vtp.reshape(bp, heads, HEAD_DIM, lp).transpose(0, 3, 1, 2)
        ksm = kvs[:, :heads * HEAD_DIM].reshape(bs, ss, heads, HEAD_DIM)
        vsm = kvs[:, heads * HEAD_DIM:].reshape(bs, ss, heads, HEAD_DIM)
        for lst, val in zip(outs, (kpr, vpr, cpr, ksm, vsm, csm)):
            lst.append(val)
    k_prompt, v_prompt, conv_prompt, k_sample, v_sample, conv_sample = [jnp.stack(o) for o in outs]
    return (hp.reshape(bp, sp, d), hs.reshape(bs, ss, d), k_prompt, v_prompt, conv_prompt,
            k_sample, v_sample, conv_sample)
```

```python
import functools

import jax
import jax.numpy as jnp
from jax import lax
from jax.experimental import pallas as pl
from jax.experimental.pallas import tpu as pltpu

F32 = jnp.float32
BF16 = jnp.bfloat16

CHUNK = 64
LEFT_CHUNKS = 8
LEFT_LEN = LEFT_CHUNKS * CHUNK
BAND = (LEFT_CHUNKS + 1) * CHUNK
HEAD_DIM = 64
REL_CLIP = 128
EPS = 1e-6
NEG = -1e30
LOG2E = 1.4426950408889634
SCORE_SCALE = HEAD_DIM ** -0.5 * LOG2E

LANES = 128
ATTN_TQ = 256
ATTN_NK = LEFT_LEN + ATTN_TQ
ATTN_SLOTS = ATTN_NK // ATTN_TQ
REL_ROW = ATTN_NK + ATTN_TQ
IN_PROJ_TN = 2048
MLP_TF = 1024
MIB = 1024 * 1024


def _rms(x, g):
    return x * lax.rsqrt(jnp.mean(x * x, axis=-1, keepdims=True) + EPS) * g


def _sigmoid(x):
    return 0.5 * jnp.tanh(0.5 * x) + 0.5


VMEM_MIB = {"in_proj": 56, "attn_prompt": 40, "attn_sample": 32, "mix": 58, "mlp": 56, "pe": 48}


def _params(call, grid_rank):
    return pltpu.CompilerParams(dimension_semantics=("arbitrary",) * grid_rank,
                                vmem_limit_bytes=VMEM_MIB[call] * MIB)


STEP_QK = 2


def _in_proj_kernel(tiles_per_seq, seq_rows, x_ref, g_ref, w_ref, cw_ref, cbias_ref, *refs):
    if tiles_per_seq is None:
        state_ref, qk_ref, v_ref, kv_ref, cv_ref, u_ref, gates_ref, n_ref = refs
    else:
        qk_ref, v_ref, kt_ref, vt_ref, cv_ref, u_ref, gates_ref, n_ref, carry_ref = refs
        is_tail = pl.program_id(0) % tiles_per_seq == tiles_per_seq - 1
    j = pl.program_id(1)
    tm = x_ref.shape[0]
    half = w_ref.shape[1] // 2

    def proj():
        z = jnp.dot(n_ref[...], w_ref[...], preferred_element_type=F32)
        return z[:, :half], z[:, half:]

    def step(jj, tail, body):
        cond = j == jj
        if tail is not None and tiles_per_seq is not None:
            cond = cond & (is_tail if tail else jnp.logical_not(is_tail))
        pl.when(cond)(body)

    def q_k(tail):
        def body():
            q, k = proj()
            qk_ref[:, :half] = (q * SCORE_SCALE).astype(BF16)
            qk_ref[:, half:] = k.astype(BF16)
            if tiles_per_seq is None:
                kv_ref[:, :half] = k
            elif tail:
                kt_ref[...] = k.T
        return body

    def v_bg(tail):
        def body():
            v, bg = proj()
            cv_ref[...] = bg.astype(BF16)
            if tiles_per_seq is None:
                v_ref[...] = v.astype(BF16)
                kv_ref[:, half:] = v
            else:
                vt = v.T
                v_ref[...] = vt.astype(BF16)
                if tail:
                    vt_ref[...] = vt
        return body

    def conv_branch():
        cg, xin = proj()
        u = cg * xin
        u_ref[...] = u
        row = lax.broadcasted_iota(jnp.int32, u.shape, 0)
        if tiles_per_seq is None:
            nseq = state_ref.shape[0]
            state = state_ref[...]
            per_row = lambda r: jnp.broadcast_to(state[:, r:r + 1, :], (nseq, seq_rows, half)).reshape(u.shape)
            prev2, prev1 = per_row(0), per_row(1)
            row = row % seq_rows
        else:
            inside = pl.program_id(0) % tiles_per_seq != 0
            prev2 = jnp.where(inside, carry_ref[6:7, :], 0.0)
            prev1 = jnp.where(inside, carry_ref[7:8, :], 0.0)
            carry_ref[...] = u[tm - carry_ref.shape[0]:, :]
        um1 = jnp.where(row == 0, prev1, pltpu.roll(u, 1, axis=0))
        um2 = jnp.where(row == 0, prev2, jnp.where(row == 1, prev1, pltpu.roll(u, 2, axis=0)))
        conv = cbias_ref[...] + cw_ref[0:1, :] * um2 + cw_ref[1:2, :] * um1 + cw_ref[2:3, :] * u
        cv_ref[...] = (cv_ref[...].astype(F32) * conv).astype(BF16)

    def gate(first):
        def body():
            if first:
                n_ref[...] = _rms(x_ref[...], g_ref[...]).astype(BF16)
            za, zb = proj()
            gates_ref[:, :half] = _sigmoid(za).astype(BF16)
            gates_ref[:, half:] = _sigmoid(zb).astype(BF16)
        return body

    if tiles_per_seq is not None:
        @pl.when((pl.program_id(0) == 0) & (j == 0))
        def _():
            carry_ref[...] = jnp.zeros_like(carry_ref)

    step(0, None, gate(True))
    pl.when((j > 0) & (j < STEP_QK))(gate(False))
    for tail in ((None,) if tiles_per_seq is None else (False, True)):
        step(STEP_QK, tail, q_k(tail))
        step(STEP_QK + 1, tail, v_bg(tail))
    step(STEP_QK + 2, None, conv_branch)


def _in_proj(x, g, w_in, conv_w, conv_b, state, tm, tiles_per_seq, seq_rows):
    t, d = x.shape
    tn = IN_PROJ_TN
    nj = w_in.shape[1] // tn
    half = tn // 2
    assert nj == 5 and d == tn and t % tm == 0

    rows = lambda w: pl.BlockSpec((tm, w), lambda i, j: (i, 0))
    const = lambda a: pl.BlockSpec(a.shape, lambda i, j: (0,) * a.ndim)
    extra_in, extra_specs, scratch = (), (), [pltpu.VMEM((tm, d), BF16)]
    if tiles_per_seq is None:
        assert tm % seq_rows == 0
        extra_in = (state,)
        extra_specs = (pl.BlockSpec((tm // seq_rows,) + state.shape[1:], lambda i, j: (i, 0, 0)),)
        kv_shapes = (jax.ShapeDtypeStruct((t, half), BF16),
                     jax.ShapeDtypeStruct((t, tn), F32))
        kv_specs = (rows(half), rows(tn))
    else:
        scratch.append(pltpu.VMEM((8, half), F32))
        nseq = t // (tm * tiles_per_seq)
        tail_spec = pl.BlockSpec((None, half, tm), lambda i, j: (i // tiles_per_seq, 0, 0))
        kv_shapes = (jax.ShapeDtypeStruct((half, t), BF16),
                     jax.ShapeDtypeStruct((nseq, half, tm), F32),
                     jax.ShapeDtypeStruct((nseq, half, tm), F32))
        kv_specs = (pl.BlockSpec((half, tm), lambda i, j: (0, i)), tail_spec, tail_spec)
    out_shape = (
        jax.ShapeDtypeStruct((t, tn), BF16),
        *kv_shapes,
        jax.ShapeDtypeStruct((t, half), BF16),
        jax.ShapeDtypeStruct((t, half), F32),
        jax.ShapeDtypeStruct((t, 2 * tn), BF16),
    )
    return pl.pallas_call(
        functools.partial(_in_proj_kernel, tiles_per_seq, seq_rows),
        out_shape=out_shape,
        grid=(t // tm, nj),
        in_specs=[
            pl.BlockSpec((tm, d), lambda i, j: (i, 0)),
            pl.BlockSpec((1, d), lambda i, j: (0, 0)),
            pl.BlockSpec((d, tn), lambda i, j: (0, (j + nj - STEP_QK) % nj)),
            const(conv_w), const(conv_b), *extra_specs,
        ],
        out_specs=(
            rows(tn),
            *kv_specs,
            rows(half),
            rows(half),
            pl.BlockSpec((tm, tn), lambda i, j: (i, jnp.minimum(j, STEP_QK - 1))),
        ),
        scratch_shapes=scratch,
        compiler_params=_params("in_proj", 2),
    )(x, g, w_in, conv_w, conv_b, *extra_in)


def _rel_rows(rel_table):
    h, n_rel = rel_table.shape
    lo = LEFT_LEN - REL_CLIP
    hi = ATTN_NK - lo - n_rel
    first = jnp.broadcast_to(rel_table[:, :1], (h, lo))
    last = jnp.broadcast_to(rel_table[:, -1:], (h, hi))
    negative = jnp.broadcast_to(rel_table[:, :1], (h, REL_ROW - ATTN_NK))
    return jnp.concatenate([first, rel_table, last, negative], axis=1).astype(F32) * LOG2E


def _toeplitz(row, nq):
    return pltpu.roll(jnp.broadcast_to(row, (nq, row.shape[1])), 0, 1, stride=1, stride_axis=0)


def _attn_prompt_kernel(q_ref, kp_ref, kc_ref, vtp_ref, vtc_ref, rel_ref, o_ref, bias_ref):
    t = pl.program_id(1)

    @pl.when((pl.program_id(0) == 0) & (t == 0))
    def _():
        qi = lax.broadcasted_iota(jnp.int32, (ATTN_TQ, ATTN_NK), 0)
        kj = lax.broadcasted_iota(jnp.int32, (ATTN_TQ, ATTN_NK), 1)
        first = (qi // CHUNK) * CHUNK
        visible = (kj >= first) & (kj < first + BAND)
        for h in range(bias_ref.shape[0]):
            bias_ref[h] = jnp.where(visible, _toeplitz(rel_ref[h:h + 1, :], ATTN_TQ)[:, :ATTN_NK], NEG).T

    tiles = q_ref.shape[0] // ATTN_TQ
    prev_slots = kp_ref.shape[0] // ATTN_TQ

    def key_slot(s):
        k_ref, vt_ref = (kp_ref, vtp_ref) if s < prev_slots else (kc_ref, vtc_ref)
        r0 = (s % prev_slots) * ATTN_TQ
        return k_ref.at[pl.ds(r0, ATTN_TQ)], vt_ref.at[:, pl.ds(r0, ATTN_TQ)]

    for s in range(tiles):
        g = t * tiles + s
        rows = pl.ds(s * ATTN_TQ, ATTN_TQ)
        for first_slot in range(ATTN_SLOTS):
            tiles_before = ATTN_SLOTS - 1 - first_slot
            if first_slot and (tiles_before < s or (tiles_before - s) % tiles):
                continue
            cond = (g >= tiles_before) if first_slot == 0 else (g == tiles_before)

            @pl.when(cond)
            def _(s=s, rows=rows, first_slot=first_slot):
                slots = [key_slot(s + prev_slots - (ATTN_SLOTS - 1) + n) for n in range(first_slot, ATTN_SLOTS)]
                _band_heads(q_ref.at[rows], [k for k, _ in slots], [v for _, v in slots], bias_ref,
                            first_slot * ATTN_TQ, o_ref.at[rows])


def _band_heads(q_ref, k_refs, vt_refs, bias_ref, key0, o_ref):
    nk = len(k_refs) * ATTN_TQ
    heads = 2 * (q_ref.shape[1] // LANES)
    lane = lax.broadcasted_iota(jnp.int32, (ATTN_TQ, LANES), 1)
    feat = lax.broadcasted_iota(jnp.int32, (LANES, nk), 0)

    def scores(h):
        sl = slice(h // 2 * LANES, (h // 2 + 1) * LANES)
        own_lane = (lane >= HEAD_DIM) if h % 2 else (lane < HEAD_DIM)
        q_pair = q_ref[:, sl]
        qm = jnp.where(own_lane, q_pair, jnp.zeros_like(q_pair))
        keys = jnp.concatenate([r[:, sl] for r in k_refs], axis=0)
        st = lax.dot_general(keys, qm, (((1,), (1,)), ((), ())), preferred_element_type=F32)
        return st + bias_ref[h, key0:key0 + nk, :]

    def weighted_values(h, st):
        sl = slice(h // 2 * LANES, (h // 2 + 1) * LANES)
        own_feat = (feat >= HEAD_DIM) if h % 2 else (feat < HEAD_DIM)
        pt = jnp.exp2(st - jnp.max(st, axis=0, keepdims=True)).astype(BF16)
        vals_t = jnp.concatenate([r[sl, :] for r in vt_refs], axis=1)
        vm = jnp.where(own_feat, vals_t, jnp.ones_like(vals_t))
        acc = jnp.dot(vm, pt, preferred_element_type=F32)
        lo, hi = acc[:HEAD_DIM], acc[HEAD_DIM:]
        return hi / lo if h % 2 else lo / hi

    halves = []
    ahead = 3
    pending = [scores(h) for h in range(ahead)]
    for h in range(heads):
        if h + ahead < heads:
            pending.append(scores(h + ahead))
        halves.append(weighted_values(h, pending.pop(0)))
        if h % 2:
            sl = slice(h // 2 * LANES, (h // 2 + 1) * LANES)
            o_ref[:, sl] = jnp.concatenate(halves, axis=0).T.astype(o_ref.dtype)
            halves = []


def _attn_prompt(qk, v_t, rel_rows, batch, seq):
    t, width = v_t.shape[1], v_t.shape[0]
    step = LEFT_LEN
    assert seq % step == 0 and step % ATTN_TQ == 0
    nt = seq // step
    blk = (step, width)
    here = lambda b, i: b * nt + i
    before = lambda b, i: b * nt + jnp.maximum(i - 1, 0)
    return pl.pallas_call(
        _attn_prompt_kernel,
        out_shape=jax.ShapeDtypeStruct((t, width), BF16),
        grid=(batch, nt),
        in_specs=[
            pl.BlockSpec(blk, lambda b, i: (here(b, i), 0)),
            pl.BlockSpec(blk, lambda b, i: (before(b, i), 1)),
            pl.BlockSpec(blk, lambda b, i: (here(b, i), 1)),
            pl.BlockSpec((width, step), lambda b, i: (0, before(b, i))),
            pl.BlockSpec((width, step), lambda b, i: (0, here(b, i))),
            pl.BlockSpec(rel_rows.shape, lambda b, i: (0, 0)),
        ],
        out_specs=pl.BlockSpec(blk, lambda b, i: (here(b, i), 0)),
        scratch_shapes=[pltpu.VMEM((rel_rows.shape[0], ATTN_NK, ATTN_TQ), F32)],
        compiler_params=_params("attn_prompt", 2),
    )(qk, qk, qk, v_t, v_t, rel_rows)


def _attn_sample_kernel(q_ref, kn_ref, vn_ref, kct_ref, vct_ref, rel_ref, o_ref, bias_ref):
    nq = q_ref.shape[0]
    lc = kct_ref.shape[1]

    @pl.when(pl.program_id(0) == 0)
    def _():
        for h in range(bias_ref.shape[0]):
            bias_ref[h] = _toeplitz(rel_ref[h:h + 1, :], nq)[:, :bias_ref.shape[2]]

    lane = lax.broadcasted_iota(jnp.int32, (nq, LANES), 1)
    nt_dims = (((1,), (1,)), ((), ()))

    def scores(h):
        sl = slice(h // 2 * LANES, (h // 2 + 1) * LANES)
        own = (lane >= HEAD_DIM) if h % 2 else (lane < HEAD_DIM)
        qm = jnp.where(own, q_ref[:, sl], jnp.zeros((nq, LANES), BF16))
        s_cache = jnp.dot(qm, kct_ref[sl, :].astype(BF16), preferred_element_type=F32)
        s_new = lax.dot_general(qm, kn_ref[:, sl], nt_dims, preferred_element_type=F32)
        return s_cache + bias_ref[h, :, :lc], s_new + bias_ref[h, :, lc:lc + nq]

    def weighted_values(h, s_cache, s_new):
        sl = slice(h // 2 * LANES, (h // 2 + 1) * LANES)
        m = jnp.maximum(jnp.max(s_cache, axis=-1, keepdims=True), jnp.max(s_new, axis=-1, keepdims=True))
        p_cache, p_new = jnp.exp2(s_cache - m), jnp.exp2(s_new - m)
        denom = jnp.sum(p_cache, axis=-1, keepdims=True) + jnp.sum(p_new, axis=-1, keepdims=True)
        acc = lax.dot_general(p_cache.astype(BF16), vct_ref[sl, :].astype(BF16), nt_dims,
                              preferred_element_type=F32)
        acc = acc + jnp.dot(p_new.astype(BF16), vn_ref[:, sl], preferred_element_type=F32)
        return acc / denom

    heads = bias_ref.shape[0]
    ahead = 3
    pending = [scores(h) for h in range(ahead)]
    for h in range(heads):
        if h + ahead < heads:
            pending.append(scores(h + ahead))
        out = weighted_values(h, *pending.pop(0))
        if h % 2:
            sl = slice(h // 2 * LANES, (h // 2 + 1) * LANES)
            o_ref[:, sl] = jnp.where(lane < HEAD_DIM, prev_out, out).astype(o_ref.dtype)
        prev_out = out


def _attn_sample(qk, v, cache_kt, cache_vt, rel_rows, batch, seq):
    t, width = v.shape
    lc = cache_kt.shape[2]
    blk = (seq, width)
    bias_cols = -(-(lc + seq) // LANES) * LANES
    return pl.pallas_call(
        _attn_sample_kernel,
        out_shape=jax.ShapeDtypeStruct((t, width), BF16),
        grid=(batch,),
        in_specs=[
            pl.BlockSpec(blk, lambda b: (b, 0)),
            pl.BlockSpec(blk, lambda b: (b, 1)),
            pl.BlockSpec(blk, lambda b: (b, 0)),
            pl.BlockSpec((None, width, lc), lambda b: (b, 0, 0)),
            pl.BlockSpec((None, width, lc), lambda b: (b, 0, 0)),
            pl.BlockSpec(rel_rows.shape, lambda b: (0, 0)),
        ],
        out_specs=pl.BlockSpec(blk, lambda b: (b, 0)),
        scratch_shapes=[pltpu.VMEM((rel_rows.shape[0], seq, bias_cols), F32)],
        compiler_params=_params("attn_sample", 1),
    )(qk, qk, v, cache_kt, cache_vt, rel_rows)


def _mix_kernel(attn_ref, cv_ref, gates_ref, x_ref, wa_ref, wc_ref, wo_ref, g_ref, o_ref):
    d = x_ref.shape[1]
    ya = jnp.dot(attn_ref[...], wa_ref[...], preferred_element_type=F32)
    yb = jnp.dot(cv_ref[...], wc_ref[...], preferred_element_type=F32)
    merged = gates_ref[:, :d].astype(F32) * ya + gates_ref[:, d:].astype(F32) * yb
    mix = jnp.dot(merged.astype(BF16), wo_ref[...], preferred_element_type=F32)
    o_ref[...] = x_ref[...] + _rms(mix, g_ref[...])


def _mix(attn, cv, gates, x, wa, wc, wo, g, tm):
    t, d = x.shape
    row_spec = lambda w: pl.BlockSpec((tm, w), lambda i: (i, 0))
    resident = lambda a: pl.BlockSpec(a.shape, lambda i: (0,) * a.ndim, pipeline_mode=pl.Buffered(1))
    return pl.pallas_call(
        _mix_kernel,
        out_shape=jax.ShapeDtypeStruct((t, d), F32),
        grid=(t // tm,),
        in_specs=[row_spec(attn.shape[1]), row_spec(cv.shape[1]), row_spec(2 * d), row_spec(d),
                  resident(wa), resident(wc), resident(wo), resident(g)],
        out_specs=row_spec(d),
        compiler_params=_params("mix", 1),
    )(attn, cv, gates, x, wa, wc, wo, g)


def _mlp_kernel(nj, h_ref, g1_ref, wu_ref, wd_ref, g2_ref, o_ref, n_ref, acc_ref):
    j = pl.program_id(1)

    def step(first, last):
        def body():
            if first:
                n_ref[...] = _rms(h_ref[...], g1_ref[...]).astype(BF16)
            a = jnp.dot(n_ref[...], wu_ref[...], preferred_element_type=F32)
            a = jnp.square(jnp.maximum(a, 0.0)).astype(BF16)
            part = jnp.dot(a, wd_ref[...], preferred_element_type=F32)
            if last:
                f = part if first else acc_ref[...] + part
                o_ref[...] = h_ref[...] + _rms(f, g2_ref[...])
            elif first:
                acc_ref[...] = part
            else:
                acc_ref[...] += part
        return body

    if nj == 1:
        step(True, True)()
    else:
        pl.when(j == 0)(step(True, False))
        pl.when(j == nj - 1)(step(False, True))
        if nj > 2:
            pl.when((j > 0) & (j < nj - 1))(step(False, False))


def _mlp(h, g1, w_up, w_down, g2, tm):
    t, d = h.shape
    tf = MLP_TF
    nj = w_up.shape[1] // tf
    return pl.pallas_call(
        functools.partial(_mlp_kernel, nj),
        out_shape=jax.ShapeDtypeStruct((t, d), F32),
        grid=(t // tm, nj),
        in_specs=[
            pl.BlockSpec((tm, d), lambda i, j: (i, 0)),
            pl.BlockSpec((1, d), lambda i, j: (0, 0)),
            pl.BlockSpec((d, tf), lambda i, j: (0, j)),
            pl.BlockSpec((tf, d), lambda i, j: (j, 0)),
            pl.BlockSpec((1, d), lambda i, j: (0, 0)),
        ],
        out_specs=pl.BlockSpec((tm, d), lambda i, j: (i, 0)),
        scratch_shapes=[pltpu.VMEM((tm, d), BF16), pltpu.VMEM((tm, d), F32)],
        compiler_params=_params("mlp", 2),
    )(h, g1, w_up, w_down, g2)


def _pe_kernel(row_splits, h_ref, p_ref, wg_ref, wp_ref, g_ref, o_ref):
    rows = h_ref.shape[0] // row_splits

    def products(r):
        sl = slice(r * rows, (r + 1) * rows)
        return (jnp.dot(h_ref[sl, :].astype(BF16), wg_ref[...], preferred_element_type=F32),
                jnp.dot(p_ref[sl, :].astype(BF16), wp_ref[...], preferred_element_type=F32))

    pending = products(0)
    for r in range(row_splits):
        gate_pre, pe = pending
        if r + 1 < row_splits:
            pending = products(r + 1)
        sl = slice(r * rows, (r + 1) * rows)
        o_ref[sl, :] = h_ref[sl, :] + _rms(_sigmoid(gate_pre) * pe, g_ref[...])


def _pe(h, p, wg, wp, g, tm):
    t, d = h.shape
    resident = lambda a: pl.BlockSpec(a.shape, lambda i: (0,) * a.ndim, pipeline_mode=pl.Buffered(1))
    return pl.pallas_call(
        functools.partial(_pe_kernel, max(tm // 256, 1)),
        out_shape=jax.ShapeDtypeStruct((t, d), F32),
        grid=(t // tm,),
        in_specs=[pl.BlockSpec((tm, d), lambda i: (i, 0)), pl.BlockSpec((tm, p.shape[1]), lambda i: (i, 0)),
                  resident(wg), resident(wp), resident(g)],
        out_specs=pl.BlockSpec((tm, d), lambda i: (i, 0)),
        compiler_params=_params("pe", 1),
    )(h, p, wg, wp, g)


def _layer(x, pe_in, seq, tiles_per_seq, attn_fn, state, w, tm_proj, tm_mix, tm_mlp, tm_pe):
    t = x.shape[0]
    qk, *kv, cv, u, gates = _in_proj(x, w["g_pre_mix"], w["w_in"], w["conv_w"], w["conv_b"], state,
                                     min(tm_proj, t), tiles_per_seq, seq)
    attn = attn_fn(qk, kv[0])
    h = _mix(attn, cv, gates, x, w["w_attn_out"], w["w_conv_out"], w["w_o"], w["g_post_mix"], min(tm_mix, t))
    h = _mlp(h, w["g_pre_mlp"], w["w_up"], w["w_down"], w["g_post_mlp"], min(tm_mlp, t))
    h = _pe(h, pe_in, w["w_pe_gate"], w["w_pe"], w["g_pe"], min(tm_pe, t))
    return h, kv, u.reshape(t // seq, seq, -1)[:, -2:]


def kernel(x_prompt, x_sample, cache_k, cache_v, state_conv, p_prompt, p_sample, w_in, rel_table, w_attn_out,
           conv_w, conv_b, w_conv_out, w_o, g_pre_mix, g_post_mix, g_pre_mlp, g_post_mlp, w_up, w_down, w_pe,
           w_pe_gate, g_pe):
    depth = w_in.shape[0]
    bp, sp, d = x_prompt.shape
    bs, ss, _ = x_sample.shape
    lc, heads = cache_k.shape[2], cache_k.shape[3]
    lp = min(LEFT_LEN, sp)
    assert sp % lp == 0 and sp % ATTN_TQ == 0 and lc == LEFT_LEN and ss <= CHUNK
    assert rel_table.shape[2] == 2 * REL_CLIP + 1

    hp = x_prompt.reshape(bp * sp, d)
    hs = x_sample.reshape(bs * ss, d)
    outs = [[] for _ in range(6)]
    for i in range(depth):
        w = {
            "w_in": w_in[i].astype(BF16), "w_attn_out": w_attn_out[i].astype(BF16),
            "w_conv_out": w_conv_out[i].astype(BF16), "w_o": w_o[i].astype(BF16),
            "w_up": w_up[i].astype(BF16), "w_down": w_down[i].astype(BF16),
            "w_pe": w_pe[i].astype(BF16), "w_pe_gate": w_pe_gate[i].astype(BF16),
            "conv_w": conv_w[i], "conv_b": conv_b[i][None],
            "g_pre_mix": g_pre_mix[i][None], "g_post_mix": g_post_mix[i][None],
            "g_pre_mlp": g_pre_mlp[i][None], "g_post_mlp": g_post_mlp[i][None], "g_pe": g_pe[i][None],
        }
        rel_rows = _rel_rows(rel_table[i])
        kct = cache_k[i].transpose(0, 2, 3, 1).reshape(bs, heads * HEAD_DIM, lc)
        vct = cache_v[i].transpose(0, 2, 3, 1).reshape(bs, heads * HEAD_DIM, lc)

        hp, (_, ktp, vtp), cpr = _layer(
            hp, p_prompt[i].reshape(bp * sp, -1), sp, sp // lp,
            lambda qk, v_t: _attn_prompt(qk, v_t, rel_rows, bp, sp), None, w,
            tm_proj=lp, tm_mix=512, tm_mlp=512, tm_pe=512)
        hs, (_, kvs), csm = _layer(
            hs, p_sample[i].reshape(bs * ss, -1), ss, None,
            lambda qk, v: _attn_sample(qk, v, kct, vct, rel_rows, bs, ss), state_conv[i], w,
            tm_proj=512, tm_mix=256, tm_mlp=512, tm_pe=512)
        kpr = ktp.reshape(bp, heads, HEAD_DIM, lp).transpose(0, 3, 1, 2)
        vpr = vtp.reshape(bp, heads, HEAD_DIM, lp).transpose(0, 3, 1, 2)
        ksm = kvs[:, :heads * HEAD_DIM].reshape(bs, ss, heads, HEAD_DIM)
        vsm = kvs[:, heads * HEAD_DIM:].reshape(bs, ss, heads, HEAD_DIM)
        for lst, val in zip(outs, (kpr, vpr, cpr, ksm, vsm, csm)):
            lst.append(val)
    k_prompt, v_prompt, conv_prompt, k_sample, v_sample, conv_sample = [jnp.stack(o) for o in outs]
    return (hp.reshape(bp, sp, d), hs.reshape(bs, ss, d), k_prompt, v_prompt, conv_prompt,
            k_sample, v_sample, conv_sample)
```

```python
import functools

import jax
import jax.numpy as jnp
from jax import lax
from jax.experimental import pallas as pl
from jax.experimental.pallas import tpu as pltpu

F32 = jnp.float32
BF16 = jnp.bfloat16

CHUNK = 64
LEFT_CHUNKS = 8
LEFT_LEN = LEFT_CHUNKS * CHUNK
BAND = (LEFT_CHUNKS + 1) * CHUNK
HEAD_DIM = 64
REL_CLIP = 128
EPS = 1e-6
NEG = -1e30
LOG2E = 1.4426950408889634
SCORE_SCALE = HEAD_DIM ** -0.5 * LOG2E

LANES = 128
ATTN_TQ = 256
ATTN_NK = LEFT_LEN + ATTN_TQ
ATTN_SLOTS = ATTN_NK // ATTN_TQ
REL_ROW = ATTN_NK + ATTN_TQ
IN_PROJ_TN = 2048
MLP_TF = 1024
MIB = 1024 * 1024


def _rms(x, g):
    return x * lax.rsqrt(jnp.mean(x * x, axis=-1, keepdims=True) + EPS) * g


def _sigmoid(x):
    return 0.5 * jnp.tanh(0.5 * x) + 0.5


VMEM_MIB = {"in_proj": 56, "attn_prompt": 40, "attn_sample": 32, "mix": 58, "mlp": 56, "pe": 48}


def _params(call, grid_rank):
    return pltpu.CompilerParams(dimension_semantics=("arbitrary",) * grid_rank,
                                vmem_limit_bytes=VMEM_MIB[call] * MIB)


STEP_QK = 2


def _in_proj_kernel(tiles_per_seq, x_ref, g_ref, w_ref, qk_ref, *refs):
    if tiles_per_seq is None:
        v_ref, kv_ref, cb_ref, u_ref, gates_ref, n_ref = refs
    else:
        v_ref, kt_ref, vt_ref, cb_ref, u_ref, gates_ref, n_ref = refs
        is_tail = pl.program_id(0) % tiles_per_seq == tiles_per_seq - 1
    j = pl.program_id(1)
    half = w_ref.shape[1] // 2

    def proj():
        z = jnp.dot(n_ref[...], w_ref[...], preferred_element_type=F32)
        return z[:, :half], z[:, half:]

    def step(jj, tail, body):
        cond = j == jj
        if tail is not None and tiles_per_seq is not None:
            cond = cond & (is_tail if tail else jnp.logical_not(is_tail))
        pl.when(cond)(body)

    def q_k(tail):
        def body():
            q, k = proj()
            qk_ref[:, :half] = (q * SCORE_SCALE).astype(BF16)
            qk_ref[:, half:] = k.astype(BF16)
            if tiles_per_seq is None:
                kv_ref[:, :half] = k
            elif tail:
                kt_ref[...] = k.T
        return body

    def v_bg(tail):
        def body():
            v, bg = proj()
            cb_ref[...] = bg.astype(BF16)
            if tiles_per_seq is None:
                v_ref[...] = v.astype(BF16)
                kv_ref[:, half:] = v
            else:
                vt = v.T
                v_ref[...] = vt.astype(BF16)
                if tail:
                    vt_ref[...] = vt
        return body

    def conv_in():
        cg, xin = proj()
        u_ref[...] = cg * xin

    def gate(first):
        def body():
            if first:
                n_ref[...] = _rms(x_ref[...], g_ref[...]).astype(BF16)
            za, zb = proj()
            gates_ref[:, :half] = _sigmoid(za).astype(BF16)
            gates_ref[:, half:] = _sigmoid(zb).astype(BF16)
        return body

    step(0, None, gate(True))
    pl.when((j > 0) & (j < STEP_QK))(gate(False))
    for tail in ((None,) if tiles_per_seq is None else (False, True)):
        step(STEP_QK, tail, q_k(tail))
        step(STEP_QK + 1, tail, v_bg(tail))
    step(STEP_QK + 2, None, conv_in)


def _in_proj(x, g, w_in, tm, tiles_per_seq):
    t, d = x.shape
    tn = IN_PROJ_TN
    nj = w_in.shape[1] // tn
    half = tn // 2
    assert nj == 5 and d == tn and t % tm == 0

    rows = lambda w: pl.BlockSpec((tm, w), lambda i, j: (i, 0))
    if tiles_per_seq is None:
        kv_shapes = (jax.ShapeDtypeStruct((t, half), BF16),
                     jax.ShapeDtypeStruct((t, tn), F32))
        kv_specs = (rows(half), rows(tn))
    else:
        nseq = t // (tm * tiles_per_seq)
        tail_spec = pl.BlockSpec((None, half, tm), lambda i, j: (i // tiles_per_seq, 0, 0))
        kv_shapes = (jax.ShapeDtypeStruct((half, t), BF16),
                     jax.ShapeDtypeStruct((nseq, half, tm), F32),
                     jax.ShapeDtypeStruct((nseq, half, tm), F32))
        kv_specs = (pl.BlockSpec((half, tm), lambda i, j: (0, i)), tail_spec, tail_spec)
    out_shape = (
        jax.ShapeDtypeStruct((t, tn), BF16),
        *kv_shapes,
        jax.ShapeDtypeStruct((t, half), BF16),
        jax.ShapeDtypeStruct((t, half), F32),
        jax.ShapeDtypeStruct((t, 2 * tn), BF16),
    )
    return pl.pallas_call(
        functools.partial(_in_proj_kernel, tiles_per_seq),
        out_shape=out_shape,
        grid=(t // tm, nj),
        in_specs=[
            pl.BlockSpec((tm, d), lambda i, j: (i, 0)),
            pl.BlockSpec((1, d), lambda i, j: (0, 0)),
            pl.BlockSpec((d, tn), lambda i, j: (0, (j + nj - STEP_QK) % nj)),
        ],
        out_specs=(
            rows(tn),
            *kv_specs,
            rows(half),
            rows(half),
            pl.BlockSpec((tm, tn), lambda i, j: (i, jnp.minimum(j, STEP_QK - 1))),
        ),
        scratch_shapes=[pltpu.VMEM((tm, d), BF16)],
        compiler_params=_params("in_proj", 2),
    )(x, g, w_in)


def _rel_rows(rel_table):
    h, n_rel = rel_table.shape
    lo = LEFT_LEN - REL_CLIP
    hi = ATTN_NK - lo - n_rel
    first = jnp.broadcast_to(rel_table[:, :1], (h, lo))
    last = jnp.broadcast_to(rel_table[:, -1:], (h, hi))
    negative = jnp.broadcast_to(rel_table[:, :1], (h, REL_ROW - ATTN_NK))
    return jnp.concatenate([first, rel_table, last, negative], axis=1).astype(F32) * LOG2E


def _toeplitz(row, nq):
    return pltpu.roll(jnp.broadcast_to(row, (nq, row.shape[1])), 0, 1, stride=1, stride_axis=0)


def _attn_prompt_kernel(q_ref, kp_ref, kc_ref, vtp_ref, vtc_ref, rel_ref, o_ref, bias_ref):
    t = pl.program_id(1)

    @pl.when((pl.program_id(0) == 0) & (t == 0))
    def _():
        qi = lax.broadcasted_iota(jnp.int32, (ATTN_TQ, ATTN_NK), 0)
        kj = lax.broadcasted_iota(jnp.int32, (ATTN_TQ, ATTN_NK), 1)
        first = (qi // CHUNK) * CHUNK
        visible = (kj >= first) & (kj < first + BAND)
        for h in range(bias_ref.shape[0]):
            bias_ref[h] = jnp.where(visible, _toeplitz(rel_ref[h:h + 1, :], ATTN_TQ)[:, :ATTN_NK], NEG).T

    tiles = q_ref.shape[0] // ATTN_TQ
    prev_slots = kp_ref.shape[0] // ATTN_TQ

    def key_slot(s):
        k_ref, vt_ref = (kp_ref, vtp_ref) if s < prev_slots else (kc_ref, vtc_ref)
        r0 = (s % prev_slots) * ATTN_TQ
        return k_ref.at[pl.ds(r0, ATTN_TQ)], vt_ref.at[:, pl.ds(r0, ATTN_TQ)]

    for s in range(tiles):
        g = t * tiles + s
        rows = pl.ds(s * ATTN_TQ, ATTN_TQ)
        for first_slot in range(ATTN_SLOTS):
            tiles_before = ATTN_SLOTS - 1 - first_slot
            if first_slot and (tiles_before < s or (tiles_before - s) % tiles):
                continue
            cond = (g >= tiles_before) if first_slot == 0 else (g == tiles_before)

            @pl.when(cond)
            def _(s=s, rows=rows, first_slot=first_slot):
                slots = [key_slot(s + prev_slots - (ATTN_SLOTS - 1) + n) for n in range(first_slot, ATTN_SLOTS)]
                _band_heads(q_ref.at[rows], [k for k, _ in slots], [v for _, v in slots], bias_ref,
                            first_slot * ATTN_TQ, o_ref.at[rows])


def _band_heads(q_ref, k_refs, vt_refs, bias_ref, key0, o_ref):
    nk = len(k_refs) * ATTN_TQ
    heads = 2 * (q_ref.shape[1] // LANES)
    lane = lax.broadcasted_iota(jnp.int32, (ATTN_TQ, LANES), 1)
    feat = lax.broadcasted_iota(jnp.int32, (LANES, nk), 0)

    def scores(h):
        sl = slice(h // 2 * LANES, (h // 2 + 1) * LANES)
        own_lane = (lane >= HEAD_DIM) if h % 2 else (lane < HEAD_DIM)
        q_pair = q_ref[:, sl]
        qm = jnp.where(own_lane, q_pair, jnp.zeros_like(q_pair))
        keys = jnp.concatenate([r[:, sl] for r in k_refs], axis=0)
        return lax.dot_general(keys, qm, (((1,), (1,)), ((), ())), preferred_element_type=F32)

    def probabilities(h, st):
        groups = []
        for c0 in range(0, ATTN_TQ, LANES):
            first = c0 // CHUNK * CHUNK
            last = (c0 + LANES - CHUNK) // CHUNK * CHUNK + BAND
            lo, hi = max(first - key0, 0), min(last - key0, nk)
            s = st[lo:hi, c0:c0 + LANES] + bias_ref[h, key0 + lo:key0 + hi, c0:c0 + LANES]
            p = jnp.exp2(s - jnp.max(s, axis=0, keepdims=True)).astype(BF16)
            pads = [jnp.zeros((n, LANES), BF16) for n in (lo, nk - hi)]
            groups.append(jnp.concatenate([x for x in (pads[0], p, pads[1]) if x.shape[0]], axis=0))
        return jnp.concatenate(groups, axis=1)

    def weighted_values(h, st):
        sl = slice(h // 2 * LANES, (h // 2 + 1) * LANES)
        own_feat = (feat >= HEAD_DIM) if h % 2 else (feat < HEAD_DIM)
        pt = probabilities(h, st)
        vals_t = jnp.concatenate([r[sl, :] for r in vt_refs], axis=1)
        vm = jnp.where(own_feat, vals_t, jnp.ones_like(vals_t))
        acc = jnp.dot(vm, pt, preferred_element_type=F32)
        lo, hi = acc[:HEAD_DIM], acc[HEAD_DIM:]
        return hi / lo if h % 2 else lo / hi

    halves = []
    ahead = 3
    pending = [scores(h) for h in range(ahead)]
    for h in range(heads):
        if h + ahead < heads:
            pending.append(scores(h + ahead))
        halves.append(weighted_values(h, pending.pop(0)))
        if h % 2:
            sl = slice(h // 2 * LANES, (h // 2 + 1) * LANES)
            o_ref[:, sl] = jnp.concatenate(halves, axis=0).T.astype(o_ref.dtype)
            halves = []


def _attn_prompt(qk, v_t, rel_rows, batch, seq):
    t, width = v_t.shape[1], v_t.shape[0]
    step = LEFT_LEN
    assert seq % step == 0 and step % ATTN_TQ == 0
    nt = seq // step
    blk = (step, width)
    here = lambda b, i: b * nt + i
    before = lambda b, i: b * nt + jnp.maximum(i - 1, 0)
    return pl.pallas_call(
        _attn_prompt_kernel,
        out_shape=jax.ShapeDtypeStruct((t, width), BF16),
        grid=(batch, nt),
        in_specs=[
            pl.BlockSpec(blk, lambda b, i: (here(b, i), 0)),
            pl.BlockSpec(blk, lambda b, i: (before(b, i), 1)),
            pl.BlockSpec(blk, lambda b, i: (here(b, i), 1)),
            pl.BlockSpec((width, step), lambda b, i: (0, before(b, i))),
            pl.BlockSpec((width, step), lambda b, i: (0, here(b, i))),
            pl.BlockSpec(rel_rows.shape, lambda b, i: (0, 0)),
        ],
        out_specs=pl.BlockSpec(blk, lambda b, i: (here(b, i), 0)),
        scratch_shapes=[pltpu.VMEM((rel_rows.shape[0], ATTN_NK, ATTN_TQ), F32)],
        compiler_params=_params("attn_prompt", 2),
    )(qk, qk, qk, v_t, v_t, rel_rows)


def _attn_sample_kernel(q_ref, kn_ref, vn_ref, kct_ref, vct_ref, rel_ref, o_ref, bias_ref):
    nq = q_ref.shape[0]
    lc = kct_ref.shape[1]

    @pl.when(pl.program_id(0) == 0)
    def _():
        for h in range(bias_ref.shape[0]):
            bias_ref[h] = _toeplitz(rel_ref[h:h + 1, :], nq)[:, :bias_ref.shape[2]]

    lane = lax.broadcasted_iota(jnp.int32, (nq, LANES), 1)
    nt_dims = (((1,), (1,)), ((), ()))

    def scores(h):
        sl = slice(h // 2 * LANES, (h // 2 + 1) * LANES)
        own = (lane >= HEAD_DIM) if h % 2 else (lane < HEAD_DIM)
        qm = jnp.where(own, q_ref[:, sl], jnp.zeros((nq, LANES), BF16))
        s_cache = jnp.dot(qm, kct_ref[sl, :].astype(BF16), preferred_element_type=F32)
        s_new = lax.dot_general(qm, kn_ref[:, sl], nt_dims, preferred_element_type=F32)
        return s_cache + bias_ref[h, :, :lc], s_new + bias_ref[h, :, lc:lc + nq]

    def weighted_values(h, s_cache, s_new):
        sl = slice(h // 2 * LANES, (h // 2 + 1) * LANES)
        m = jnp.maximum(jnp.max(s_cache, axis=-1, keepdims=True), jnp.max(s_new, axis=-1, keepdims=True))
        p_cache, p_new = jnp.exp2(s_cache - m), jnp.exp2(s_new - m)
        denom = jnp.sum(p_cache, axis=-1, keepdims=True) + jnp.sum(p_new, axis=-1, keepdims=True)
        acc = lax.dot_general(p_cache.astype(BF16), vct_ref[sl, :].astype(BF16), nt_dims,
                              preferred_element_type=F32)
        acc = acc + jnp.dot(p_new.astype(BF16), vn_ref[:, sl], preferred_element_type=F32)
        return acc / denom

    heads = bias_ref.shape[0]
    ahead = 3
    pending = [scores(h) for h in range(ahead)]
    for h in range(heads):
        if h + ahead < heads:
            pending.append(scores(h + ahead))
        out = weighted_values(h, *pending.pop(0))
        if h % 2:
            sl = slice(h // 2 * LANES, (h // 2 + 1) * LANES)
            o_ref[:, sl] = jnp.where(lane < HEAD_DIM, prev_out, out).astype(o_ref.dtype)
        prev_out = out


def _attn_sample(qk, v, cache_kt, cache_vt, rel_rows, batch, seq):
    t, width = v.shape
    lc = cache_kt.shape[2]
    blk = (seq, width)
    bias_cols = -(-(lc + seq) // LANES) * LANES
    return pl.pallas_call(
        _attn_sample_kernel,
        out_shape=jax.ShapeDtypeStruct((t, width), BF16),
        grid=(batch,),
        in_specs=[
            pl.BlockSpec(blk, lambda b: (b, 0)),
            pl.BlockSpec(blk, lambda b: (b, 1)),
            pl.BlockSpec(blk, lambda b: (b, 0)),
            pl.BlockSpec((None, width, lc), lambda b: (b, 0, 0)),
            pl.BlockSpec((None, width, lc), lambda b: (b, 0, 0)),
            pl.BlockSpec(rel_rows.shape, lambda b: (0, 0)),
        ],
        out_specs=pl.BlockSpec(blk, lambda b: (b, 0)),
        scratch_shapes=[pltpu.VMEM((rel_rows.shape[0], seq, bias_cols), F32)],
        compiler_params=_params("attn_sample", 1),
    )(qk, qk, v, cache_kt, cache_vt, rel_rows)


def _mix_kernel(seq_rows, attn_ref, cb_ref, u_ref, halo_ref, gates_ref, x_ref, wa_ref, wc_ref, wo_ref,
                cw_ref, cbias_ref, g_ref, o_ref):
    tm, d = x_ref.shape
    u = u_ref[...]
    row = lax.broadcasted_iota(jnp.int32, u.shape, 0)
    if len(halo_ref.shape) == 3:
        nseq = halo_ref.shape[0]
        state = halo_ref[...]
        per_row = lambda r: jnp.broadcast_to(state[:, r:r + 1, :], (nseq, seq_rows, u.shape[1])).reshape(u.shape)
        prev2, prev1 = per_row(0), per_row(1)
        row = row % seq_rows
    else:
        keep = ((pl.program_id(0) * tm) % seq_rows != 0).astype(F32)
        nh = halo_ref.shape[0]
        prev2, prev1 = halo_ref[nh - 2:nh - 1, :] * keep, halo_ref[nh - 1:nh, :] * keep
    um1 = jnp.where(row == 0, prev1, pltpu.roll(u, 1, axis=0))
    um2 = jnp.where(row == 0, prev2, jnp.where(row == 1, prev1, pltpu.roll(u, 2, axis=0)))
    conv = cbias_ref[...] + cw_ref[0:1, :] * um2 + cw_ref[1:2, :] * um1 + cw_ref[2:3, :] * u
    ya = jnp.dot(attn_ref[...], wa_ref[...], preferred_element_type=F32)
    yb = jnp.dot((cb_ref[...].astype(F32) * conv).astype(BF16), wc_ref[...], preferred_element_type=F32)
    merged = gates_ref[:, :d].astype(F32) * ya + gates_ref[:, d:].astype(F32) * yb
    mix = jnp.dot(merged.astype(BF16), wo_ref[...], preferred_element_type=F32)
    o_ref[...] = x_ref[...] + _rms(mix, g_ref[...])


def _mix(attn, cb, u, state, gates, x, wa, wc, wo, conv_w, conv_b, g, tm, seq_rows):
    t, d = x.shape
    c = u.shape[1]
    halo_rows = 8
    if state is None:
        halo, halo_spec = u, pl.BlockSpec(
            (halo_rows, c), lambda i: (jnp.maximum(i * (tm // halo_rows) - 1, 0), 0))
    else:
        assert tm % seq_rows == 0
        halo, halo_spec = state, pl.BlockSpec((tm // seq_rows,) + state.shape[1:], lambda i: (i, 0, 0))
    row_spec = lambda w: pl.BlockSpec((tm, w), lambda i: (i, 0))
    resident = lambda a: pl.BlockSpec(a.shape, lambda i: (0,) * a.ndim, pipeline_mode=pl.Buffered(1))
    return pl.pallas_call(
        functools.partial(_mix_kernel, seq_rows),
        out_shape=jax.ShapeDtypeStruct((t, d), F32),
        grid=(t // tm,),
        in_specs=[row_spec(attn.shape[1]), row_spec(c), row_spec(c), halo_spec, row_spec(2 * d), row_spec(d),
                  resident(wa), resident(wc), resident(wo), resident(conv_w), resident(conv_b), resident(g)],
        out_specs=row_spec(d),
        compiler_params=_params("mix", 1),
    )(attn, cb, u, halo, gates, x, wa, wc, wo, conv_w, conv_b, g)


def _mlp_kernel(nj, h_ref, g1_ref, wu_ref, wd_ref, g2_ref, o_ref, n_ref, acc_ref):
    j = pl.program_id(1)

    def step(first, last):
        def body():
            if first:
                n_ref[...] = _rms(h_ref[...], g1_ref[...]).astype(BF16)
            a = jnp.dot(n_ref[...], wu_ref[...], preferred_element_type=F32)
            a = jnp.square(jnp.maximum(a, 0.0)).astype(BF16)
            part = jnp.dot(a, wd_ref[...], preferred_element_type=F32)
            if last:
                f = part if first else acc_ref[...] + part
                o_ref[...] = h_ref[...] + _rms(f, g2_ref[...])
            elif first:
                acc_ref[...] = part
            else:
                acc_ref[...] += part
        return body

    if nj == 1:
        step(True, True)()
    else:
        pl.when(j == 0)(step(True, False))
        pl.when(j == nj - 1)(step(False, True))
        if nj > 2:
            pl.when((j > 0) & (j < nj - 1))(step(False, False))


def _mlp(h, g1, w_up, w_down, g2, tm):
    t, d = h.shape
    tf = MLP_TF
    nj = w_up.shape[1] // tf
    return pl.pallas_call(
        functools.partial(_mlp_kernel, nj),
        out_shape=jax.ShapeDtypeStruct((t, d), F32),
        grid=(t // tm, nj),
        in_specs=[
            pl.BlockSpec((tm, d), lambda i, j: (i, 0)),
            pl.BlockSpec((1, d), lambda i, j: (0, 0)),
            pl.BlockSpec((d, tf), lambda i, j: (0, j)),
            pl.BlockSpec((tf, d), lambda i, j: (j, 0)),
            pl.BlockSpec((1, d), lambda i, j: (0, 0)),
        ],
        out_specs=pl.BlockSpec((tm, d), lambda i, j: (i, 0)),
        scratch_shapes=[pltpu.VMEM((tm, d), BF16), pltpu.VMEM((tm, d), F32)],
        compiler_params=_params("mlp", 2),
    )(h, g1, w_up, w_down, g2)


def _pe_kernel(row_splits, h_ref, p_ref, wg_ref, wp_ref, g_ref, o_ref):
    rows = h_ref.shape[0] // row_splits

    def products(r):
        sl = slice(r * rows, (r + 1) * rows)
        return (jnp.dot(h_ref[sl, :].astype(BF16), wg_ref[...], preferred_element_type=F32),
                jnp.dot(p_ref[sl, :].astype(BF16), wp_ref[...], preferred_element_type=F32))

    pending = products(0)
    for r in range(row_splits):
        gate_pre, pe = pending
        if r + 1 < row_splits:
            pending = products(r + 1)
        sl = slice(r * rows, (r + 1) * rows)
        o_ref[sl, :] = h_ref[sl, :] + _rms(_sigmoid(gate_pre) * pe, g_ref[...])


def _pe(h, p, wg, wp, g, tm):
    t, d = h.shape
    resident = lambda a: pl.BlockSpec(a.shape, lambda i: (0,) * a.ndim, pipeline_mode=pl.Buffered(1))
    return pl.pallas_call(
        functools.partial(_pe_kernel, max(tm // 256, 1)),
        out_shape=jax.ShapeDtypeStruct((t, d), F32),
        grid=(t // tm,),
        in_specs=[pl.BlockSpec((tm, d), lambda i: (i, 0)), pl.BlockSpec((tm, p.shape[1]), lambda i: (i, 0)),
                  resident(wg), resident(wp), resident(g)],
        out_specs=pl.BlockSpec((tm, d), lambda i: (i, 0)),
        compiler_params=_params("pe", 1),
    )(h, p, wg, wp, g)


def _layer(x, pe_in, seq, tiles_per_seq, attn_fn, state, w, tm_proj, tm_mix, tm_mlp, tm_pe):
    t = x.shape[0]
    qk, *kv, cb, u, gates = _in_proj(x, w["g_pre_mix"], w["w_in"], min(tm_proj, t), tiles_per_seq)
    attn = attn_fn(qk, kv[0])
    h = _mix(attn, cb, u, state, gates, x, w["w_attn_out"], w["w_conv_out"], w["w_o"], w["conv_w"],
             w["conv_b"], w["g_post_mix"], min(tm_mix, t), seq)
    h = _mlp(h, w["g_pre_mlp"], w["w_up"], w["w_down"], w["g_post_mlp"], min(tm_mlp, t))
    h = _pe(h, pe_in, w["w_pe_gate"], w["w_pe"], w["g_pe"], min(tm_pe, t))
    return h, kv, u.reshape(t // seq, seq, -1)[:, -2:]


def kernel(x_prompt, x_sample, cache_k, cache_v, state_conv, p_prompt, p_sample, w_in, rel_table, w_attn_out,
           conv_w, conv_b, w_conv_out, w_o, g_pre_mix, g_post_mix, g_pre_mlp, g_post_mlp, w_up, w_down, w_pe,
           w_pe_gate, g_pe):
    depth = w_in.shape[0]
    bp, sp, d = x_prompt.shape
    bs, ss, _ = x_sample.shape
    lc, heads = cache_k.shape[2], cache_k.shape[3]
    lp = min(LEFT_LEN, sp)
    assert sp % lp == 0 and sp % ATTN_TQ == 0 and lc == LEFT_LEN and ss <= CHUNK
    assert rel_table.shape[2] == 2 * REL_CLIP + 1

    hp = x_prompt.reshape(bp * sp, d)
    hs = x_sample.reshape(bs * ss, d)
    outs = [[] for _ in range(6)]
    for i in range(depth):
        w = {
            "w_in": w_in[i].astype(BF16), "w_attn_out": w_attn_out[i].astype(BF16),
            "w_conv_out": w_conv_out[i].astype(BF16), "w_o": w_o[i].astype(BF16),
            "w_up": w_up[i].astype(BF16), "w_down": w_down[i].astype(BF16),
            "w_pe": w_pe[i].astype(BF16), "w_pe_gate": w_pe_gate[i].astype(BF16),
            "conv_w": conv_w[i], "conv_b": conv_b[i][None],
            "g_pre_mix": g_pre_mix[i][None], "g_post_mix": g_post_mix[i][None],
            "g_pre_mlp": g_pre_mlp[i][None], "g_post_mlp": g_post_mlp[i][None], "g_pe": g_pe[i][None],
        }
        rel_rows = _rel_rows(rel_table[i])
        kct = cache_k[i].transpose(0, 2, 3, 1).reshape(bs, heads * HEAD_DIM, lc)
        vct = cache_v[i].transpose(0, 2, 3, 1).reshape(bs, heads * HEAD_DIM, lc)

        hp, (_, ktp, vtp), cpr = _layer(
            hp, p_prompt[i].reshape(bp * sp, -1), sp, sp // lp,
            lambda qk, v_t: _attn_prompt(qk, v_t, rel_rows, bp, sp), None, w,
            tm_proj=lp, tm_mix=512, tm_mlp=512, tm_pe=512)
        hs, (_, kvs), csm = _layer(
            hs, p_sample[i].reshape(bs * ss, -1), ss, None,
            lambda qk, v: _attn_sample(qk, v, kct, vct, rel_rows, bs, ss), state_conv[i], w,
            tm_proj=512, tm_mix=256, tm_mlp=512, tm_pe=512)
        kpr = ktp.reshape(bp, heads, HEAD_DIM, lp).transpose(0, 3, 1, 2)
        vpr = vtp.reshape(bp, heads, HEAD_DIM, lp).transpose(0, 3, 1, 2)
        ksm = kvs[:, :heads * HEAD_DIM].reshape(bs, ss, heads, HEAD_DIM)
        vsm = kvs[:, heads * HEAD_DIM:].reshape(bs, ss, heads, HEAD_DIM)
        for lst, val in zip(outs, (kpr, vpr, cpr, ksm, vsm, csm)):
            lst.append(val)
    k_prompt, v_prompt, conv_prompt, k_sample, v_sample, conv_sample = [jnp.stack(o) for o in outs]
    return (hp.reshape(bp, sp, d), hs.reshape(bs, ss, d), k_prompt, v_prompt, conv_prompt,
            k_sample, v_sample, conv_sample)
```

```python
import functools

import jax
import jax.numpy as jnp
from jax import lax
from jax.experimental import pallas as pl
from jax.experimental.pallas import tpu as pltpu

F32 = jnp.float32
BF16 = jnp.bfloat16

CHUNK = 64
LEFT_CHUNKS = 8
LEFT_LEN = LEFT_CHUNKS * CHUNK
BAND = (LEFT_CHUNKS + 1) * CHUNK
HEAD_DIM = 64
REL_CLIP = 128
EPS = 1e-6
NEG = -1e30
LOG2E = 1.4426950408889634
SCORE_SCALE = HEAD_DIM ** -0.5 * LOG2E

LANES = 128
ATTN_TQ = 256
ATTN_NK = LEFT_LEN + ATTN_TQ
ATTN_SLOTS = ATTN_NK // ATTN_TQ
REL_ROW = ATTN_NK + ATTN_TQ
IN_PROJ_TN = 2048
MLP_TF = 1024
MIB = 1024 * 1024


def _rms(x, g):
    return x * lax.rsqrt(jnp.mean(x * x, axis=-1, keepdims=True) + EPS) * g


def _sigmoid(x):
    return 0.5 * jnp.tanh(0.5 * x) + 0.5


VMEM_MIB = {"in_proj": 56, "attn_prompt": 40, "attn_sample": 32, "mix": 58, "mlp": 56, "pe": 48}


def _params(call, grid_rank):
    return pltpu.CompilerParams(dimension_semantics=("arbitrary",) * grid_rank,
                                vmem_limit_bytes=VMEM_MIB[call] * MIB)


STEP_QK = 2


def _in_proj_kernel(tiles_per_seq, x_ref, g_ref, w_ref, qk_ref, *refs):
    if tiles_per_seq is None:
        v_ref, kv_ref, cb_ref, u_ref, gates_ref, n_ref = refs
    else:
        v_ref, kt_ref, vt_ref, cb_ref, u_ref, gates_ref, n_ref = refs
        is_tail = pl.program_id(0) % tiles_per_seq == tiles_per_seq - 1
    j = pl.program_id(1)
    half = w_ref.shape[1] // 2

    def proj():
        z = jnp.dot(n_ref[...], w_ref[...], preferred_element_type=F32)
        return z[:, :half], z[:, half:]

    def step(jj, tail, body):
        cond = j == jj
        if tail is not None and tiles_per_seq is not None:
            cond = cond & (is_tail if tail else jnp.logical_not(is_tail))
        pl.when(cond)(body)

    def q_k(tail):
        def body():
            q, k = proj()
            qk_ref[:, :half] = (q * SCORE_SCALE).astype(BF16)
            qk_ref[:, half:] = k.astype(BF16)
            if tiles_per_seq is None:
                kv_ref[:, :half] = k
            elif tail:
                kt_ref[...] = k.T
        return body

    def v_bg(tail):
        def body():
            v, bg = proj()
            cb_ref[...] = bg.astype(BF16)
            if tiles_per_seq is None:
                v_ref[...] = v.astype(BF16)
                kv_ref[:, half:] = v
            else:
                vt = v.T
                v_ref[...] = vt.astype(BF16)
                if tail:
                    vt_ref[...] = vt
        return body

    def conv_in():
        cg, xin = proj()
        u_ref[...] = cg * xin

    def gate(first):
        def body():
            if first:
                n_ref[...] = _rms(x_ref[...], g_ref[...]).astype(BF16)
            za, zb = proj()
            gates_ref[:, :half] = _sigmoid(za).astype(BF16)
            gates_ref[:, half:] = _sigmoid(zb).astype(BF16)
        return body

    step(0, None, gate(True))
    pl.when((j > 0) & (j < STEP_QK))(gate(False))
    for tail in ((None,) if tiles_per_seq is None else (False, True)):
        step(STEP_QK, tail, q_k(tail))
        step(STEP_QK + 1, tail, v_bg(tail))
    step(STEP_QK + 2, None, conv_in)


def _in_proj(x, g, w_in, tm, tiles_per_seq):
    t, d = x.shape
    tn = IN_PROJ_TN
    nj = w_in.shape[1] // tn
    half = tn // 2
    assert nj == 5 and d == tn and t % tm == 0

    rows = lambda w: pl.BlockSpec((tm, w), lambda i, j: (i, 0))
    if tiles_per_seq is None:
        kv_shapes = (jax.ShapeDtypeStruct((t, half), BF16),
                     jax.ShapeDtypeStruct((t, tn), F32))
        kv_specs = (rows(half), rows(tn))
    else:
        nseq = t // (tm * tiles_per_seq)
        tail_spec = pl.BlockSpec((None, half, tm), lambda i, j: (i // tiles_per_seq, 0, 0))
        kv_shapes = (jax.ShapeDtypeStruct((half, t), BF16),
                     jax.ShapeDtypeStruct((nseq, half, tm), F32),
                     jax.ShapeDtypeStruct((nseq, half, tm), F32))
        kv_specs = (pl.BlockSpec((half, tm), lambda i, j: (0, i)), tail_spec, tail_spec)
    out_shape = (
        jax.ShapeDtypeStruct((t, tn), BF16),
        *kv_shapes,
        jax.ShapeDtypeStruct((t, half), BF16),
        jax.ShapeDtypeStruct((t, half), F32),
        jax.ShapeDtypeStruct((t, 2 * tn), BF16),
    )
    return pl.pallas_call(
        functools.partial(_in_proj_kernel, tiles_per_seq),
        out_shape=out_shape,
        grid=(t // tm, nj),
        in_specs=[
            pl.BlockSpec((tm, d), lambda i, j: (i, 0)),
            pl.BlockSpec((1, d), lambda i, j: (0, 0)),
            pl.BlockSpec((d, tn), lambda i, j: (0, (j + nj - STEP_QK) % nj)),
        ],
        out_specs=(
            rows(tn),
            *kv_specs,
            rows(half),
            rows(half),
            pl.BlockSpec((tm, tn), lambda i, j: (i, jnp.minimum(j, STEP_QK - 1))),
        ),
        scratch_shapes=[pltpu.VMEM((tm, d), BF16)],
        compiler_params=_params("in_proj", 2),
    )(x, g, w_in)


def _rel_rows(rel_table):
    h, n_rel = rel_table.shape
    lo = LEFT_LEN - REL_CLIP
    hi = ATTN_NK - lo - n_rel
    first = jnp.broadcast_to(rel_table[:, :1], (h, lo))
    last = jnp.broadcast_to(rel_table[:, -1:], (h, hi))
    negative = jnp.broadcast_to(rel_table[:, :1], (h, REL_ROW - ATTN_NK))
    rows = jnp.concatenate([first, rel_table, last, negative], axis=1).astype(F32)
    return (rows - rows[:, :1]) * LOG2E


def _toeplitz(row, nq):
    return pltpu.roll(jnp.broadcast_to(row, (nq, row.shape[1])), 0, 1, stride=1, stride_axis=0)


def _attn_prompt_kernel(q_ref, kp_ref, kc_ref, vtp_ref, vtc_ref, rel_ref, o_ref, bias_ref):
    t = pl.program_id(1)

    @pl.when((pl.program_id(0) == 0) & (t == 0))
    def _():
        qi = lax.broadcasted_iota(jnp.int32, (ATTN_TQ, ATTN_NK), 0)
        kj = lax.broadcasted_iota(jnp.int32, (ATTN_TQ, ATTN_NK), 1)
        first = (qi // CHUNK) * CHUNK
        visible = (kj >= first) & (kj < first + BAND)
        for h in range(bias_ref.shape[0]):
            bias_ref[h] = jnp.where(visible, _toeplitz(rel_ref[h:h + 1, :], ATTN_TQ)[:, :ATTN_NK], NEG).T

    tiles = q_ref.shape[0] // ATTN_TQ
    prev_slots = kp_ref.shape[0] // ATTN_TQ

    def key_slot(s):
        k_ref, vt_ref = (kp_ref, vtp_ref) if s < prev_slots else (kc_ref, vtc_ref)
        r0 = (s % prev_slots) * ATTN_TQ
        return k_ref.at[pl.ds(r0, ATTN_TQ)], vt_ref.at[:, pl.ds(r0, ATTN_TQ)]

    for s in range(tiles):
        g = t * tiles + s
        rows = pl.ds(s * ATTN_TQ, ATTN_TQ)
        for first_slot in range(ATTN_SLOTS):
            tiles_before = ATTN_SLOTS - 1 - first_slot
            if first_slot and (tiles_before < s or (tiles_before - s) % tiles):
                continue
            cond = (g >= tiles_before) if first_slot == 0 else (g == tiles_before)

            @pl.when(cond)
            def _(s=s, rows=rows, first_slot=first_slot):
                slots = [key_slot(s + prev_slots - (ATTN_SLOTS - 1) + n) for n in range(first_slot, ATTN_SLOTS)]
                _band_heads(q_ref.at[rows], [k for k, _ in slots], [v for _, v in slots], bias_ref,
                            first_slot * ATTN_TQ, o_ref.at[rows])


def _band_heads(q_ref, k_refs, vt_refs, bias_ref, key0, o_ref):
    nk = len(k_refs) * ATTN_TQ
    heads = 2 * (q_ref.shape[1] // LANES)
    lane = lax.broadcasted_iota(jnp.int32, (ATTN_TQ, LANES), 1)
    feat = lax.broadcasted_iota(jnp.int32, (LANES, nk), 0)

    def scores(h):
        sl = slice(h // 2 * LANES, (h // 2 + 1) * LANES)
        own_lane = (lane >= HEAD_DIM) if h % 2 else (lane < HEAD_DIM)
        q_pair = q_ref[:, sl]
        qm = jnp.where(own_lane, q_pair, jnp.zeros_like(q_pair))
        keys = jnp.concatenate([r[:, sl] for r in k_refs], axis=0)
        return lax.dot_general(keys, qm, (((1,), (1,)), ((), ())), preferred_element_type=F32)

    def probabilities(h, st):
        groups = []
        for c0 in range(0, ATTN_TQ, LANES):
            first = c0 // CHUNK * CHUNK
            last = (c0 + LANES - CHUNK) // CHUNK * CHUNK + BAND
            lo, hi = max(first - key0, 0), min(last - key0, nk)
            z0 = min(max(first + CHUNK - key0, lo), hi)
            z1 = min(max(c0 + LEFT_LEN - REL_CLIP - key0, z0), hi)
            cols = slice(c0, c0 + LANES)
            biased = lambda a, b: st[a:b, cols] + bias_ref[h, key0 + a:key0 + b, cols]
            parts = ([biased(lo, z0)] if z0 > lo else []) + ([st[z0:z1, cols]] if z1 > z0 else []) \
                + ([biased(z1, hi)] if hi > z1 else [])
            s = jnp.concatenate(parts, axis=0)
            p = jnp.exp2(s - jnp.max(s, axis=0, keepdims=True)).astype(BF16)
            pads = [jnp.zeros((n, LANES), BF16) for n in (lo, nk - hi)]
            groups.append(jnp.concatenate([x for x in (pads[0], p, pads[1]) if x.shape[0]], axis=0))
        return jnp.concatenate(groups, axis=1)

    def weighted_values(h, st):
        sl = slice(h // 2 * LANES, (h // 2 + 1) * LANES)
        own_feat = (feat >= HEAD_DIM) if h % 2 else (feat < HEAD_DIM)
        pt = probabilities(h, st)
        vals_t = jnp.concatenate([r[sl, :] for r in vt_refs], axis=1)
        vm = jnp.where(own_feat, vals_t, jnp.ones_like(vals_t))
        acc = jnp.dot(vm, pt, preferred_element_type=F32)
        lo, hi = acc[:HEAD_DIM], acc[HEAD_DIM:]
        return hi / lo if h % 2 else lo / hi

    halves = []
    ahead = 3
    pending = [scores(h) for h in range(ahead)]
    for h in range(heads):
        if h + ahead < heads:
            pending.append(scores(h + ahead))
        halves.append(weighted_values(h, pending.pop(0)))
        if h % 2:
            sl = slice(h // 2 * LANES, (h // 2 + 1) * LANES)
            o_ref[:, sl] = jnp.concatenate(halves, axis=0).T.astype(o_ref.dtype)
            halves = []


def _attn_prompt(qk, v_t, rel_rows, batch, seq):
    t, width = v_t.shape[1], v_t.shape[0]
    step = LEFT_LEN
    assert seq % step == 0 and step % ATTN_TQ == 0
    nt = seq // step
    blk = (step, width)
    here = lambda b, i: b * nt + i
    before = lambda b, i: b * nt + jnp.maximum(i - 1, 0)
    return pl.pallas_call(
        _attn_prompt_kernel,
        out_shape=jax.ShapeDtypeStruct((t, width), BF16),
        grid=(batch, nt),
        in_specs=[
            pl.BlockSpec(blk, lambda b, i: (here(b, i), 0)),
            pl.BlockSpec(blk, lambda b, i: (before(b, i), 1)),
            pl.BlockSpec(blk, lambda b, i: (here(b, i), 1)),
            pl.BlockSpec((width, step), lambda b, i: (0, before(b, i))),
            pl.BlockSpec((width, step), lambda b, i: (0, here(b, i))),
            pl.BlockSpec(rel_rows.shape, lambda b, i: (0, 0)),
        ],
        out_specs=pl.BlockSpec(blk, lambda b, i: (here(b, i), 0)),
        scratch_shapes=[pltpu.VMEM((rel_rows.shape[0], ATTN_NK, ATTN_TQ), F32)],
        compiler_params=_params("attn_prompt", 2),
    )(qk, qk, qk, v_t, v_t, rel_rows)


def _attn_sample_kernel(q_ref, kn_ref, vn_ref, kct_ref, vct_ref, rel_ref, o_ref, bias_ref):
    nq = q_ref.shape[0]
    lc = kct_ref.shape[1]

    @pl.when(pl.program_id(0) == 0)
    def _():
        for h in range(bias_ref.shape[0]):
            bias_ref[h] = _toeplitz(rel_ref[h:h + 1, :], nq)[:, :bias_ref.shape[2]]

    lane = lax.broadcasted_iota(jnp.int32, (nq, LANES), 1)
    nt_dims = (((1,), (1,)), ((), ()))

    def scores(h):
        sl = slice(h // 2 * LANES, (h // 2 + 1) * LANES)
        own = (lane >= HEAD_DIM) if h % 2 else (lane < HEAD_DIM)
        qm = jnp.where(own, q_ref[:, sl], jnp.zeros((nq, LANES), BF16))
        s_cache = jnp.dot(qm, kct_ref[sl, :].astype(BF16), preferred_element_type=F32)
        s_new = lax.dot_general(qm, kn_ref[:, sl], nt_dims, preferred_element_type=F32)
        return s_cache + bias_ref[h, :, :lc], s_new + bias_ref[h, :, lc:lc + nq]

    def weighted_values(h, s_cache, s_new):
        sl = slice(h // 2 * LANES, (h // 2 + 1) * LANES)
        m = jnp.maximum(jnp.max(s_cache, axis=-1, keepdims=True), jnp.max(s_new, axis=-1, keepdims=True))
        p_cache, p_new = jnp.exp2(s_cache - m), jnp.exp2(s_new - m)
        denom = jnp.sum(p_cache, axis=-1, keepdims=True) + jnp.sum(p_new, axis=-1, keepdims=True)
        acc = lax.dot_general(p_cache.astype(BF16), vct_ref[sl, :].astype(BF16), nt_dims,
                              preferred_element_type=F32)
        acc = acc + jnp.dot(p_new.astype(BF16), vn_ref[:, sl], preferred_element_type=F32)
        return acc / denom

    heads = bias_ref.shape[0]
    ahead = 3
    pending = [scores(h) for h in range(ahead)]
    for h in range(heads):
        if h + ahead < heads:
            pending.append(scores(h + ahead))
        out = weighted_values(h, *pending.pop(0))
        if h % 2:
            sl = slice(h // 2 * LANES, (h // 2 + 1) * LANES)
            o_ref[:, sl] = jnp.where(lane < HEAD_DIM, prev_out, out).astype(o_ref.dtype)
        prev_out = out


def _attn_sample(qk, v, cache_kt, cache_vt, rel_rows, batch, seq):
    t, width = v.shape
    lc = cache_kt.shape[2]
    blk = (seq, width)
    bias_cols = -(-(lc + seq) // LANES) * LANES
    return pl.pallas_call(
        _attn_sample_kernel,
        out_shape=jax.ShapeDtypeStruct((t, width), BF16),
        grid=(batch,),
        in_specs=[
            pl.BlockSpec(blk, lambda b: (b, 0)),
            pl.BlockSpec(blk, lambda b: (b, 1)),
            pl.BlockSpec(blk, lambda b: (b, 0)),
            pl.BlockSpec((None, width, lc), lambda b: (b, 0, 0)),
            pl.BlockSpec((None, width, lc), lambda b: (b, 0, 0)),
            pl.BlockSpec(rel_rows.shape, lambda b: (0, 0)),
        ],
        out_specs=pl.BlockSpec(blk, lambda b: (b, 0)),
        scratch_shapes=[pltpu.VMEM((rel_rows.shape[0], seq, bias_cols), F32)],
        compiler_params=_params("attn_sample", 1),
    )(qk, qk, v, cache_kt, cache_vt, rel_rows)


def _mix_kernel(seq_rows, attn_ref, cb_ref, u_ref, halo_ref, gates_ref, x_ref, wa_ref, wc_ref, wo_ref,
                cw_ref, cbias_ref, g_ref, o_ref):
    tm, d = x_ref.shape
    u = u_ref[...]
    row = lax.broadcasted_iota(jnp.int32, u.shape, 0)
    if len(halo_ref.shape) == 3:
        nseq = halo_ref.shape[0]
        state = halo_ref[...]
        per_row = lambda r: jnp.broadcast_to(state[:, r:r + 1, :], (nseq, seq_rows, u.shape[1])).reshape(u.shape)
        prev2, prev1 = per_row(0), per_row(1)
        row = row % seq_rows
    else:
        keep = ((pl.program_id(0) * tm) % seq_rows != 0).astype(F32)
        nh = halo_ref.shape[0]
        prev2, prev1 = halo_ref[nh - 2:nh - 1, :] * keep, halo_ref[nh - 1:nh, :] * keep
    um1 = jnp.where(row == 0, prev1, pltpu.roll(u, 1, axis=0))
    um2 = jnp.where(row == 0, prev2, jnp.where(row == 1, prev1, pltpu.roll(u, 2, axis=0)))
    conv = cbias_ref[...] + cw_ref[0:1, :] * um2 + cw_ref[1:2, :] * um1 + cw_ref[2:3, :] * u
    ya = jnp.dot(attn_ref[...], wa_ref[...], preferred_element_type=F32)
    yb = jnp.dot((cb_ref[...].astype(F32) * conv).astype(BF16), wc_ref[...], preferred_element_type=F32)
    merged = gates_ref[:, :d].astype(F32) * ya + gates_ref[:, d:].astype(F32) * yb
    mix = jnp.dot(merged.astype(BF16), wo_ref[...], preferred_element_type=F32)
    o_ref[...] = x_ref[...] + _rms(mix, g_ref[...])


def _mix(attn, cb, u, state, gates, x, wa, wc, wo, conv_w, conv_b, g, tm, seq_rows):
    t, d = x.shape
    c = u.shape[1]
    halo_rows = 8
    if state is None:
        halo, halo_spec = u, pl.BlockSpec(
            (halo_rows, c), lambda i: (jnp.maximum(i * (tm // halo_rows) - 1, 0), 0))
    else:
        assert tm % seq_rows == 0
        halo, halo_spec = state, pl.BlockSpec((tm // seq_rows,) + state.shape[1:], lambda i: (i, 0, 0))
    row_spec = lambda w: pl.BlockSpec((tm, w), lambda i: (i, 0))
    resident = lambda a: pl.BlockSpec(a.shape, lambda i: (0,) * a.ndim, pipeline_mode=pl.Buffered(1))
    return pl.pallas_call(
        functools.partial(_mix_kernel, seq_rows),
        out_shape=jax.ShapeDtypeStruct((t, d), F32),
        grid=(t // tm,),
        in_specs=[row_spec(attn.shape[1]), row_spec(c), row_spec(c), halo_spec, row_spec(2 * d), row_spec(d),
                  resident(wa), resident(wc), resident(wo), resident(conv_w), resident(conv_b), resident(g)],
        out_specs=row_spec(d),
        compiler_params=_params("mix", 1),
    )(attn, cb, u, halo, gates, x, wa, wc, wo, conv_w, conv_b, g)


def _mlp_kernel(nj, h_ref, g1_ref, wu_ref, wd_ref, g2_ref, o_ref, n_ref, acc_ref):
    j = pl.program_id(1)

    def step(first, last):
        def body():
            if first:
                n_ref[...] = _rms(h_ref[...], g1_ref[...]).astype(BF16)
            a = jnp.dot(n_ref[...], wu_ref[...], preferred_element_type=F32)
            a = jnp.square(jnp.maximum(a, 0.0)).astype(BF16)
            part = jnp.dot(a, wd_ref[...], preferred_element_type=F32)
            if last:
                f = part if first else acc_ref[...] + part
                o_ref[...] = h_ref[...] + _rms(f, g2_ref[...])
            elif first:
                acc_ref[...] = part
            else:
                acc_ref[...] += part
        return body

    if nj == 1:
        step(True, True)()
    else:
        pl.when(j == 0)(step(True, False))
        pl.when(j == nj - 1)(step(False, True))
        if nj > 2:
            pl.when((j > 0) & (j < nj - 1))(step(False, False))


def _mlp(h, g1, w_up, w_down, g2, tm):
    t, d = h.shape
    tf = MLP_TF
    nj = w_up.shape[1] // tf
    return pl.pallas_call(
        functools.partial(_mlp_kernel, nj),
        out_shape=jax.ShapeDtypeStruct((t, d), F32),
        grid=(t // tm, nj),
        in_specs=[
            pl.BlockSpec((tm, d), lambda i, j: (i, 0)),
            pl.BlockSpec((1, d), lambda i, j: (0, 0)),
            pl.BlockSpec((d, tf), lambda i, j: (0, j)),
            pl.BlockSpec((tf, d), lambda i, j: (j, 0)),
            pl.BlockSpec((1, d), lambda i, j: (0, 0)),
        ],
        out_specs=pl.BlockSpec((tm, d), lambda i, j: (i, 0)),
        scratch_shapes=[pltpu.VMEM((tm, d), BF16), pltpu.VMEM((tm, d), F32)],
        compiler_params=_params("mlp", 2),
    )(h, g1, w_up, w_down, g2)


def _pe_kernel(row_splits, h_ref, p_ref, wg_ref, wp_ref, g_ref, o_ref):
    rows = h_ref.shape[0] // row_splits

    def products(r):
        sl = slice(r * rows, (r + 1) * rows)
        return (jnp.dot(h_ref[sl, :].astype(BF16), wg_ref[...], preferred_element_type=F32),
                jnp.dot(p_ref[sl, :].astype(BF16), wp_ref[...], preferred_element_type=F32))

    pending = products(0)
    for r in range(row_splits):
        gate_pre, pe = pending
        if r + 1 < row_splits:
            pending = products(r + 1)
        sl = slice(r * rows, (r + 1) * rows)
        o_ref[sl, :] = h_ref[sl, :] + _rms(_sigmoid(gate_pre) * pe, g_ref[...])


def _pe(h, p, wg, wp, g, tm):
    t, d = h.shape
    resident = lambda a: pl.BlockSpec(a.shape, lambda i: (0,) * a.ndim, pipeline_mode=pl.Buffered(1))
    return pl.pallas_call(
        functools.partial(_pe_kernel, max(tm // 256, 1)),
        out_shape=jax.ShapeDtypeStruct((t, d), F32),
        grid=(t // tm,),
        in_specs=[pl.BlockSpec((tm, d), lambda i: (i, 0)), pl.BlockSpec((tm, p.shape[1]), lambda i: (i, 0)),
                  resident(wg), resident(wp), resident(g)],
        out_specs=pl.BlockSpec((tm, d), lambda i: (i, 0)),
        compiler_params=_params("pe", 1),
    )(h, p, wg, wp, g)


def _layer(x, pe_in, seq, tiles_per_seq, attn_fn, state, w, tm_proj, tm_mix, tm_mlp, tm_pe):
    t = x.shape[0]
    qk, *kv, cb, u, gates = _in_proj(x, w["g_pre_mix"], w["w_in"], min(tm_proj, t), tiles_per_seq)
    attn = attn_fn(qk, kv[0])
    h = _mix(attn, cb, u, state, gates, x, w["w_attn_out"], w["w_conv_out"], w["w_o"], w["conv_w"],
             w["conv_b"], w["g_post_mix"], min(tm_mix, t), seq)
    h = _mlp(h, w["g_pre_mlp"], w["w_up"], w["w_down"], w["g_post_mlp"], min(tm_mlp, t))
    h = _pe(h, pe_in, w["w_pe_gate"], w["w_pe"], w["g_pe"], min(tm_pe, t))
    return h, kv, u.reshape(t // seq, seq, -1)[:, -2:]


def kernel(x_prompt, x_sample, cache_k, cache_v, state_conv, p_prompt, p_sample, w_in, rel_table, w_attn_out,
           conv_w, conv_b, w_conv_out, w_o, g_pre_mix, g_post_mix, g_pre_mlp, g_post_mlp, w_up, w_down, w_pe,
           w_pe_gate, g_pe):
    depth = w_in.shape[0]
    bp, sp, d = x_prompt.shape
    bs, ss, _ = x_sample.shape
    lc, heads = cache_k.shape[2], cache_k.shape[3]
    lp = min(LEFT_LEN, sp)
    assert sp % lp == 0 and sp % ATTN_TQ == 0 and lc == LEFT_LEN and ss <= CHUNK
    assert rel_table.shape[2] == 2 * REL_CLIP + 1

    hp = x_prompt.reshape(bp * sp, d)
    hs = x_sample.reshape(bs * ss, d)
    outs = [[] for _ in range(6)]
    for i in range(depth):
        w = {
            "w_in": w_in[i].astype(BF16), "w_attn_out": w_attn_out[i].astype(BF16),
            "w_conv_out": w_conv_out[i].astype(BF16), "w_o": w_o[i].astype(BF16),
            "w_up": w_up[i].astype(BF16), "w_down": w_down[i].astype(BF16),
            "w_pe": w_pe[i].astype(BF16), "w_pe_gate": w_pe_gate[i].astype(BF16),
            "conv_w": conv_w[i], "conv_b": conv_b[i][None],
            "g_pre_mix": g_pre_mix[i][None], "g_post_mix": g_post_mix[i][None],
            "g_pre_mlp": g_pre_mlp[i][None], "g_post_mlp": g_post_mlp[i][None], "g_pe": g_pe[i][None],
        }
        rel_rows = _rel_rows(rel_table[i])
        kct = cache_k[i].transpose(0, 2, 3, 1).reshape(bs, heads * HEAD_DIM, lc)
        vct = cache_v[i].transpose(0, 2, 3, 1).reshape(bs, heads * HEAD_DIM, lc)

        hp, (_, ktp, vtp), cpr = _layer(
            hp, p_prompt[i].reshape(bp * sp, -1), sp, sp // lp,
            lambda qk, v_t: _attn_prompt(qk, v_t, rel_rows, bp, sp), None, w,
            tm_proj=lp, tm_mix=512, tm_mlp=512, tm_pe=512)
        hs, (_, kvs), csm = _layer(
            hs, p_sample[i].reshape(bs * ss, -1), ss, None,
            lambda qk, v: _attn_sample(qk, v, kct, vct, rel_rows, bs, ss), state_conv[i], w,
            tm_proj=512, tm_mix=256, tm_mlp=512, tm_pe=512)
        kpr = ktp.reshape(bp, heads, HEAD_DIM, lp).transpose(0, 3, 1, 2)
        vpr = vtp.reshape(bp, heads, HEAD_DIM, lp).transpose(0, 3, 1, 2)
        ksm = kvs[:, :heads * HEAD_DIM].reshape(bs, ss, heads, HEAD_DIM)
        vsm = kvs[:, heads * HEAD_DIM:].reshape(bs, ss, heads, HEAD_DIM)
        for lst, val in zip(outs, (kpr, vpr, cpr, ksm, vsm, csm)):
            lst.append(val)
    k_prompt, v_prompt, conv_prompt, k_sample, v_sample, conv_sample = [jnp.stack(o) for o in outs]
    return (hp.reshape(bp, sp, d), hs.reshape(bs, ss, d), k_prompt, v_prompt, conv_prompt,
            k_sample, v_sample, conv_sample)
```

```python
import functools

import jax
import jax.numpy as jnp
from jax import lax
from jax.experimental import pallas as pl
from jax.experimental.pallas import tpu as pltpu

F32 = jnp.float32
BF16 = jnp.bfloat16

CHUNK = 64
LEFT_CHUNKS = 8
LEFT_LEN = LEFT_CHUNKS * CHUNK
BAND = (LEFT_CHUNKS + 1) * CHUNK
HEAD_DIM = 64
REL_CLIP = 128
EPS = 1e-6
NEG = -1e30
LOG2E = 1.4426950408889634
SCORE_SCALE = HEAD_DIM ** -0.5 * LOG2E

LANES = 128
ATTN_TQ = 256
ATTN_NK = LEFT_LEN + ATTN_TQ
ATTN_SLOTS = ATTN_NK // ATTN_TQ
REL_ROW = ATTN_NK + ATTN_TQ
IN_PROJ_TN = 2048
MLP_TF = 1024
MIB = 1024 * 1024


def _rms(x, g):
    return x * lax.rsqrt(jnp.mean(x * x, axis=-1, keepdims=True) + EPS) * g


def _sigmoid(x):
    return 0.5 * jnp.tanh(0.5 * x) + 0.5


VMEM_MIB = {"in_proj": 56, "attn_prompt": 40, "attn_sample": 32, "mix": 58, "mlp": 56, "pe": 48}


def _params(call, grid_rank):
    return pltpu.CompilerParams(dimension_semantics=("arbitrary",) * grid_rank,
                                vmem_limit_bytes=VMEM_MIB[call] * MIB)


STEP_QK = 2


def _in_proj_kernel(tiles_per_seq, x_ref, g_ref, w_ref, qk_ref, *refs):
    if tiles_per_seq is None:
        v_ref, kv_ref, cb_ref, u_ref, gates_ref, n_ref = refs
    else:
        v_ref, kt_ref, vt_ref, cb_ref, u_ref, gates_ref, n_ref = refs
        is_tail = pl.program_id(0) % tiles_per_seq == tiles_per_seq - 1
    j = pl.program_id(1)
    half = w_ref.shape[1] // 2

    def proj():
        z = jnp.dot(n_ref[...], w_ref[...], preferred_element_type=F32)
        return z[:, :half], z[:, half:]

    def step(jj, tail, body):
        cond = j == jj
        if tail is not None and tiles_per_seq is not None:
            cond = cond & (is_tail if tail else jnp.logical_not(is_tail))
        pl.when(cond)(body)

    def q_k(tail):
        def body():
            q, k = proj()
            qk_ref[:, :half] = (q * SCORE_SCALE).astype(BF16)
            qk_ref[:, half:] = k.astype(BF16)
            if tiles_per_seq is None:
                kv_ref[:, :half] = k
            elif tail:
                kt_ref[...] = k.T
        return body

    def v_bg(tail):
        def body():
            v, bg = proj()
            cb_ref[...] = bg.astype(BF16)
            if tiles_per_seq is None:
                v_ref[...] = v.astype(BF16)
                kv_ref[:, half:] = v
            else:
                vt = v.T
                v_ref[...] = vt.astype(BF16)
                if tail:
                    vt_ref[...] = vt
        return body

    def conv_in():
        cg, xin = proj()
        u_ref[...] = cg * xin

    def gate(first):
        def body():
            if first:
                n_ref[...] = _rms(x_ref[...], g_ref[...]).astype(BF16)
            za, zb = proj()
            gates_ref[:, :half] = _sigmoid(za).astype(BF16)
            gates_ref[:, half:] = _sigmoid(zb).astype(BF16)
        return body

    step(0, None, gate(True))
    pl.when((j > 0) & (j < STEP_QK))(gate(False))
    for tail in ((None,) if tiles_per_seq is None else (False, True)):
        step(STEP_QK, tail, q_k(tail))
        step(STEP_QK + 1, tail, v_bg(tail))
    step(STEP_QK + 2, None, conv_in)


def _in_proj(x, g, w_in, tm, tiles_per_seq):
    t, d = x.shape
    tn = IN_PROJ_TN
    nj = w_in.shape[1] // tn
    half = tn // 2
    assert nj == 5 and d == tn and t % tm == 0

    rows = lambda w: pl.BlockSpec((tm, w), lambda i, j: (i, 0))
    if tiles_per_seq is None:
        kv_shapes = (jax.ShapeDtypeStruct((t, half), BF16),
                     jax.ShapeDtypeStruct((t, tn), F32))
        kv_specs = (rows(half), rows(tn))
    else:
        nseq = t // (tm * tiles_per_seq)
        tail_spec = pl.BlockSpec((None, half, tm), lambda i, j: (i // tiles_per_seq, 0, 0))
        kv_shapes = (jax.ShapeDtypeStruct((half, t), BF16),
                     jax.ShapeDtypeStruct((nseq, half, tm), F32),
                     jax.ShapeDtypeStruct((nseq, half, tm), F32))
        kv_specs = (pl.BlockSpec((half, tm), lambda i, j: (0, i)), tail_spec, tail_spec)
    out_shape = (
        jax.ShapeDtypeStruct((t, tn), BF16),
        *kv_shapes,
        jax.ShapeDtypeStruct((t, half), BF16),
        jax.ShapeDtypeStruct((t, half), F32),
        jax.ShapeDtypeStruct((t, 2 * tn), BF16),
    )
    return pl.pallas_call(
        functools.partial(_in_proj_kernel, tiles_per_seq),
        out_shape=out_shape,
        grid=(t // tm, nj),
        in_specs=[
            pl.BlockSpec((tm, d), lambda i, j: (i, 0)),
            pl.BlockSpec((1, d), lambda i, j: (0, 0)),
            pl.BlockSpec((d, tn), lambda i, j: (0, (j + nj - STEP_QK) % nj)),
        ],
        out_specs=(
            rows(tn),
            *kv_specs,
            rows(half),
            rows(half),
            pl.BlockSpec((tm, tn), lambda i, j: (i, jnp.minimum(j, STEP_QK - 1))),
        ),
        scratch_shapes=[pltpu.VMEM((tm, d), BF16)],
        compiler_params=_params("in_proj", 2),
    )(x, g, w_in)


def _rel_rows(rel_table):
    h, n_rel = rel_table.shape
    lo = LEFT_LEN - REL_CLIP
    hi = ATTN_NK - lo - n_rel
    first = jnp.broadcast_to(rel_table[:, :1], (h, lo))
    last = jnp.broadcast_to(rel_table[:, -1:], (h, hi))
    negative = jnp.broadcast_to(rel_table[:, :1], (h, REL_ROW - ATTN_NK))
    rows = jnp.concatenate([first, rel_table, last, negative], axis=1).astype(F32)
    return (rows - rows[:, :1]) * LOG2E


def _toeplitz(row, nq):
    return pltpu.roll(jnp.broadcast_to(row, (nq, row.shape[1])), 0, 1, stride=1, stride_axis=0)


def _attn_prompt_kernel(n_casts, q_ref, kp_ref, kc_ref, vtp_ref, vtc_ref, rel_ref, *refs):
    cast_in, (o_ref, *cast_out, bias_ref) = refs[:n_casts], refs[n_casts:]
    t = pl.program_id(1)

    for src, dst in zip(cast_in, cast_out):
        dst[...] = src[...].astype(BF16)

    @pl.when((pl.program_id(0) == 0) & (t == 0))
    def _():
        qi = lax.broadcasted_iota(jnp.int32, (ATTN_TQ, ATTN_NK), 0)
        kj = lax.broadcasted_iota(jnp.int32, (ATTN_TQ, ATTN_NK), 1)
        first = (qi // CHUNK) * CHUNK
        visible = (kj >= first) & (kj < first + BAND)
        for h in range(bias_ref.shape[0]):
            bias_ref[h] = jnp.where(visible, _toeplitz(rel_ref[h:h + 1, :], ATTN_TQ)[:, :ATTN_NK], NEG).T

    tiles = q_ref.shape[0] // ATTN_TQ
    prev_slots = kp_ref.shape[0] // ATTN_TQ

    def key_slot(s):
        k_ref, vt_ref = (kp_ref, vtp_ref) if s < prev_slots else (kc_ref, vtc_ref)
        r0 = (s % prev_slots) * ATTN_TQ
        return k_ref.at[pl.ds(r0, ATTN_TQ)], vt_ref.at[:, pl.ds(r0, ATTN_TQ)]

    for s in range(tiles):
        g = t * tiles + s
        rows = pl.ds(s * ATTN_TQ, ATTN_TQ)
        for first_slot in range(ATTN_SLOTS):
            tiles_before = ATTN_SLOTS - 1 - first_slot
            if first_slot and (tiles_before < s or (tiles_before - s) % tiles):
                continue
            cond = (g >= tiles_before) if first_slot == 0 else (g == tiles_before)

            @pl.when(cond)
            def _(s=s, rows=rows, first_slot=first_slot):
                slots = [key_slot(s + prev_slots - (ATTN_SLOTS - 1) + n) for n in range(first_slot, ATTN_SLOTS)]
                _band_heads(q_ref.at[rows], [k for k, _ in slots], [v for _, v in slots], bias_ref,
                            first_slot * ATTN_TQ, o_ref.at[rows])


def _band_heads(q_ref, k_refs, vt_refs, bias_ref, key0, o_ref):
    nk = len(k_refs) * ATTN_TQ
    heads = 2 * (q_ref.shape[1] // LANES)
    lane = lax.broadcasted_iota(jnp.int32, (ATTN_TQ, LANES), 1)
    feat = lax.broadcasted_iota(jnp.int32, (LANES, nk), 0)

    def scores(h):
        sl = slice(h // 2 * LANES, (h // 2 + 1) * LANES)
        own_lane = (lane >= HEAD_DIM) if h % 2 else (lane < HEAD_DIM)
        q_pair = q_ref[:, sl]
        qm = jnp.where(own_lane, q_pair, jnp.zeros_like(q_pair))
        keys = jnp.concatenate([r[:, sl] for r in k_refs], axis=0)
        return lax.dot_general(keys, qm, (((1,), (1,)), ((), ())), preferred_element_type=F32)

    def probabilities(h, st):
        groups = []
        for c0 in range(0, ATTN_TQ, LANES):
            first = c0 // CHUNK * CHUNK
            last = (c0 + LANES - CHUNK) // CHUNK * CHUNK + BAND
            lo, hi = max(first - key0, 0), min(last - key0, nk)
            z0 = min(max(first + CHUNK - key0, lo), hi)
            z1 = min(max(c0 + LEFT_LEN - REL_CLIP - key0, z0), hi)
            cols = slice(c0, c0 + LANES)
            biased = lambda a, b: st[a:b, cols] + bias_ref[h, key0 + a:key0 + b, cols]
            parts = ([biased(lo, z0)] if z0 > lo else []) + ([st[z0:z1, cols]] if z1 > z0 else []) \
                + ([biased(z1, hi)] if hi > z1 else [])
            s = jnp.concatenate(parts, axis=0)
            p = jnp.exp2(s - jnp.max(s, axis=0, keepdims=True)).astype(BF16)
            pads = [jnp.zeros((n, LANES), BF16) for n in (lo, nk - hi)]
            groups.append(jnp.concatenate([x for x in (pads[0], p, pads[1]) if x.shape[0]], axis=0))
        return jnp.concatenate(groups, axis=1)

    def weighted_values(h, st):
        sl = slice(h // 2 * LANES, (h // 2 + 1) * LANES)
        own_feat = (feat >= HEAD_DIM) if h % 2 else (feat < HEAD_DIM)
        pt = probabilities(h, st)
        vals_t = jnp.concatenate([r[sl, :] for r in vt_refs], axis=1)
        vm = jnp.where(own_feat, vals_t, jnp.ones_like(vals_t))
        acc = jnp.dot(vm, pt, preferred_element_type=F32)
        lo, hi = acc[:HEAD_DIM], acc[HEAD_DIM:]
        return hi / lo if h % 2 else lo / hi

    halves = []
    ahead = 3
    pending = [scores(h) for h in range(ahead)]
    for h in range(heads):
        if h + ahead < heads:
            pending.append(scores(h + ahead))
        halves.append(weighted_values(h, pending.pop(0)))
        if h % 2:
            sl = slice(h // 2 * LANES, (h // 2 + 1) * LANES)
            o_ref[:, sl] = jnp.concatenate(halves, axis=0).T.astype(o_ref.dtype)
            halves = []


def _attn_prompt(qk, v_t, rel_rows, batch, seq, casts):
    t, width = v_t.shape[1], v_t.shape[0]
    step = LEFT_LEN
    assert seq % step == 0 and step % ATTN_TQ == 0
    nt = seq // step
    blk = (step, width)
    here = lambda b, i: b * nt + i
    before = lambda b, i: b * nt + jnp.maximum(i - 1, 0)

    steps = batch * nt
    cast_specs = []
    for a, axis in casts:
        assert a.ndim == 2 and a.shape[axis] % steps == 0
        shape = tuple(n // steps if ax == axis else n for ax, n in enumerate(a.shape))
        cast_specs.append(pl.BlockSpec(
            shape, lambda b, i, axis=axis: tuple(here(b, i) if ax == axis else 0 for ax in range(2))))
    outs = pl.pallas_call(
        functools.partial(_attn_prompt_kernel, len(casts)),
        out_shape=(jax.ShapeDtypeStruct((t, width), BF16),
                   *[jax.ShapeDtypeStruct(a.shape, BF16) for a, _ in casts]),
        grid=(batch, nt),
        in_specs=[
            pl.BlockSpec(blk, lambda b, i: (here(b, i), 0)),
            pl.BlockSpec(blk, lambda b, i: (before(b, i), 1)),
            pl.BlockSpec(blk, lambda b, i: (here(b, i), 1)),
            pl.BlockSpec((width, step), lambda b, i: (0, before(b, i))),
            pl.BlockSpec((width, step), lambda b, i: (0, here(b, i))),
            pl.BlockSpec(rel_rows.shape, lambda b, i: (0, 0)),
            *cast_specs,
        ],
        out_specs=(pl.BlockSpec(blk, lambda b, i: (here(b, i), 0)), *cast_specs),
        scratch_shapes=[pltpu.VMEM((rel_rows.shape[0], ATTN_NK, ATTN_TQ), F32)],
        compiler_params=_params("attn_prompt", 2),
    )(qk, qk, qk, v_t, v_t, rel_rows, *[a for a, _ in casts])
    return outs[0], outs[1:]


def _attn_sample_kernel(q_ref, kn_ref, vn_ref, kct_ref, vct_ref, rel_ref, o_ref, bias_ref):
    nq = q_ref.shape[0]
    lc = kct_ref.shape[1]

    @pl.when(pl.program_id(0) == 0)
    def _():
        for h in range(bias_ref.shape[0]):
            bias_ref[h] = _toeplitz(rel_ref[h:h + 1, :], nq)[:, :bias_ref.shape[2]]

    lane = lax.broadcasted_iota(jnp.int32, (nq, LANES), 1)
    nt_dims = (((1,), (1,)), ((), ()))

    def scores(h):
        sl = slice(h // 2 * LANES, (h // 2 + 1) * LANES)
        own = (lane >= HEAD_DIM) if h % 2 else (lane < HEAD_DIM)
        qm = jnp.where(own, q_ref[:, sl], jnp.zeros((nq, LANES), BF16))
        s_cache = jnp.dot(qm, kct_ref[sl, :].astype(BF16), preferred_element_type=F32)
        s_new = lax.dot_general(qm, kn_ref[:, sl], nt_dims, preferred_element_type=F32)
        return s_cache + bias_ref[h, :, :lc], s_new + bias_ref[h, :, lc:lc + nq]

    def weighted_values(h, s_cache, s_new):
        sl = slice(h // 2 * LANES, (h // 2 + 1) * LANES)
        m = jnp.maximum(jnp.max(s_cache, axis=-1, keepdims=True), jnp.max(s_new, axis=-1, keepdims=True))
        p_cache, p_new = jnp.exp2(s_cache - m), jnp.exp2(s_new - m)
        denom = jnp.sum(p_cache, axis=-1, keepdims=True) + jnp.sum(p_new, axis=-1, keepdims=True)
        acc = lax.dot_general(p_cache.astype(BF16), vct_ref[sl, :].astype(BF16), nt_dims,
                              preferred_element_type=F32)
        acc = acc + jnp.dot(p_new.astype(BF16), vn_ref[:, sl], preferred_element_type=F32)
        return acc / denom

    heads = bias_ref.shape[0]
    ahead = 3
    pending = [scores(h) for h in range(ahead)]
    for h in range(heads):
        if h + ahead < heads:
            pending.append(scores(h + ahead))
        out = weighted_values(h, *pending.pop(0))
        if h % 2:
            sl = slice(h // 2 * LANES, (h // 2 + 1) * LANES)
            o_ref[:, sl] = jnp.where(lane < HEAD_DIM, prev_out, out).astype(o_ref.dtype)
        prev_out = out


def _attn_sample(qk, v, cache_kt, cache_vt, rel_rows, batch, seq):
    t, width = v.shape
    lc = cache_kt.shape[2]
    blk = (seq, width)
    bias_cols = -(-(lc + seq) // LANES) * LANES
    return pl.pallas_call(
        _attn_sample_kernel,
        out_shape=jax.ShapeDtypeStruct((t, width), BF16),
        grid=(batch,),
        in_specs=[
            pl.BlockSpec(blk, lambda b: (b, 0)),
            pl.BlockSpec(blk, lambda b: (b, 1)),
            pl.BlockSpec(blk, lambda b: (b, 0)),
            pl.BlockSpec((None, width, lc), lambda b: (b, 0, 0)),
            pl.BlockSpec((None, width, lc), lambda b: (b, 0, 0)),
            pl.BlockSpec(rel_rows.shape, lambda b: (0, 0)),
        ],
        out_specs=pl.BlockSpec(blk, lambda b: (b, 0)),
        scratch_shapes=[pltpu.VMEM((rel_rows.shape[0], seq, bias_cols), F32)],
        compiler_params=_params("attn_sample", 1),
    )(qk, qk, v, cache_kt, cache_vt, rel_rows)


def _mix_kernel(seq_rows, attn_ref, cb_ref, u_ref, halo_ref, gates_ref, x_ref, wa_ref, wc_ref, wo_ref,
                cw_ref, cbias_ref, g_ref, o_ref):
    tm, d = x_ref.shape
    u = u_ref[...]
    row = lax.broadcasted_iota(jnp.int32, u.shape, 0)
    if len(halo_ref.shape) == 3:
        nseq = halo_ref.shape[0]
        state = halo_ref[...]
        per_row = lambda r: jnp.broadcast_to(state[:, r:r + 1, :], (nseq, seq_rows, u.shape[1])).reshape(u.shape)
        prev2, prev1 = per_row(0), per_row(1)
        row = row % seq_rows
    else:
        keep = ((pl.program_id(0) * tm) % seq_rows != 0).astype(F32)
        nh = halo_ref.shape[0]
        prev2, prev1 = halo_ref[nh - 2:nh - 1, :] * keep, halo_ref[nh - 1:nh, :] * keep
    um1 = jnp.where(row == 0, prev1, pltpu.roll(u, 1, axis=0))
    um2 = jnp.where(row == 0, prev2, jnp.where(row == 1, prev1, pltpu.roll(u, 2, axis=0)))
    conv = cbias_ref[...] + cw_ref[0:1, :] * um2 + cw_ref[1:2, :] * um1 + cw_ref[2:3, :] * u
    ya = jnp.dot(attn_ref[...], wa_ref[...], preferred_element_type=F32)
    yb = jnp.dot((cb_ref[...].astype(F32) * conv).astype(BF16), wc_ref[...], preferred_element_type=F32)
    merged = gates_ref[:, :d].astype(F32) * ya + gates_ref[:, d:].astype(F32) * yb
    mix = jnp.dot(merged.astype(BF16), wo_ref[...], preferred_element_type=F32)
    o_ref[...] = x_ref[...] + _rms(mix, g_ref[...])


def _mix(attn, cb, u, state, gates, x, wa, wc, wo, conv_w, conv_b, g, tm, seq_rows):
    t, d = x.shape
    c = u.shape[1]
    halo_rows = 8
    if state is None:
        halo, halo_spec = u, pl.BlockSpec(
            (halo_rows, c), lambda i: (jnp.maximum(i * (tm // halo_rows) - 1, 0), 0))
    else:
        assert tm % seq_rows == 0
        halo, halo_spec = state, pl.BlockSpec((tm // seq_rows,) + state.shape[1:], lambda i: (i, 0, 0))
    row_spec = lambda w: pl.BlockSpec((tm, w), lambda i: (i, 0))
    resident = lambda a: pl.BlockSpec(a.shape, lambda i: (0,) * a.ndim, pipeline_mode=pl.Buffered(1))
    return pl.pallas_call(
        functools.partial(_mix_kernel, seq_rows),
        out_shape=jax.ShapeDtypeStruct((t, d), F32),
        grid=(t // tm,),
        in_specs=[row_spec(attn.shape[1]), row_spec(c), row_spec(c), halo_spec, row_spec(2 * d), row_spec(d),
                  resident(wa), resident(wc), resident(wo), resident(conv_w), resident(conv_b), resident(g)],
        out_specs=row_spec(d),
        compiler_params=_params("mix", 1),
    )(attn, cb, u, halo, gates, x, wa, wc, wo, conv_w, conv_b, g)


def _mlp_kernel(nj, h_ref, g1_ref, wu_ref, wd_ref, g2_ref, o_ref, n_ref, acc_ref):
    j = pl.program_id(1)

    def step(first, last):
        def body():
            if first:
                n_ref[...] = _rms(h_ref[...], g1_ref[...]).astype(BF16)
            a = jnp.dot(n_ref[...], wu_ref[...], preferred_element_type=F32)
            a = jnp.square(jnp.maximum(a, 0.0)).astype(BF16)
            part = jnp.dot(a, wd_ref[...], preferred_element_type=F32)
            if last:
                f = part if first else acc_ref[...] + part
                o_ref[...] = h_ref[...] + _rms(f, g2_ref[...])
            elif first:
                acc_ref[...] = part
            else:
                acc_ref[...] += part
        return body

    if nj == 1:
        step(True, True)()
    else:
        pl.when(j == 0)(step(True, False))
        pl.when(j == nj - 1)(step(False, True))
        if nj > 2:
            pl.when((j > 0) & (j < nj - 1))(step(False, False))


def _mlp(h, g1, w_up, w_down, g2, tm):
    t, d = h.shape
    tf = MLP_TF
    nj = w_up.shape[1] // tf
    return pl.pallas_call(
        functools.partial(_mlp_kernel, nj),
        out_shape=jax.ShapeDtypeStruct((t, d), F32),
        grid=(t // tm, nj),
        in_specs=[
            pl.BlockSpec((tm, d), lambda i, j: (i, 0)),
            pl.BlockSpec((1, d), lambda i, j: (0, 0)),
            pl.BlockSpec((d, tf), lambda i, j: (0, j)),
            pl.BlockSpec((tf, d), lambda i, j: (j, 0)),
            pl.BlockSpec((1, d), lambda i, j: (0, 0)),
        ],
        out_specs=pl.BlockSpec((tm, d), lambda i, j: (i, 0)),
        scratch_shapes=[pltpu.VMEM((tm, d), BF16), pltpu.VMEM((tm, d), F32)],
        compiler_params=_params("mlp", 2),
    )(h, g1, w_up, w_down, g2)


def _pe_kernel(row_splits, h_ref, p_ref, wg_ref, wp_ref, g_ref, o_ref):
    rows = h_ref.shape[0] // row_splits

    def products(r):
        sl = slice(r * rows, (r + 1) * rows)
        return (jnp.dot(h_ref[sl, :].astype(BF16), wg_ref[...], preferred_element_type=F32),
                jnp.dot(p_ref[sl, :].astype(BF16), wp_ref[...], preferred_element_type=F32))

    pending = products(0)
    for r in range(row_splits):
        gate_pre, pe = pending
        if r + 1 < row_splits:
            pending = products(r + 1)
        sl = slice(r * rows, (r + 1) * rows)
        o_ref[sl, :] = h_ref[sl, :] + _rms(_sigmoid(gate_pre) * pe, g_ref[...])


def _pe(h, p, wg, wp, g, tm):
    t, d = h.shape
    resident = lambda a: pl.BlockSpec(a.shape, lambda i: (0,) * a.ndim, pipeline_mode=pl.Buffered(1))
    return pl.pallas_call(
        functools.partial(_pe_kernel, max(tm // 256, 1)),
        out_shape=jax.ShapeDtypeStruct((t, d), F32),
        grid=(t // tm,),
        in_specs=[pl.BlockSpec((tm, d), lambda i: (i, 0)), pl.BlockSpec((tm, p.shape[1]), lambda i: (i, 0)),
                  resident(wg), resident(wp), resident(g)],
        out_specs=pl.BlockSpec((tm, d), lambda i: (i, 0)),
        compiler_params=_params("pe", 1),
    )(h, p, wg, wp, g)


def _layer(x, pe_in, seq, tiles_per_seq, attn_fn, state, w, tm_proj, tm_mix, tm_mlp, tm_pe):
    t = x.shape[0]
    qk, *kv, cb, u, gates = _in_proj(x, w["g_pre_mix"], w["w_in"], min(tm_proj, t), tiles_per_seq)
    attn, cast_weights = attn_fn(qk, kv[0])
    w = {**w, **cast_weights}
    h = _mix(attn, cb, u, state, gates, x, w["w_attn_out"], w["w_conv_out"], w["w_o"], w["conv_w"],
             w["conv_b"], w["g_post_mix"], min(tm_mix, t), seq)
    h = _mlp(h, w["g_pre_mlp"], w["w_up"], w["w_down"], w["g_post_mlp"], min(tm_mlp, t))
    h = _pe(h, pe_in, w["w_pe_gate"], w["w_pe"], w["g_pe"], min(tm_pe, t))
    return h, kv, u.reshape(t // seq, seq, -1)[:, -2:], w


def kernel(x_prompt, x_sample, cache_k, cache_v, state_conv, p_prompt, p_sample, w_in, rel_table, w_attn_out,
           conv_w, conv_b, w_conv_out, w_o, g_pre_mix, g_post_mix, g_pre_mlp, g_post_mlp, w_up, w_down, w_pe,
           w_pe_gate, g_pe):
    depth = w_in.shape[0]
    bp, sp, d = x_prompt.shape
    bs, ss, _ = x_sample.shape
    lc, heads = cache_k.shape[2], cache_k.shape[3]
    lp = min(LEFT_LEN, sp)
    assert sp % lp == 0 and sp % ATTN_TQ == 0 and lc == LEFT_LEN and ss <= CHUNK
    assert rel_table.shape[2] == 2 * REL_CLIP + 1

    hp = x_prompt.reshape(bp * sp, d)
    hs = x_sample.reshape(bs * ss, d)
    outs = [[] for _ in range(6)]
    for i in range(depth):
        w = {
            "w_in": w_in[i].astype(BF16), "w_attn_out": w_attn_out[i].astype(BF16),
            "w_conv_out": w_conv_out[i].astype(BF16), "w_o": w_o[i].astype(BF16),
            "w_pe": w_pe[i].astype(BF16), "w_pe_gate": w_pe_gate[i].astype(BF16),
            "conv_w": conv_w[i], "conv_b": conv_b[i][None],
            "g_pre_mix": g_pre_mix[i][None], "g_post_mix": g_post_mix[i][None],
            "g_pre_mlp": g_pre_mlp[i][None], "g_post_mlp": g_post_mlp[i][None], "g_pe": g_pe[i][None],
        }
        rel_rows = _rel_rows(rel_table[i])
        kct = cache_k[i].transpose(0, 2, 3, 1).reshape(bs, heads * HEAD_DIM, lc)
        vct = cache_v[i].transpose(0, 2, 3, 1).reshape(bs, heads * HEAD_DIM, lc)

        def attn_prompt(qk, v_t):
            attn, (wu, wd) = _attn_prompt(qk, v_t, rel_rows, bp, sp, [(w_up[i], 1), (w_down[i], 0)])
            return attn, {"w_up": wu, "w_down": wd}

        hp, (_, ktp, vtp), cpr, w = _layer(
            hp, p_prompt[i].reshape(bp * sp, -1), sp, sp // lp, attn_prompt, None, w,
            tm_proj=lp, tm_mix=512, tm_mlp=512, tm_pe=512)
        hs, (_, kvs), csm, _ = _layer(
            hs, p_sample[i].reshape(bs * ss, -1), ss, None,
            lambda qk, v: (_attn_sample(qk, v, kct, vct, rel_rows, bs, ss), {}), state_conv[i], w,
            tm_proj=512, tm_mix=256, tm_mlp=512, tm_pe=512)
        kpr = ktp.reshape(bp, heads, HEAD_DIM, lp).transpose(0, 3, 1, 2)
        vpr = vtp.reshape(bp, heads, HEAD_DIM, lp).transpose(0, 3, 1, 2)
        ksm = kvs[:, :heads * HEAD_DIM].reshape(bs, ss, heads, HEAD_DIM)
        vsm = kvs[:, heads * HEAD_DIM:].reshape(bs, ss, heads, HEAD_DIM)
        for lst, val in zip(outs, (kpr, vpr, cpr, ksm, vsm, csm)):
            lst.append(val)
    k_prompt, v_prompt, conv_prompt, k_sample, v_sample, conv_sample = [jnp.stack(o) for o in outs]
    return (hp.reshape(bp, sp, d), hs.reshape(bs, ss, d), k_prompt, v_prompt, conv_prompt,
            k_sample, v_sample, conv_sample)
```

```python
import functools

import jax
import jax.numpy as jnp
from jax import lax
from jax.experimental import pallas as pl
from jax.experimental.pallas import tpu as pltpu

F32 = jnp.float32
BF16 = jnp.bfloat16

CHUNK = 64
LEFT_CHUNKS = 8
LEFT_LEN = LEFT_CHUNKS * CHUNK
BAND = (LEFT_CHUNKS + 1) * CHUNK
HEAD_DIM = 64
REL_CLIP = 128
EPS = 1e-6
NEG = -1e30
LOG2E = 1.4426950408889634
SCORE_SCALE = HEAD_DIM ** -0.5 * LOG2E

LANES = 128
ATTN_TQ = 256
ATTN_NK = LEFT_LEN + ATTN_TQ
ATTN_SLOTS = ATTN_NK // ATTN_TQ
REL_ROW = ATTN_NK + ATTN_TQ
ATTN_STEP_ROWS = 1024
IN_PROJ_TN = 2048
MLP_TF = 1024
MIB = 1024 * 1024


def _rms(x, g):
    return x * lax.rsqrt(jnp.mean(x * x, axis=-1, keepdims=True) + EPS) * g


def _sigmoid(x):
    return 0.5 * jnp.tanh(0.5 * x) + 0.5


VMEM_MIB = {"in_proj": 56, "attn_prompt": 56, "attn_sample": 32, "mix": 58, "mlp": 56, "pe": 48}


def _params(call, grid_rank):
    return pltpu.CompilerParams(dimension_semantics=("arbitrary",) * grid_rank,
                                vmem_limit_bytes=VMEM_MIB[call] * MIB)


STEP_QK = 2


def _in_proj_kernel(tiles_per_seq, x_ref, g_ref, w_ref, qk_ref, *refs):
    if tiles_per_seq is None:
        v_ref, kv_ref, cb_ref, u_ref, gates_ref, n_ref = refs
    else:
        v_ref, kt_ref, vt_ref, cb_ref, u_ref, gates_ref, n_ref = refs
        is_tail = pl.program_id(0) % tiles_per_seq == tiles_per_seq - 1
    j = pl.program_id(1)
    half = w_ref.shape[1] // 2

    def proj():
        z = jnp.dot(n_ref[...], w_ref[...], preferred_element_type=F32)
        return z[:, :half], z[:, half:]

    def step(jj, tail, body):
        cond = j == jj
        if tail is not None and tiles_per_seq is not None:
            cond = cond & (is_tail if tail else jnp.logical_not(is_tail))
        pl.when(cond)(body)

    def q_k(tail):
        def body():
            q, k = proj()
            qk_ref[:, :half] = (q * SCORE_SCALE).astype(BF16)
            qk_ref[:, half:] = k.astype(BF16)
            if tiles_per_seq is None:
                kv_ref[:, :half] = k
            elif tail:
                kt_ref[...] = k.T
        return body

    def v_bg(tail):
        def body():
            v, bg = proj()
            cb_ref[...] = bg.astype(BF16)
            if tiles_per_seq is None:
                v_ref[...] = v.astype(BF16)
                kv_ref[:, half:] = v
            else:
                vt = v.T
                v_ref[...] = vt.astype(BF16)
                if tail:
                    vt_ref[...] = vt
        return body

    def conv_in():
        cg, xin = proj()
        u_ref[...] = cg * xin

    def gate(first):
        def body():
            if first:
                n_ref[...] = _rms(x_ref[...], g_ref[...]).astype(BF16)
            za, zb = proj()
            gates_ref[:, :half] = _sigmoid(za).astype(BF16)
            gates_ref[:, half:] = _sigmoid(zb).astype(BF16)
        return body

    step(0, None, gate(True))
    pl.when((j > 0) & (j < STEP_QK))(gate(False))
    for tail in ((None,) if tiles_per_seq is None else (False, True)):
        step(STEP_QK, tail, q_k(tail))
        step(STEP_QK + 1, tail, v_bg(tail))
    step(STEP_QK + 2, None, conv_in)


def _in_proj(x, g, w_in, tm, tiles_per_seq):
    t, d = x.shape
    tn = IN_PROJ_TN
    nj = w_in.shape[1] // tn
    half = tn // 2
    assert nj == 5 and d == tn and t % tm == 0

    rows = lambda w: pl.BlockSpec((tm, w), lambda i, j: (i, 0))
    if tiles_per_seq is None:
        kv_shapes = (jax.ShapeDtypeStruct((t, half), BF16),
                     jax.ShapeDtypeStruct((t, tn), F32))
        kv_specs = (rows(half), rows(tn))
    else:
        nseq = t // (tm * tiles_per_seq)
        tail_spec = pl.BlockSpec((None, half, tm), lambda i, j: (i // tiles_per_seq, 0, 0))
        kv_shapes = (jax.ShapeDtypeStruct((half, t), BF16),
                     jax.ShapeDtypeStruct((nseq, half, tm), F32),
                     jax.ShapeDtypeStruct((nseq, half, tm), F32))
        kv_specs = (pl.BlockSpec((half, tm), lambda i, j: (0, i)), tail_spec, tail_spec)
    out_shape = (
        jax.ShapeDtypeStruct((t, tn), BF16),
        *kv_shapes,
        jax.ShapeDtypeStruct((t, half), BF16),
        jax.ShapeDtypeStruct((t, half), F32),
        jax.ShapeDtypeStruct((t, 2 * tn), BF16),
    )
    return pl.pallas_call(
        functools.partial(_in_proj_kernel, tiles_per_seq),
        out_shape=out_shape,
        grid=(t // tm, nj),
        in_specs=[
            pl.BlockSpec((tm, d), lambda i, j: (i, 0)),
            pl.BlockSpec((1, d), lambda i, j: (0, 0)),
            pl.BlockSpec((d, tn), lambda i, j: (0, (j + nj - STEP_QK) % nj)),
        ],
        out_specs=(
            rows(tn),
            *kv_specs,
            rows(half),
            rows(half),
            pl.BlockSpec((tm, tn), lambda i, j: (i, jnp.minimum(j, STEP_QK - 1))),
        ),
        scratch_shapes=[pltpu.VMEM((tm, d), BF16)],
        compiler_params=_params("in_proj", 2),
    )(x, g, w_in)


def _rel_rows(rel_table):
    h, n_rel = rel_table.shape
    lo = LEFT_LEN - REL_CLIP
    hi = ATTN_NK - lo - n_rel
    first = jnp.broadcast_to(rel_table[:, :1], (h, lo))
    last = jnp.broadcast_to(rel_table[:, -1:], (h, hi))
    negative = jnp.broadcast_to(rel_table[:, :1], (h, REL_ROW - ATTN_NK))
    rows = jnp.concatenate([first, rel_table, last, negative], axis=1).astype(F32)
    return (rows - rows[:, :1]) * LOG2E


def _toeplitz(row, nq):
    return pltpu.roll(jnp.broadcast_to(row, (nq, row.shape[1])), 0, 1, stride=1, stride_axis=0)


def _attn_prompt_kernel(n_casts, q_ref, kp_ref, kc_ref, vtp_ref, vtc_ref, rel_ref, *refs):
    cast_in, (o_ref, *cast_out, bias_ref) = refs[:n_casts], refs[n_casts:]
    t = pl.program_id(1)

    for src, dst in zip(cast_in, cast_out):
        dst[...] = src[...].astype(BF16)

    @pl.when((pl.program_id(0) == 0) & (t == 0))
    def _():
        qi = lax.broadcasted_iota(jnp.int32, (ATTN_TQ, ATTN_NK), 0)
        kj = lax.broadcasted_iota(jnp.int32, (ATTN_TQ, ATTN_NK), 1)
        first = (qi // CHUNK) * CHUNK
        visible = (kj >= first) & (kj < first + BAND)
        for h in range(bias_ref.shape[0]):
            bias_ref[h] = jnp.where(visible, _toeplitz(rel_ref[h:h + 1, :], ATTN_TQ)[:, :ATTN_NK], NEG).T

    tiles = q_ref.shape[0] // ATTN_TQ
    prev_slots = kp_ref.shape[0] // ATTN_TQ

    def key_slot(s):
        k_ref, vt_ref = (kp_ref, vtp_ref) if s < prev_slots else (kc_ref, vtc_ref)
        r0 = (s if s < prev_slots else s - prev_slots) * ATTN_TQ
        return k_ref.at[pl.ds(r0, ATTN_TQ)], vt_ref.at[:, pl.ds(r0, ATTN_TQ)]

    for s in range(tiles):
        g = t * tiles + s
        rows = pl.ds(s * ATTN_TQ, ATTN_TQ)
        for first_slot in range(ATTN_SLOTS):
            tiles_before = ATTN_SLOTS - 1 - first_slot
            if first_slot and (tiles_before < s or (tiles_before - s) % tiles):
                continue
            cond = (g >= tiles_before) if first_slot == 0 else (g == tiles_before)

            @pl.when(cond)
            def _(s=s, rows=rows, first_slot=first_slot):
                slots = [key_slot(s + prev_slots - (ATTN_SLOTS - 1) + n) for n in range(first_slot, ATTN_SLOTS)]
                _band_heads(q_ref.at[rows], [k for k, _ in slots], [v for _, v in slots], bias_ref,
                            first_slot * ATTN_TQ, o_ref.at[rows])


def _band_heads(q_ref, k_refs, vt_refs, bias_ref, key0, o_ref):
    nk = len(k_refs) * ATTN_TQ
    heads = 2 * (q_ref.shape[1] // LANES)
    lane = lax.broadcasted_iota(jnp.int32, (ATTN_TQ, LANES), 1)
    feat = lax.broadcasted_iota(jnp.int32, (LANES, nk), 0)

    def scores(h):
        sl = slice(h // 2 * LANES, (h // 2 + 1) * LANES)
        own_lane = (lane >= HEAD_DIM) if h % 2 else (lane < HEAD_DIM)
        q_pair = q_ref[:, sl]
        qm = jnp.where(own_lane, q_pair, jnp.zeros_like(q_pair))
        keys = jnp.concatenate([r[:, sl] for r in k_refs], axis=0)
        return lax.dot_general(keys, qm, (((1,), (1,)), ((), ())), preferred_element_type=F32)

    def probabilities(h, st):
        groups = []
        for c0 in range(0, ATTN_TQ, LANES):
            first = c0 // CHUNK * CHUNK
            last = (c0 + LANES - CHUNK) // CHUNK * CHUNK + BAND
            lo, hi = max(first - key0, 0), min(last - key0, nk)
            z0 = min(max(first + CHUNK - key0, lo), hi)
            z1 = min(max(c0 + LEFT_LEN - REL_CLIP - key0, z0), hi)
            cols = slice(c0, c0 + LANES)
            biased = lambda a, b: st[a:b, cols] + bias_ref[h, key0 + a:key0 + b, cols]
            parts = ([biased(lo, z0)] if z0 > lo else []) + ([st[z0:z1, cols]] if z1 > z0 else []) \
                + ([biased(z1, hi)] if hi > z1 else [])
            s = jnp.concatenate(parts, axis=0)
            p = jnp.exp2(s - jnp.max(s, axis=0, keepdims=True)).astype(BF16)
            pads = [jnp.zeros((n, LANES), BF16) for n in (lo, nk - hi)]
            groups.append(jnp.concatenate([x for x in (pads[0], p, pads[1]) if x.shape[0]], axis=0))
        return jnp.concatenate(groups, axis=1)

    def weighted_values(h, st):
        sl = slice(h // 2 * LANES, (h // 2 + 1) * LANES)
        own_feat = (feat >= HEAD_DIM) if h % 2 else (feat < HEAD_DIM)
        pt = probabilities(h, st)
        vals_t = jnp.concatenate([r[sl, :] for r in vt_refs], axis=1)
        vm = jnp.where(own_feat, vals_t, jnp.ones_like(vals_t))
        acc = jnp.dot(vm, pt, preferred_element_type=F32)
        lo, hi = acc[:HEAD_DIM], acc[HEAD_DIM:]
        return hi / lo if h % 2 else lo / hi

    halves = []
    ahead = 3
    pending = [scores(h) for h in range(ahead)]
    for h in range(heads):
        if h + ahead < heads:
            pending.append(scores(h + ahead))
        halves.append(weighted_values(h, pending.pop(0)))
        if h % 2:
            sl = slice(h // 2 * LANES, (h // 2 + 1) * LANES)
            o_ref[:, sl] = jnp.concatenate(halves, axis=0).T.astype(o_ref.dtype)
            halves = []


def _attn_prompt(qk, v_t, rel_rows, batch, seq, casts):
    t, width = v_t.shape[1], v_t.shape[0]
    step = min(ATTN_STEP_ROWS, seq)
    assert seq % step == 0 and step % LEFT_LEN == 0
    nt = seq // step
    blk = (step, width)
    left = step // LEFT_LEN
    here = lambda b, i: b * nt + i
    before = lambda b, i: (b * nt + i) * left - jnp.where(i > 0, 1, 0)

    steps = batch * nt
    cast_specs = []
    for a, axis in casts:
        assert a.ndim == 2 and a.shape[axis] % steps == 0
        shape = tuple(n // steps if ax == axis else n for ax, n in enumerate(a.shape))
        cast_specs.append(pl.BlockSpec(
            shape, lambda b, i, axis=axis: tuple(here(b, i) if ax == axis else 0 for ax in range(2))))
    outs = pl.pallas_call(
        functools.partial(_attn_prompt_kernel, len(casts)),
        out_shape=(jax.ShapeDtypeStruct((t, width), BF16),
                   *[jax.ShapeDtypeStruct(a.shape, BF16) for a, _ in casts]),
        grid=(batch, nt),
        in_specs=[
            pl.BlockSpec(blk, lambda b, i: (here(b, i), 0)),
            pl.BlockSpec((LEFT_LEN, width), lambda b, i: (before(b, i), 1)),
            pl.BlockSpec(blk, lambda b, i: (here(b, i), 1)),
            pl.BlockSpec((width, LEFT_LEN), lambda b, i: (0, before(b, i))),
            pl.BlockSpec((width, step), lambda b, i: (0, here(b, i))),
            pl.BlockSpec(rel_rows.shape, lambda b, i: (0, 0)),
            *cast_specs,
        ],
        out_specs=(pl.BlockSpec(blk, lambda b, i: (here(b, i), 0)), *cast_specs),
        scratch_shapes=[pltpu.VMEM((rel_rows.shape[0], ATTN_NK, ATTN_TQ), F32)],
        compiler_params=_params("attn_prompt", 2),
    )(qk, qk, qk, v_t, v_t, rel_rows, *[a for a, _ in casts])
    return outs[0], outs[1:]


def _attn_sample_kernel(q_ref, kn_ref, vn_ref, kct_ref, vct_ref, rel_ref, o_ref, bias_ref):
    nq = q_ref.shape[0]
    lc = kct_ref.shape[1]

    @pl.when(pl.program_id(0) == 0)
    def _():
        for h in range(bias_ref.shape[0]):
            bias_ref[h] = _toeplitz(rel_ref[h:h + 1, :], nq)[:, :bias_ref.shape[2]]

    lane = lax.broadcasted_iota(jnp.int32, (nq, LANES), 1)
    nt_dims = (((1,), (1,)), ((), ()))

    def scores(h):
        sl = slice(h // 2 * LANES, (h // 2 + 1) * LANES)
        own = (lane >= HEAD_DIM) if h % 2 else (lane < HEAD_DIM)
        qm = jnp.where(own, q_ref[:, sl], jnp.zeros((nq, LANES), BF16))
        s_cache = jnp.dot(qm, kct_ref[sl, :].astype(BF16), preferred_element_type=F32)
        s_new = lax.dot_general(qm, kn_ref[:, sl], nt_dims, preferred_element_type=F32)
        return s_cache + bias_ref[h, :, :lc], s_new + bias_ref[h, :, lc:lc + nq]

    def weighted_values(h, s_cache, s_new):
        sl = slice(h // 2 * LANES, (h // 2 + 1) * LANES)
        m = jnp.maximum(jnp.max(s_cache, axis=-1, keepdims=True), jnp.max(s_new, axis=-1, keepdims=True))
        p_cache, p_new = jnp.exp2(s_cache - m), jnp.exp2(s_new - m)
        denom = jnp.sum(p_cache, axis=-1, keepdims=True) + jnp.sum(p_new, axis=-1, keepdims=True)
        acc = lax.dot_general(p_cache.astype(BF16), vct_ref[sl, :].astype(BF16), nt_dims,
                              preferred_element_type=F32)
        acc = acc + jnp.dot(p_new.astype(BF16), vn_ref[:, sl], preferred_element_type=F32)
        return acc / denom

    heads = bias_ref.shape[0]
    ahead = 3
    pending = [scores(h) for h in range(ahead)]
    for h in range(heads):
        if h + ahead < heads:
            pending.append(scores(h + ahead))
        out = weighted_values(h, *pending.pop(0))
        if h % 2:
            sl = slice(h // 2 * LANES, (h // 2 + 1) * LANES)
            o_ref[:, sl] = jnp.where(lane < HEAD_DIM, prev_out, out).astype(o_ref.dtype)
        prev_out = out


def _attn_sample(qk, v, cache_kt, cache_vt, rel_rows, batch, seq):
    t, width = v.shape
    lc = cache_kt.shape[2]
    blk = (seq, width)
    bias_cols = -(-(lc + seq) // LANES) * LANES
    return pl.pallas_call(
        _attn_sample_kernel,
        out_shape=jax.ShapeDtypeStruct((t, width), BF16),
        grid=(batch,),
        in_specs=[
            pl.BlockSpec(blk, lambda b: (b, 0)),
            pl.BlockSpec(blk, lambda b: (b, 1)),
            pl.BlockSpec(blk, lambda b: (b, 0)),
            pl.BlockSpec((None, width, lc), lambda b: (b, 0, 0)),
            pl.BlockSpec((None, width, lc), lambda b: (b, 0, 0)),
            pl.BlockSpec(rel_rows.shape, lambda b: (0, 0)),
        ],
        out_specs=pl.BlockSpec(blk, lambda b: (b, 0)),
        scratch_shapes=[pltpu.VMEM((rel_rows.shape[0], seq, bias_cols), F32)],
        compiler_params=_params("attn_sample", 1),
    )(qk, qk, v, cache_kt, cache_vt, rel_rows)


def _mix_kernel(seq_rows, attn_ref, cb_ref, u_ref, halo_ref, gates_ref, x_ref, wa_ref, wc_ref, wo_ref,
                cw_ref, cbias_ref, g_ref, o_ref):
    tm, d = x_ref.shape
    u = u_ref[...]
    row = lax.broadcasted_iota(jnp.int32, u.shape, 0)
    if len(halo_ref.shape) == 3:
        nseq = halo_ref.shape[0]
        state = halo_ref[...]
        per_row = lambda r: jnp.broadcast_to(state[:, r:r + 1, :], (nseq, seq_rows, u.shape[1])).reshape(u.shape)
        prev2, prev1 = per_row(0), per_row(1)
        row = row % seq_rows
    else:
        keep = ((pl.program_id(0) * tm) % seq_rows != 0).astype(F32)
        nh = halo_ref.shape[0]
        prev2, prev1 = halo_ref[nh - 2:nh - 1, :] * keep, halo_ref[nh - 1:nh, :] * keep
    um1 = jnp.where(row == 0, prev1, pltpu.roll(u, 1, axis=0))
    um2 = jnp.where(row == 0, prev2, jnp.where(row == 1, prev1, pltpu.roll(u, 2, axis=0)))
    conv = cbias_ref[...] + cw_ref[0:1, :] * um2 + cw_ref[1:2, :] * um1 + cw_ref[2:3, :] * u
    ya = jnp.dot(attn_ref[...], wa_ref[...], preferred_element_type=F32)
    yb = jnp.dot((cb_ref[...].astype(F32) * conv).astype(BF16), wc_ref[...], preferred_element_type=F32)
    merged = gates_ref[:, :d].astype(F32) * ya + gates_ref[:, d:].astype(F32) * yb
    mix = jnp.dot(merged.astype(BF16), wo_ref[...], preferred_element_type=F32)
    o_ref[...] = x_ref[...] + _rms(mix, g_ref[...])


def _mix(attn, cb, u, state, gates, x, wa, wc, wo, conv_w, conv_b, g, tm, seq_rows):
    t, d = x.shape
    c = u.shape[1]
    halo_rows = 8
    if state is None:
        halo, halo_spec = u, pl.BlockSpec(
            (halo_rows, c), lambda i: (jnp.maximum(i * (tm // halo_rows) - 1, 0), 0))
    else:
        assert tm % seq_rows == 0
        halo, halo_spec = state, pl.BlockSpec((tm // seq_rows,) + state.shape[1:], lambda i: (i, 0, 0))
    row_spec = lambda w: pl.BlockSpec((tm, w), lambda i: (i, 0))
    resident = lambda a: pl.BlockSpec(a.shape, lambda i: (0,) * a.ndim, pipeline_mode=pl.Buffered(1))
    return pl.pallas_call(
        functools.partial(_mix_kernel, seq_rows),
        out_shape=jax.ShapeDtypeStruct((t, d), F32),
        grid=(t // tm,),
        in_specs=[row_spec(attn.shape[1]), row_spec(c), row_spec(c), halo_spec, row_spec(2 * d), row_spec(d),
                  resident(wa), resident(wc), resident(wo), resident(conv_w), resident(conv_b), resident(g)],
        out_specs=row_spec(d),
        compiler_params=_params("mix", 1),
    )(attn, cb, u, halo, gates, x, wa, wc, wo, conv_w, conv_b, g)


def _mlp_kernel(nj, h_ref, g1_ref, wu_ref, wd_ref, g2_ref, o_ref, n_ref, acc_ref):
    j = pl.program_id(1)

    def step(first, last):
        def body():
            if first:
                n_ref[...] = _rms(h_ref[...], g1_ref[...]).astype(BF16)
            a = jnp.dot(n_ref[...], wu_ref[...], preferred_element_type=F32)
            a = jnp.square(jnp.maximum(a, 0.0)).astype(BF16)
            part = jnp.dot(a, wd_ref[...], preferred_element_type=F32)
            if last:
                f = part if first else acc_ref[...] + part
                o_ref[...] = h_ref[...] + _rms(f, g2_ref[...])
            elif first:
                acc_ref[...] = part
            else:
                acc_ref[...] += part
        return body

    if nj == 1:
        step(True, True)()
    else:
        pl.when(j == 0)(step(True, False))
        pl.when(j == nj - 1)(step(False, True))
        if nj > 2:
            pl.when((j > 0) & (j < nj - 1))(step(False, False))


def _mlp(h, g1, w_up, w_down, g2, tm):
    t, d = h.shape
    tf = MLP_TF
    nj = w_up.shape[1] // tf
    return pl.pallas_call(
        functools.partial(_mlp_kernel, nj),
        out_shape=jax.ShapeDtypeStruct((t, d), F32),
        grid=(t // tm, nj),
        in_specs=[
            pl.BlockSpec((tm, d), lambda i, j: (i, 0)),
            pl.BlockSpec((1, d), lambda i, j: (0, 0)),
            pl.BlockSpec((d, tf), lambda i, j: (0, j)),
            pl.BlockSpec((tf, d), lambda i, j: (j, 0)),
            pl.BlockSpec((1, d), lambda i, j: (0, 0)),
        ],
        out_specs=pl.BlockSpec((tm, d), lambda i, j: (i, 0)),
        scratch_shapes=[pltpu.VMEM((tm, d), BF16), pltpu.VMEM((tm, d), F32)],
        compiler_params=_params("mlp", 2),
    )(h, g1, w_up, w_down, g2)


def _pe_kernel(row_splits, h_ref, p_ref, wg_ref, wp_ref, g_ref, o_ref):
    rows = h_ref.shape[0] // row_splits

    def products(r):
        sl = slice(r * rows, (r + 1) * rows)
        return (jnp.dot(h_ref[sl, :].astype(BF16), wg_ref[...], preferred_element_type=F32),
                jnp.dot(p_ref[sl, :].astype(BF16), wp_ref[...], preferred_element_type=F32))

    pending = products(0)
    for r in range(row_splits):
        gate_pre, pe = pending
        if r + 1 < row_splits:
            pending = products(r + 1)
        sl = slice(r * rows, (r + 1) * rows)
        o_ref[sl, :] = h_ref[sl, :] + _rms(_sigmoid(gate_pre) * pe, g_ref[...])


def _pe(h, p, wg, wp, g, tm):
    t, d = h.shape
    resident = lambda a: pl.BlockSpec(a.shape, lambda i: (0,) * a.ndim, pipeline_mode=pl.Buffered(1))
    return pl.pallas_call(
        functools.partial(_pe_kernel, max(tm // 256, 1)),
        out_shape=jax.ShapeDtypeStruct((t, d), F32),
        grid=(t // tm,),
        in_specs=[pl.BlockSpec((tm, d), lambda i: (i, 0)), pl.BlockSpec((tm, p.shape[1]), lambda i: (i, 0)),
                  resident(wg), resident(wp), resident(g)],
        out_specs=pl.BlockSpec((tm, d), lambda i: (i, 0)),
        compiler_params=_params("pe", 1),
    )(h, p, wg, wp, g)


def _layer(x, pe_in, seq, tiles_per_seq, attn_fn, state, w, tm_proj, tm_mix, tm_mlp, tm_pe):
    t = x.shape[0]
    qk, *kv, cb, u, gates = _in_proj(x, w["g_pre_mix"], w["w_in"], min(tm_proj, t), tiles_per_seq)
    attn, cast_weights = attn_fn(qk, kv[0])
    w = {**w, **cast_weights}
    h = _mix(attn, cb, u, state, gates, x, w["w_attn_out"], w["w_conv_out"], w["w_o"], w["conv_w"],
             w["conv_b"], w["g_post_mix"], min(tm_mix, t), seq)
    h = _mlp(h, w["g_pre_mlp"], w["w_up"], w["w_down"], w["g_post_mlp"], min(tm_mlp, t))
    h = _pe(h, pe_in, w["w_pe_gate"], w["w_pe"], w["g_pe"], min(tm_pe, t))
    return h, kv, u.reshape(t // seq, seq, -1)[:, -2:], w


def kernel(x_prompt, x_sample, cache_k, cache_v, state_conv, p_prompt, p_sample, w_in, rel_table, w_attn_out,
           conv_w, conv_b, w_conv_out, w_o, g_pre_mix, g_post_mix, g_pre_mlp, g_post_mlp, w_up, w_down, w_pe,
           w_pe_gate, g_pe):
    depth = w_in.shape[0]
    bp, sp, d = x_prompt.shape
    bs, ss, _ = x_sample.shape
    lc, heads = cache_k.shape[2], cache_k.shape[3]
    lp = min(LEFT_LEN, sp)
    assert sp % lp == 0 and sp % ATTN_TQ == 0 and lc == LEFT_LEN and ss <= CHUNK
    assert rel_table.shape[2] == 2 * REL_CLIP + 1

    hp = x_prompt.reshape(bp * sp, d)
    hs = x_sample.reshape(bs * ss, d)
    outs = [[] for _ in range(6)]
    for i in range(depth):
        w = {
            "w_in": w_in[i].astype(BF16), "w_attn_out": w_attn_out[i].astype(BF16),
            "w_conv_out": w_conv_out[i].astype(BF16), "w_o": w_o[i].astype(BF16),
            "w_pe": w_pe[i].astype(BF16), "w_pe_gate": w_pe_gate[i].astype(BF16),
            "conv_w": conv_w[i], "conv_b": conv_b[i][None],
            "g_pre_mix": g_pre_mix[i][None], "g_post_mix": g_post_mix[i][None],
            "g_pre_mlp": g_pre_mlp[i][None], "g_post_mlp": g_post_mlp[i][None], "g_pe": g_pe[i][None],
        }
        rel_rows = _rel_rows(rel_table[i])
        kct = cache_k[i].transpose(0, 2, 3, 1).reshape(bs, heads * HEAD_DIM, lc)
        vct = cache_v[i].transpose(0, 2, 3, 1).reshape(bs, heads * HEAD_DIM, lc)

        def attn_prompt(qk, v_t):
            attn, (wu, wd) = _attn_prompt(qk, v_t, rel_rows, bp, sp, [(w_up[i], 1), (w_down[i], 0)])
            return attn, {"w_up": wu, "w_down": wd}

        hp, (_, ktp, vtp), cpr, w = _layer(
            hp, p_prompt[i].reshape(bp * sp, -1), sp, sp // lp, attn_prompt, None, w,
            tm_proj=lp, tm_mix=512, tm_mlp=512, tm_pe=512)
        hs, (_, kvs), csm, _ = _layer(
            hs, p_sample[i].reshape(bs * ss, -1), ss, None,
            lambda qk, v: (_attn_sample(qk, v, kct, vct, rel_rows, bs, ss), {}), state_conv[i], w,
            tm_proj=512, tm_mix=256, tm_mlp=512, tm_pe=512)
        kpr = ktp.reshape(bp, heads, HEAD_DIM, lp).transpose(0, 3, 1, 2)
        vpr = vtp.reshape(bp, heads, HEAD_DIM, lp).transpose(0, 3, 1, 2)
        ksm = kvs[:, :heads * HEAD_DIM].reshape(bs, ss, heads, HEAD_DIM)
        vsm = kvs[:, heads * HEAD_DIM:].reshape(bs, ss, heads, HEAD_DIM)
        for lst, val in zip(outs, (kpr, vpr, cpr, ksm, vsm, csm)):
            lst.append(val)
    k_prompt, v_prompt, conv_prompt, k_sample, v_sample, conv_sample = [jnp.stack(o) for o in outs]
    return (hp.reshape(bp, sp, d), hs.reshape(bs, ss, d), k_prompt, v_prompt, conv_prompt,
            k_sample, v_sample, conv_sample)
```

```python
import functools

import jax
import jax.numpy as jnp
from jax import lax
from jax.experimental import pallas as pl
from jax.experimental.pallas import tpu as pltpu

F32 = jnp.float32
BF16 = jnp.bfloat16

CHUNK = 64
LEFT_CHUNKS = 8
LEFT_LEN = LEFT_CHUNKS * CHUNK
BAND = (LEFT_CHUNKS + 1) * CHUNK
HEAD_DIM = 64
REL_CLIP = 128
EPS = 1e-6
NEG = -1e30
LOG2E = 1.4426950408889634
SCORE_SCALE = HEAD_DIM ** -0.5 * LOG2E

LANES = 128
ATTN_TQ = 256
ATTN_NK = LEFT_LEN + ATTN_TQ
ATTN_SLOTS = ATTN_NK // ATTN_TQ
REL_ROW = ATTN_NK + ATTN_TQ
IN_PROJ_TN = 2048
MLP_TF = 1024
MIB = 1024 * 1024


def _rms(x, g):
    return x * lax.rsqrt(jnp.mean(x * x, axis=-1, keepdims=True) + EPS) * g


def _sigmoid(x):
    return 0.5 * jnp.tanh(0.5 * x) + 0.5


VMEM_MIB = {"in_proj": 56, "attn_prompt": 40, "attn_sample": 32, "mix": 58, "mlp": 56, "pe": 58}


def _params(call, grid_rank):
    return pltpu.CompilerParams(dimension_semantics=("arbitrary",) * grid_rank,
                                vmem_limit_bytes=VMEM_MIB[call] * MIB)


STEP_QK = 2


def _in_proj_kernel(tiles_per_seq, x_ref, g_ref, w_ref, qk_ref, *refs):
    if tiles_per_seq is None:
        v_ref, kv_ref, cb_ref, u_ref, gates_ref, n_ref = refs
    else:
        v_ref, kt_ref, vt_ref, cb_ref, u_ref, gates_ref, n_ref = refs
        is_tail = pl.program_id(0) % tiles_per_seq == tiles_per_seq - 1
    j = pl.program_id(1)
    half = w_ref.shape[1] // 2

    def proj():
        z = jnp.dot(n_ref[...], w_ref[...], preferred_element_type=F32)
        return z[:, :half], z[:, half:]

    def step(jj, tail, body):
        cond = j == jj
        if tail is not None and tiles_per_seq is not None:
            cond = cond & (is_tail if tail else jnp.logical_not(is_tail))
        pl.when(cond)(body)

    def q_k(tail):
        def body():
            q, k = proj()
            qk_ref[:, :half] = (q * SCORE_SCALE).astype(BF16)
            qk_ref[:, half:] = k.astype(BF16)
            if tiles_per_seq is None:
                kv_ref[:, :half] = k
            elif tail:
                kt_ref[...] = k.T
        return body

    def v_bg(tail):
        def body():
            v, bg = proj()
            cb_ref[...] = bg.astype(BF16)
            if tiles_per_seq is None:
                v_ref[...] = v.astype(BF16)
                kv_ref[:, half:] = v
            else:
                vt = v.T
                v_ref[...] = vt.astype(BF16)
                if tail:
                    vt_ref[...] = vt
        return body

    def conv_in():
        cg, xin = proj()
        u_ref[...] = cg * xin

    def gate(first):
        def body():
            if first:
                n_ref[...] = _rms(x_ref[...], g_ref[...]).astype(BF16)
            za, zb = proj()
            gates_ref[:, :half] = _sigmoid(za).astype(BF16)
            gates_ref[:, half:] = _sigmoid(zb).astype(BF16)
        return body

    step(0, None, gate(True))
    pl.when((j > 0) & (j < STEP_QK))(gate(False))
    for tail in ((None,) if tiles_per_seq is None else (False, True)):
        step(STEP_QK, tail, q_k(tail))
        step(STEP_QK + 1, tail, v_bg(tail))
    step(STEP_QK + 2, None, conv_in)


def _in_proj(x, g, w_in, tm, tiles_per_seq):
    t, d = x.shape
    tn = IN_PROJ_TN
    nj = w_in.shape[1] // tn
    half = tn // 2
    assert nj == 5 and d == tn and t % tm == 0

    rows = lambda w: pl.BlockSpec((tm, w), lambda i, j: (i, 0))
    if tiles_per_seq is None:
        kv_shapes = (jax.ShapeDtypeStruct((t, half), BF16),
                     jax.ShapeDtypeStruct((t, tn), F32))
        kv_specs = (rows(half), rows(tn))
    else:
        nseq = t // (tm * tiles_per_seq)
        tail_spec = pl.BlockSpec((None, half, tm), lambda i, j: (i // tiles_per_seq, 0, 0))
        kv_shapes = (jax.ShapeDtypeStruct((half, t), BF16),
                     jax.ShapeDtypeStruct((nseq, half, tm), F32),
                     jax.ShapeDtypeStruct((nseq, half, tm), F32))
        kv_specs = (pl.BlockSpec((half, tm), lambda i, j: (0, i)), tail_spec, tail_spec)
    out_shape = (
        jax.ShapeDtypeStruct((t, tn), BF16),
        *kv_shapes,
        jax.ShapeDtypeStruct((t, half), BF16),
        jax.ShapeDtypeStruct((t, half), F32),
        jax.ShapeDtypeStruct((t, 2 * tn), BF16),
    )
    return pl.pallas_call(
        functools.partial(_in_proj_kernel, tiles_per_seq),
        out_shape=out_shape,
        grid=(t // tm, nj),
        in_specs=[
            pl.BlockSpec((tm, d), lambda i, j: (i, 0)),
            pl.BlockSpec((1, d), lambda i, j: (0, 0)),
            pl.BlockSpec((d, tn), lambda i, j: (0, (j + nj - STEP_QK) % nj)),
        ],
        out_specs=(
            rows(tn),
            *kv_specs,
            rows(half),
            rows(half),
            pl.BlockSpec((tm, tn), lambda i, j: (i, jnp.minimum(j, STEP_QK - 1))),
        ),
        scratch_shapes=[pltpu.VMEM((tm, d), BF16)],
        compiler_params=_params("in_proj", 2),
    )(x, g, w_in)


def _rel_rows(rel_table):
    h, n_rel = rel_table.shape
    lo = LEFT_LEN - REL_CLIP
    hi = ATTN_NK - lo - n_rel
    first = jnp.broadcast_to(rel_table[:, :1], (h, lo))
    last = jnp.broadcast_to(rel_table[:, -1:], (h, hi))
    negative = jnp.broadcast_to(rel_table[:, :1], (h, REL_ROW - ATTN_NK))
    rows = jnp.concatenate([first, rel_table, last, negative], axis=1).astype(F32)
    return (rows - rows[:, :1]) * LOG2E


def _toeplitz(row, nq):
    return pltpu.roll(jnp.broadcast_to(row, (nq, row.shape[1])), 0, 1, stride=1, stride_axis=0)


def _attn_prompt_kernel(n_casts, q_ref, kp_ref, kc_ref, vtp_ref, vtc_ref, rel_ref, *refs):
    cast_in, (o_ref, *cast_out, bias_ref) = refs[:n_casts], refs[n_casts:]
    t = pl.program_id(1)

    for src, dst in zip(cast_in, cast_out):
        dst[...] = src[...].astype(BF16)

    @pl.when((pl.program_id(0) == 0) & (t == 0))
    def _():
        qi = lax.broadcasted_iota(jnp.int32, (ATTN_TQ, ATTN_NK), 0)
        kj = lax.broadcasted_iota(jnp.int32, (ATTN_TQ, ATTN_NK), 1)
        first = (qi // CHUNK) * CHUNK
        visible = (kj >= first) & (kj < first + BAND)
        for h in range(bias_ref.shape[0]):
            bias_ref[h] = jnp.where(visible, _toeplitz(rel_ref[h:h + 1, :], ATTN_TQ)[:, :ATTN_NK], NEG).T

    tiles = q_ref.shape[0] // ATTN_TQ
    prev_slots = kp_ref.shape[0] // ATTN_TQ

    def key_slot(s):
        k_ref, vt_ref = (kp_ref, vtp_ref) if s < prev_slots else (kc_ref, vtc_ref)
        r0 = (s % prev_slots) * ATTN_TQ
        return k_ref.at[pl.ds(r0, ATTN_TQ)], vt_ref.at[:, pl.ds(r0, ATTN_TQ)]

    for s in range(tiles):
        g = t * tiles + s
        rows = pl.ds(s * ATTN_TQ, ATTN_TQ)
        for first_slot in range(ATTN_SLOTS):
            tiles_before = ATTN_SLOTS - 1 - first_slot
            if first_slot and (tiles_before < s or (tiles_before - s) % tiles):
                continue
            cond = (g >= tiles_before) if first_slot == 0 else (g == tiles_before)

            @pl.when(cond)
            def _(s=s, rows=rows, first_slot=first_slot):
                slots = [key_slot(s + prev_slots - (ATTN_SLOTS - 1) + n) for n in range(first_slot, ATTN_SLOTS)]
                _band_heads(q_ref.at[rows], [k for k, _ in slots], [v for _, v in slots], bias_ref,
                            first_slot * ATTN_TQ, o_ref.at[rows])


def _band_heads(q_ref, k_refs, vt_refs, bias_ref, key0, o_ref):
    nk = len(k_refs) * ATTN_TQ
    heads = 2 * (q_ref.shape[1] // LANES)
    lane = lax.broadcasted_iota(jnp.int32, (ATTN_TQ, LANES), 1)
    feat = lax.broadcasted_iota(jnp.int32, (LANES, nk), 0)

    def scores(h):
        sl = slice(h // 2 * LANES, (h // 2 + 1) * LANES)
        own_lane = (lane >= HEAD_DIM) if h % 2 else (lane < HEAD_DIM)
        q_pair = q_ref[:, sl]
        qm = jnp.where(own_lane, q_pair, jnp.zeros_like(q_pair))
        keys = jnp.concatenate([r[:, sl] for r in k_refs], axis=0)
        return lax.dot_general(keys, qm, (((1,), (1,)), ((), ())), preferred_element_type=F32)

    def probabilities(h, st):
        groups = []
        for c0 in range(0, ATTN_TQ, LANES):
            first = c0 // CHUNK * CHUNK
            last = (c0 + LANES - CHUNK) // CHUNK * CHUNK + BAND
            lo, hi = max(first - key0, 0), min(last - key0, nk)
            z0 = min(max(first + CHUNK - key0, lo), hi)
            z1 = min(max(c0 + LEFT_LEN - REL_CLIP - key0, z0), hi)
            cols = slice(c0, c0 + LANES)
            biased = lambda a, b: st[a:b, cols] + bias_ref[h, key0 + a:key0 + b, cols]
            parts = ([biased(lo, z0)] if z0 > lo else []) + ([st[z0:z1, cols]] if z1 > z0 else []) \
                + ([biased(z1, hi)] if hi > z1 else [])
            s = jnp.concatenate(parts, axis=0)
            p = jnp.exp2(s - jnp.max(s, axis=0, keepdims=True)).astype(BF16)
            pads = [jnp.zeros((n, LANES), BF16) for n in (lo, nk - hi)]
            groups.append(jnp.concatenate([x for x in (pads[0], p, pads[1]) if x.shape[0]], axis=0))
        return jnp.concatenate(groups, axis=1)

    def weighted_values(h, st):
        sl = slice(h // 2 * LANES, (h // 2 + 1) * LANES)
        own_feat = (feat >= HEAD_DIM) if h % 2 else (feat < HEAD_DIM)
        pt = probabilities(h, st)
        vals_t = jnp.concatenate([r[sl, :] for r in vt_refs], axis=1)
        vm = jnp.where(own_feat, vals_t, jnp.ones_like(vals_t))
        acc = jnp.dot(vm, pt, preferred_element_type=F32)
        lo, hi = acc[:HEAD_DIM], acc[HEAD_DIM:]
        return hi / lo if h % 2 else lo / hi

    halves = []
    ahead = 3
    pending = [scores(h) for h in range(ahead)]
    for h in range(heads):
        if h + ahead < heads:
            pending.append(scores(h + ahead))
        halves.append(weighted_values(h, pending.pop(0)))
        if h % 2:
            sl = slice(h // 2 * LANES, (h // 2 + 1) * LANES)
            o_ref[:, sl] = jnp.concatenate(halves, axis=0).T.astype(o_ref.dtype)
            halves = []


def _attn_prompt(qk, v_t, rel_rows, batch, seq, casts):
    t, width = v_t.shape[1], v_t.shape[0]
    step = LEFT_LEN
    assert seq % step == 0 and step % ATTN_TQ == 0
    nt = seq // step
    blk = (step, width)
    here = lambda b, i: b * nt + i
    before = lambda b, i: b * nt + jnp.maximum(i - 1, 0)

    steps = batch * nt
    cast_specs = []
    for a, axis in casts:
        assert a.ndim == 2 and a.shape[axis] % steps == 0
        shape = tuple(n // steps if ax == axis else n for ax, n in enumerate(a.shape))
        cast_specs.append(pl.BlockSpec(
            shape, lambda b, i, axis=axis: tuple(here(b, i) if ax == axis else 0 for ax in range(2))))
    outs = pl.pallas_call(
        functools.partial(_attn_prompt_kernel, len(casts)),
        out_shape=(jax.ShapeDtypeStruct((t, width), BF16),
                   *[jax.ShapeDtypeStruct(a.shape, BF16) for a, _ in casts]),
        grid=(batch, nt),
        in_specs=[
            pl.BlockSpec(blk, lambda b, i: (here(b, i), 0)),
            pl.BlockSpec(blk, lambda b, i: (before(b, i), 1)),
            pl.BlockSpec(blk, lambda b, i: (here(b, i), 1)),
            pl.BlockSpec((width, step), lambda b, i: (0, before(b, i))),
            pl.BlockSpec((width, step), lambda b, i: (0, here(b, i))),
            pl.BlockSpec(rel_rows.shape, lambda b, i: (0, 0)),
            *cast_specs,
        ],
        out_specs=(pl.BlockSpec(blk, lambda b, i: (here(b, i), 0)), *cast_specs),
        scratch_shapes=[pltpu.VMEM((rel_rows.shape[0], ATTN_NK, ATTN_TQ), F32)],
        compiler_params=_params("attn_prompt", 2),
    )(qk, qk, qk, v_t, v_t, rel_rows, *[a for a, _ in casts])
    return outs[0], outs[1:]


def _attn_sample_kernel(q_ref, kn_ref, vn_ref, kct_ref, vct_ref, rel_ref, o_ref, bias_ref):
    nq = q_ref.shape[0]
    lc = kct_ref.shape[1]

    @pl.when(pl.program_id(0) == 0)
    def _():
        for h in range(bias_ref.shape[0]):
            bias_ref[h] = _toeplitz(rel_ref[h:h + 1, :], nq)[:, :bias_ref.shape[2]]

    lane = lax.broadcasted_iota(jnp.int32, (nq, LANES), 1)
    nt_dims = (((1,), (1,)), ((), ()))

    def scores(h):
        sl = slice(h // 2 * LANES, (h // 2 + 1) * LANES)
        own = (lane >= HEAD_DIM) if h % 2 else (lane < HEAD_DIM)
        qm = jnp.where(own, q_ref[:, sl], jnp.zeros((nq, LANES), BF16))
        s_cache = jnp.dot(qm, kct_ref[sl, :].astype(BF16), preferred_element_type=F32)
        s_new = lax.dot_general(qm, kn_ref[:, sl], nt_dims, preferred_element_type=F32)
        return s_cache + bias_ref[h, :, :lc], s_new + bias_ref[h, :, lc:lc + nq]

    def weighted_values(h, s_cache, s_new):
        sl = slice(h // 2 * LANES, (h // 2 + 1) * LANES)
        m = jnp.maximum(jnp.max(s_cache, axis=-1, keepdims=True), jnp.max(s_new, axis=-1, keepdims=True))
        p_cache, p_new = jnp.exp2(s_cache - m), jnp.exp2(s_new - m)
        denom = jnp.sum(p_cache, axis=-1, keepdims=True) + jnp.sum(p_new, axis=-1, keepdims=True)
        acc = lax.dot_general(p_cache.astype(BF16), vct_ref[sl, :].astype(BF16), nt_dims,
                              preferred_element_type=F32)
        acc = acc + jnp.dot(p_new.astype(BF16), vn_ref[:, sl], preferred_element_type=F32)
        return acc / denom

    heads = bias_ref.shape[0]
    ahead = 3
    pending = [scores(h) for h in range(ahead)]
    for h in range(heads):
        if h + ahead < heads:
            pending.append(scores(h + ahead))
        out = weighted_values(h, *pending.pop(0))
        if h % 2:
            sl = slice(h // 2 * LANES, (h // 2 + 1) * LANES)
            o_ref[:, sl] = jnp.where(lane < HEAD_DIM, prev_out, out).astype(o_ref.dtype)
        prev_out = out


def _attn_sample(qk, v, cache_kt, cache_vt, rel_rows, batch, seq):
    t, width = v.shape
    lc = cache_kt.shape[2]
    blk = (seq, width)
    bias_cols = -(-(lc + seq) // LANES) * LANES
    return pl.pallas_call(
        _attn_sample_kernel,
        out_shape=jax.ShapeDtypeStruct((t, width), BF16),
        grid=(batch,),
        in_specs=[
            pl.BlockSpec(blk, lambda b: (b, 0)),
            pl.BlockSpec(blk, lambda b: (b, 1)),
            pl.BlockSpec(blk, lambda b: (b, 0)),
            pl.BlockSpec((None, width, lc), lambda b: (b, 0, 0)),
            pl.BlockSpec((None, width, lc), lambda b: (b, 0, 0)),
            pl.BlockSpec(rel_rows.shape, lambda b: (0, 0)),
        ],
        out_specs=pl.BlockSpec(blk, lambda b: (b, 0)),
        scratch_shapes=[pltpu.VMEM((rel_rows.shape[0], seq, bias_cols), F32)],
        compiler_params=_params("attn_sample", 1),
    )(qk, qk, v, cache_kt, cache_vt, rel_rows)


def _mix_kernel(seq_rows, attn_ref, cb_ref, u_ref, halo_ref, gates_ref, x_ref, wa_ref, wc_ref, wo_ref,
                cw_ref, cbias_ref, g_ref, o_ref):
    tm, d = x_ref.shape
    u = u_ref[...]
    row = lax.broadcasted_iota(jnp.int32, u.shape, 0)
    if len(halo_ref.shape) == 3:
        nseq = halo_ref.shape[0]
        state = halo_ref[...]
        per_row = lambda r: jnp.broadcast_to(state[:, r:r + 1, :], (nseq, seq_rows, u.shape[1])).reshape(u.shape)
        prev2, prev1 = per_row(0), per_row(1)
        row = row % seq_rows
    else:
        keep = ((pl.program_id(0) * tm) % seq_rows != 0).astype(F32)
        nh = halo_ref.shape[0]
        prev2, prev1 = halo_ref[nh - 2:nh - 1, :] * keep, halo_ref[nh - 1:nh, :] * keep
    um1 = jnp.where(row == 0, prev1, pltpu.roll(u, 1, axis=0))
    um2 = jnp.where(row == 0, prev2, jnp.where(row == 1, prev1, pltpu.roll(u, 2, axis=0)))
    conv = cbias_ref[...] + cw_ref[0:1, :] * um2 + cw_ref[1:2, :] * um1 + cw_ref[2:3, :] * u
    ya = jnp.dot(attn_ref[...], wa_ref[...], preferred_element_type=F32)
    yb = jnp.dot((cb_ref[...].astype(F32) * conv).astype(BF16), wc_ref[...], preferred_element_type=F32)
    merged = gates_ref[:, :d].astype(F32) * ya + gates_ref[:, d:].astype(F32) * yb
    mix = jnp.dot(merged.astype(BF16), wo_ref[...], preferred_element_type=F32)
    o_ref[...] = x_ref[...] + _rms(mix, g_ref[...])


def _mix(attn, cb, u, state, gates, x, wa, wc, wo, conv_w, conv_b, g, tm, seq_rows):
    t, d = x.shape
    c = u.shape[1]
    halo_rows = 8
    if state is None:
        halo, halo_spec = u, pl.BlockSpec(
            (halo_rows, c), lambda i: (jnp.maximum(i * (tm // halo_rows) - 1, 0), 0))
    else:
        assert tm % seq_rows == 0
        halo, halo_spec = state, pl.BlockSpec((tm // seq_rows,) + state.shape[1:], lambda i: (i, 0, 0))
    row_spec = lambda w: pl.BlockSpec((tm, w), lambda i: (i, 0))
    resident = lambda a: pl.BlockSpec(a.shape, lambda i: (0,) * a.ndim, pipeline_mode=pl.Buffered(1))
    return pl.pallas_call(
        functools.partial(_mix_kernel, seq_rows),
        out_shape=jax.ShapeDtypeStruct((t, d), F32),
        grid=(t // tm,),
        in_specs=[row_spec(attn.shape[1]), row_spec(c), row_spec(c), halo_spec, row_spec(2 * d), row_spec(d),
                  resident(wa), resident(wc), resident(wo), resident(conv_w), resident(conv_b), resident(g)],
        out_specs=row_spec(d),
        compiler_params=_params("mix", 1),
    )(attn, cb, u, halo, gates, x, wa, wc, wo, conv_w, conv_b, g)


def _mlp_kernel(nj, h_ref, g1_ref, wu_ref, wd_ref, g2_ref, o_ref, n_ref, acc_ref):
    j = pl.program_id(1)

    def step(first, last):
        def body():
            if first:
                n_ref[...] = _rms(h_ref[...], g1_ref[...]).astype(BF16)
            a = jnp.dot(n_ref[...], wu_ref[...], preferred_element_type=F32)
            a = jnp.square(jnp.maximum(a, 0.0)).astype(BF16)
            part = jnp.dot(a, wd_ref[...], preferred_element_type=F32)
            if last:
                f = part if first else acc_ref[...] + part
                o_ref[...] = h_ref[...] + _rms(f, g2_ref[...])
            elif first:
                acc_ref[...] = part
            else:
                acc_ref[...] += part
        return body

    if nj == 1:
        step(True, True)()
    else:
        pl.when(j == 0)(step(True, False))
        pl.when(j == nj - 1)(step(False, True))
        if nj > 2:
            pl.when((j > 0) & (j < nj - 1))(step(False, False))


def _mlp(h, g1, w_up, w_down, g2, tm):
    t, d = h.shape
    tf = MLP_TF
    nj = w_up.shape[1] // tf
    return pl.pallas_call(
        functools.partial(_mlp_kernel, nj),
        out_shape=jax.ShapeDtypeStruct((t, d), F32),
        grid=(t // tm, nj),
        in_specs=[
            pl.BlockSpec((tm, d), lambda i, j: (i, 0)),
            pl.BlockSpec((1, d), lambda i, j: (0, 0)),
            pl.BlockSpec((d, tf), lambda i, j: (0, j)),
            pl.BlockSpec((tf, d), lambda i, j: (j, 0)),
            pl.BlockSpec((1, d), lambda i, j: (0, 0)),
        ],
        out_specs=pl.BlockSpec((tm, d), lambda i, j: (i, 0)),
        scratch_shapes=[pltpu.VMEM((tm, d), BF16), pltpu.VMEM((tm, d), F32)],
        compiler_params=_params("mlp", 2),
    )(h, g1, w_up, w_down, g2)


def _pe_kernel(row_splits, h_ref, p_ref, wg_ref, wp_ref, g_ref, o_ref):
    rows = h_ref.shape[0] // row_splits

    def products(r):
        sl = slice(r * rows, (r + 1) * rows)
        return (jnp.dot(h_ref[sl, :].astype(BF16), wg_ref[...], preferred_element_type=F32),
                jnp.dot(p_ref[sl, :].astype(BF16), wp_ref[...], preferred_element_type=F32))

    pending = products(0)
    for r in range(row_splits):
        gate_pre, pe = pending
        if r + 1 < row_splits:
            pending = products(r + 1)
        sl = slice(r * rows, (r + 1) * rows)
        o_ref[sl, :] = h_ref[sl, :] + _rms(_sigmoid(gate_pre) * pe, g_ref[...])


def _pe(h, p, wg, wp, g, tm):
    t, d = h.shape
    resident = lambda a: pl.BlockSpec(a.shape, lambda i: (0,) * a.ndim, pipeline_mode=pl.Buffered(1))
    return pl.pallas_call(
        functools.partial(_pe_kernel, max(tm // 256, 1)),
        out_shape=jax.ShapeDtypeStruct((t, d), F32),
        grid=(t // tm,),
        in_specs=[pl.BlockSpec((tm, d), lambda i: (i, 0)), pl.BlockSpec((tm, p.shape[1]), lambda i: (i, 0)),
                  resident(wg), resident(wp), resident(g)],
        out_specs=pl.BlockSpec((tm, d), lambda i: (i, 0)),
        compiler_params=_params("pe", 1),
    )(h, p, wg, wp, g)


def _layer(x, pe_in, seq, tiles_per_seq, attn_fn, state, w, tm_proj, tm_mix, tm_mlp, tm_pe):
    t = x.shape[0]
    qk, *kv, cb, u, gates = _in_proj(x, w["g_pre_mix"], w["w_in"], min(tm_proj, t), tiles_per_seq)
    attn, cast_weights = attn_fn(qk, kv[0])
    w = {**w, **cast_weights}
    h = _mix(attn, cb, u, state, gates, x, w["w_attn_out"], w["w_conv_out"], w["w_o"], w["conv_w"],
             w["conv_b"], w["g_post_mix"], min(tm_mix, t), seq)
    h = _mlp(h, w["g_pre_mlp"], w["w_up"], w["w_down"], w["g_post_mlp"], min(tm_mlp, t))
    h = _pe(h, pe_in, w["w_pe_gate"], w["w_pe"], w["g_pe"], min(tm_pe, t))
    return h, kv, u.reshape(t // seq, seq, -1)[:, -2:], w


def kernel(x_prompt, x_sample, cache_k, cache_v, state_conv, p_prompt, p_sample, w_in, rel_table, w_attn_out,
           conv_w, conv_b, w_conv_out, w_o, g_pre_mix, g_post_mix, g_pre_mlp, g_post_mlp, w_up, w_down, w_pe,
           w_pe_gate, g_pe):
    depth = w_in.shape[0]
    bp, sp, d = x_prompt.shape
    bs, ss, _ = x_sample.shape
    lc, heads = cache_k.shape[2], cache_k.shape[3]
    lp = min(LEFT_LEN, sp)
    assert sp % lp == 0 and sp % ATTN_TQ == 0 and lc == LEFT_LEN and ss <= CHUNK
    assert rel_table.shape[2] == 2 * REL_CLIP + 1

    hp = x_prompt.reshape(bp * sp, d)
    hs = x_sample.reshape(bs * ss, d)
    outs = [[] for _ in range(6)]
    for i in range(depth):
        w = {
            "w_in": w_in[i].astype(BF16), "w_attn_out": w_attn_out[i].astype(BF16),
            "w_conv_out": w_conv_out[i].astype(BF16), "w_o": w_o[i].astype(BF16),
            "w_pe": w_pe[i].astype(BF16), "w_pe_gate": w_pe_gate[i].astype(BF16),
            "conv_w": conv_w[i], "conv_b": conv_b[i][None],
            "g_pre_mix": g_pre_mix[i][None], "g_post_mix": g_post_mix[i][None],
            "g_pre_mlp": g_pre_mlp[i][None], "g_post_mlp": g_post_mlp[i][None], "g_pe": g_pe[i][None],
        }
        rel_rows = _rel_rows(rel_table[i])
        kct = cache_k[i].transpose(0, 2, 3, 1).reshape(bs, heads * HEAD_DIM, lc)
        vct = cache_v[i].transpose(0, 2, 3, 1).reshape(bs, heads * HEAD_DIM, lc)

        def attn_prompt(qk, v_t):
            attn, (wu, wd) = _attn_prompt(qk, v_t, rel_rows, bp, sp, [(w_up[i], 1), (w_down[i], 0)])
            return attn, {"w_up": wu, "w_down": wd}

        hp, (_, ktp, vtp), cpr, w = _layer(
            hp, p_prompt[i].reshape(bp * sp, -1), sp, sp // lp, attn_prompt, None, w,
            tm_proj=lp, tm_mix=512, tm_mlp=512, tm_pe=1024)
        hs, (_, kvs), csm, _ = _layer(
            hs, p_sample[i].reshape(bs * ss, -1), ss, None,
            lambda qk, v: (_attn_sample(qk, v, kct, vct, rel_rows, bs, ss), {}), state_conv[i], w,
            tm_proj=512, tm_mix=256, tm_mlp=512, tm_pe=512)
        kpr = ktp.reshape(bp, heads, HEAD_DIM, lp).transpose(0, 3, 1, 2)
        vpr = vtp.reshape(bp, heads, HEAD_DIM, lp).transpose(0, 3, 1, 2)
        ksm = kvs[:, :heads * HEAD_DIM].reshape(bs, ss, heads, HEAD_DIM)
        vsm = kvs[:, heads * HEAD_DIM:].reshape(bs, ss, heads, HEAD_DIM)
        for lst, val in zip(outs, (kpr, vpr, cpr, ksm, vsm, csm)):
            lst.append(val)
    k_prompt, v_prompt, conv_prompt, k_sample, v_sample, conv_sample = [jnp.stack(o) for o in outs]
    return (hp.reshape(bp, sp, d), hs.reshape(bs, ss, d), k_prompt, v_prompt, conv_prompt,
            k_sample, v_sample, conv_sample)
```
